```python
import math
import jax, jax.numpy as jnp
from jax import lax
import numpy as np

D_MODEL = 1024
BATCH = 8
SEQ = 2048
DEPTH = 2
DEC_BATCH = 128
DEC_SEQ = 8
PAST_LEN = 16384
PAGE_SIZE = 128

D_MIX = D_MODEL
DN_HEADS = 4
DN_DK = 128
DN_DV = 128
DN_QK = DN_HEADS * DN_DK
DN_V = DN_HEADS * DN_DV
DN_CONV = 4
DN_CHUNK = 64
QKV_DIM = 2 * DN_QK + DN_V
SG_GROUPS = 4
SG_CH = (D_MIX - DN_V) // SG_GROUPS
SG_WIDTH = SG_GROUPS * SG_CH
SG_CHUNK = 128
P_IN = QKV_DIM + DN_V + 2 * DN_HEADS + 2 * SG_WIDTH
MEM_LEN = 256
MEM_HEADS = 4
MEM_HD = D_MODEL // MEM_HEADS
PEER_HEADS = 8
PEER_NKEYS = 128
PEER_NEXP = PEER_NKEYS * PEER_NKEYS
PEER_DKEY = 128
PEER_TOPK = 16
PEER_BLOCK = 128
EPS = 1e-6

kernel_name = 'hymba_deltanet_sgu_peer_step'

F32 = jnp.float32


def _rmsnorm(x, g):
    xf = x.astype(F32)
    y = xf * lax.rsqrt(jnp.mean(xf * xf, axis=-1, keepdims=True) + EPS)
    return (y * g.astype(F32)).astype(x.dtype)


def _layernorm(x, g, b):
    xf = x.astype(F32)
    mu = jnp.mean(xf, axis=-1, keepdims=True)
    var = jnp.mean(jnp.square(xf - mu), axis=-1, keepdims=True)
    y = (xf - mu) * lax.rsqrt(var + EPS)
    return (y * g.astype(F32) + b.astype(F32)).astype(x.dtype)


def _l2norm(x):
    xf = x.astype(F32)
    return xf * lax.rsqrt(jnp.sum(xf * xf, axis=-1, keepdims=True) + EPS)


def _short_conv(x, buf, w):
    T = x.shape[1]
    xx = jnp.concatenate([buf.astype(x.dtype), x], axis=1)
    y = sum(xx[:, j:j + T, :] * w[j] for j in range(DN_CONV))
    return jax.nn.silu(y), xx[:, T:, :]


def _gated_delta(q, k, v, beta, g, S0):
    B, T = q.shape[0], q.shape[1]
    C = min(DN_CHUNK, T)
    n = -(-T // C)
    pad = n * C - T

    def prep(a):
        a = jnp.moveaxis(a.astype(F32), 1, 2)
        a = jnp.pad(a, [(0, 0), (0, 0), (0, pad)] + [(0, 0)] * (a.ndim - 3))
        return a.reshape(a.shape[:2] + (n, C) + a.shape[3:])

    q, k, v, beta, g = prep(q), prep(k), prep(v), prep(beta), prep(g)
    gc = jnp.cumsum(g, axis=-1)
    idx = jnp.arange(C)
    strict = idx[:, None] > idx[None, :]
    causal = idx[:, None] >= idx[None, :]
    diff = gc[..., :, None] - gc[..., None, :]
    kb = k * beta[..., None]
    A = jnp.einsum('bhncd,bhnsd->bhncs', kb, k) * jnp.exp(jnp.where(strict, diff, -jnp.inf))
    u = lax.linalg.triangular_solve(A, v * beta[..., None], left_side=True, lower=True, unit_diagonal=True)
    w = lax.linalg.triangular_solve(A, kb * jnp.exp(gc)[..., None], left_side=True, lower=True, unit_diagonal=True)
    qk = jnp.einsum('bhncd,bhnsd->bhncs', q, k) * jnp.exp(jnp.where(causal, diff, -jnp.inf))

    def step(S, xs):
        qn, kn, un, wn, gn, qkn = xs
        v_new = un - jnp.einsum('bhcd,bhde->bhce', wn, S)
        o = jnp.einsum('bhcd,bhde->bhce', qn * jnp.exp(gn)[..., None], S) + jnp.einsum('bhcs,bhse->bhce', qkn, v_new)
        glast = gn[..., -1]
        S = S * jnp.exp(glast)[..., None, None] + jnp.einsum(
            'bhcd,bhce->bhde', kn * jnp.exp(glast[..., None] - gn)[..., None], v_new)
        return S, o

    xs = tuple(jnp.moveaxis(a, 2, 0) for a in (q, k, u, w, gc, qk))
    S, o = lax.scan(step, S0.astype(F32), xs)
    o = jnp.moveaxis(o, 0, 2).reshape(B, DN_HEADS, n * C, DN_DV)[:, :, :T]
    return jnp.moveaxis(o, 1, 2), S


def _chunk_spatial_mix(v, w, b):
    B, T = v.shape[0], v.shape[1]
    nc = -(-T // SG_CHUNK)
    pad = nc * SG_CHUNK - T
    vp = jnp.pad(v, ((0, 0), (0, pad), (0, 0), (0, 0))).reshape(B, nc, SG_CHUNK, SG_GROUPS, SG_CH)
    tril = jnp.tril(jnp.ones((SG_CHUNK, SG_CHUNK), dtype=bool))
    wm = jnp.where(tril[None], w, jnp.zeros((), w.dtype))
    out = jnp.einsum('gts,bnsgc->bntgc', wm, vp) + b.T[None, None, :, :, None]
    return out.reshape(B, nc * SG_CHUNK, SG_GROUPS, SG_CH)[:, :T]


def _hybrid_mixer(n, S0, c0, w_in, conv_w, a_log, dt_bias, o_norm_g, sg_ln_g, sg_ln_b, sg_w, sg_b, w_out):
    B, T, _ = n.shape
    proj = n @ w_in
    o1 = QKV_DIM
    o2 = o1 + DN_V
    o3 = o2 + DN_HEADS
    o4 = o3 + DN_HEADS
    o5 = o4 + SG_WIDTH
    qkv, z, b_a, a_a, gu, gv = jnp.split(proj, [o1, o2, o3, o4, o5], axis=-1)
    qkv, conv_new = _short_conv(qkv, c0, conv_w)
    q, k, v = jnp.split(qkv, [DN_QK, 2 * DN_QK], axis=-1)
    q = _l2norm(q.reshape(B, T, DN_HEADS, DN_DK)) * (DN_DK ** -0.5)
    k = _l2norm(k.reshape(B, T, DN_HEADS, DN_DK))
    v = v.reshape(B, T, DN_HEADS, DN_DV)
    beta = jax.nn.sigmoid(b_a.astype(F32))
    g = -jnp.exp(a_log.astype(F32)) * jax.nn.softplus(a_a.astype(F32) + dt_bias.astype(F32))
    o, S = _gated_delta(q, k, v, beta, g, S0)
    o = _rmsnorm(o, o_norm_g) * jax.nn.silu(z.reshape(B, T, DN_HEADS, DN_DV).astype(F32))
    o_a = o.reshape(B, T, DN_V).astype(n.dtype)
    u = jax.nn.gelu(gu).reshape(B, T, SG_GROUPS, SG_CH)
    vv = _layernorm(jax.nn.gelu(gv).reshape(B, T, SG_GROUPS, SG_CH), sg_ln_g, sg_ln_b)
    o_b = (u * _chunk_spatial_mix(vv, sg_w, sg_b)).reshape(B, T, SG_WIDTH)
    y = jnp.concatenate([o_a, o_b.astype(n.dtype)], axis=-1) @ w_out
    return y, S, conv_new, vv.reshape(B, T, SG_WIDTH)


def _mem_kv(mem, g, wk, wv):
    m = _rmsnorm(mem, g)
    B, M = mem.shape[0], mem.shape[1]
    return (m @ wk).reshape(B, M, MEM_HEADS, MEM_HD), (m @ wv).reshape(B, M, MEM_HEADS, MEM_HD)


def _mem_attend(n, mk, mv, wq, wo):
    B, T, _ = n.shape
    q = (n @ wq).reshape(B, T, MEM_HEADS, MEM_HD)
    s = jnp.einsum('bthd,bmhd->bhtm', q, mk.astype(q.dtype)).astype(F32) * (MEM_HD ** -0.5)
    p = jax.nn.softmax(s, axis=-1).astype(n.dtype)
    o = jnp.einsum('bhtm,bmhd->bthd', p, mv.astype(n.dtype)).reshape(B, T, D_MODEL)
    return o @ wo


def _peer(n, wq, keys, u_tab, v_tab):
    B, T, D = n.shape
    N = B * T
    x = n.reshape(N, D)
    q = (x @ wq).reshape(N, PEER_HEADS, 2, PEER_DKEY // 2)
    s = jnp.einsum('thpd,hpkd->thpk', q, keys).astype(F32)
    sv, si = lax.top_k(s, PEER_TOPK)
    cand = sv[:, :, 0, :, None] + sv[:, :, 1, None, :]
    cv, ci = lax.top_k(cand.reshape(N, PEER_HEADS, PEER_TOPK * PEER_TOPK), PEER_TOPK)
    i1 = jnp.take_along_axis(si[:, :, 0], ci // PEER_TOPK, axis=-1)
    i2 = jnp.take_along_axis(si[:, :, 1], ci % PEER_TOPK, axis=-1)
    eidx = (i1 * PEER_NKEYS + i2).reshape(N, PEER_HEADS * PEER_TOPK)
    gate = jax.nn.softmax(cv, axis=-1).reshape(N, PEER_HEADS * PEER_TOPK)
    nb = -(-N // PEER_BLOCK)
    pad = nb * PEER_BLOCK - N
    xp = jnp.pad(x, ((0, pad), (0, 0))).reshape(nb, PEER_BLOCK, D)
    ip = jnp.pad(eidx, ((0, pad), (0, 0))).reshape(nb, PEER_BLOCK, -1)
    gp = jnp.pad(gate, ((0, pad), (0, 0))).reshape(nb, PEER_BLOCK, -1)

    def blk(args):
        xb, ib, gb = args
        h = jax.nn.gelu(jnp.einsum('tkd,td->tk', u_tab[ib], xb).astype(F32))
        return jnp.einsum('tk,tkd->td', (gb * h).astype(xb.dtype), v_tab[ib])

    out = lax.map(blk, (xp, ip, gp))
    return out.reshape(nb * PEER_BLOCK, D)[:N].reshape(B, T, D)


def _layer(h, mk, mv, S0, c0, w_in, conv_w, a_log, dt_bias, o_norm_g, sg_ln_g, sg_ln_b, sg_w, sg_b, w_out,
           norm_mix_g, norm_mem_g, w_mq, w_mo, norm_ffn_g, peer_wq, peer_keys, peer_u, peer_v):
    y, S, c, vrows = _hybrid_mixer(_rmsnorm(h, norm_mix_g), S0, c0, w_in, conv_w, a_log, dt_bias, o_norm_g,
                                   sg_ln_g, sg_ln_b, sg_w, sg_b, w_out)
    h = h + y
    h = h + _mem_attend(_rmsnorm(h, norm_mem_g), mk, mv, w_mq, w_mo)
    h = h + _peer(_rmsnorm(h, norm_ffn_g), peer_wq, peer_keys, peer_u, peer_v)
    return h, S, c, vrows


def setup_inputs(seed: int = 0) -> dict:
    key = jax.random.key(seed)
    ks = jax.random.split(key, 30)
    nrm = lambda k, shp, s: jax.random.normal(k, shp, F32) * s
    gain = lambda k, shp: 1.0 + 0.02 * jax.random.normal(k, shp, F32)
    dt = jnp.exp(jax.random.uniform(ks[10], (DEPTH, DN_HEADS), F32, math.log(1e-3), math.log(1e-1)))
    return {
        'x_prompt': nrm(ks[0], (BATCH, SEQ, D_MODEL), 1.0),
        'x_sample': nrm(ks[1], (DEC_BATCH, DEC_SEQ, D_MODEL), 1.0),
        'state_delta': nrm(ks[2], (DEPTH, DEC_BATCH, DN_HEADS, DN_DK, DN_DV), 0.1),
        'state_conv': nrm(ks[3], (DEPTH, DEC_BATCH, DN_CONV - 1, QKV_DIM), 1.0),
        'cache_mem_k': nrm(ks[4], (DEPTH, DEC_BATCH, MEM_LEN, MEM_HEADS, MEM_HD), 1.0),
        'cache_mem_v': nrm(ks[5], (DEPTH, DEC_BATCH, MEM_LEN, MEM_HEADS, MEM_HD), 1.0),
        'mem_prompt': nrm(ks[6], (BATCH, MEM_LEN, D_MODEL), 1.0),
        'w_in': nrm(ks[7], (DEPTH, D_MODEL, P_IN), D_MODEL ** -0.5),
        'conv_w': nrm(ks[8], (DEPTH, DN_CONV, QKV_DIM), DN_CONV ** -0.5),
        'a_log': jnp.log(jax.random.uniform(ks[9], (DEPTH, DN_HEADS), F32, 1.0, 16.0)),
        'dt_bias': dt + jnp.log(-jnp.expm1(-dt)),
        'o_norm_g': gain(ks[11], (DEPTH, DN_DV)),
        'sg_ln_g': gain(ks[12], (DEPTH, SG_GROUPS, SG_CH)),
        'sg_ln_b': nrm(ks[13], (DEPTH, SG_GROUPS, SG_CH), 0.02),
        'sg_w': nrm(ks[14], (DEPTH, SG_GROUPS, SG_CHUNK, SG_CHUNK), 0.5 * SG_CHUNK ** -0.5),
        'sg_b': gain(ks[15], (DEPTH, SG_GROUPS, SG_CHUNK)),
        'w_out': nrm(ks[16], (DEPTH, D_MIX, D_MODEL), D_MIX ** -0.5),
        'norm_mix_g': gain(ks[17], (DEPTH, D_MODEL)),
        'norm_mem_g': gain(ks[18], (DEPTH, D_MODEL)),
        'mem_norm_g': gain(ks[19], (DEPTH, D_MODEL)),
        'w_mq': nrm(ks[20], (DEPTH, D_MODEL, D_MODEL), D_MODEL ** -0.5),
        'w_mk': nrm(ks[21], (DEPTH, D_MODEL, D_MODEL), D_MODEL ** -0.5),
        'w_mv': nrm(ks[22], (DEPTH, D_MODEL, D_MODEL), D_MODEL ** -0.5),
        'w_mo': nrm(ks[23], (DEPTH, D_MODEL, D_MODEL), D_MODEL ** -0.5),
        'norm_ffn_g': gain(ks[24], (DEPTH, D_MODEL)),
        'peer_wq': nrm(ks[25], (DEPTH, D_MODEL, PEER_HEADS * PEER_DKEY), D_MODEL ** -0.5),
        'peer_keys': nrm(ks[26], (DEPTH, PEER_HEADS, 2, PEER_NKEYS, PEER_DKEY // 2), (PEER_DKEY // 2) ** -0.5),
        'peer_u': nrm(ks[27], (DEPTH, PEER_NEXP, D_MODEL), D_MODEL ** -0.5),
        'peer_v': nrm(ks[28], (DEPTH, PEER_NEXP, D_MODEL), (PEER_HEADS * PEER_TOPK) ** -0.5),
        'final_norm_g': gain(ks[29], (D_MODEL,)),
    }


def reference(x_prompt, x_sample, state_delta, state_conv, cache_mem_k, cache_mem_v, mem_prompt,
              w_in, conv_w, a_log, dt_bias, o_norm_g, sg_ln_g, sg_ln_b, sg_w, sg_b, w_out,
              norm_mix_g, norm_mem_g, mem_norm_g, w_mq, w_mk, w_mv, w_mo, norm_ffn_g,
              peer_wq, peer_keys, peer_u, peer_v, final_norm_g):
    hp, hs = x_prompt, x_sample
    Bp = x_prompt.shape[0]
    sd_p, sc_p, mk_p, mv_p, sd_s, sc_s, vr_s = [], [], [], [], [], [], []
    for l in range(DEPTH):
        lw = (w_in[l], conv_w[l], a_log[l], dt_bias[l], o_norm_g[l], sg_ln_g[l], sg_ln_b[l], sg_w[l], sg_b[l],
              w_out[l], norm_mix_g[l], norm_mem_g[l], w_mq[l], w_mo[l], norm_ffn_g[l],
              peer_wq[l], peer_keys[l], peer_u[l], peer_v[l])
        mk, mv = _mem_kv(mem_prompt, mem_norm_g[l], w_mk[l], w_mv[l])
        S0 = jnp.zeros((Bp, DN_HEADS, DN_DK, DN_DV), F32)
        c0 = jnp.zeros((Bp, DN_CONV - 1, QKV_DIM), hp.dtype)
        hp, Sp, cp, _ = _layer(hp, mk, mv, S0, c0, *lw)
        sd_p.append(Sp)
        sc_p.append(cp)
        mk_p.append(mk)
        mv_p.append(mv)
        hs, Ss, cs, vr = _layer(hs, cache_mem_k[l], cache_mem_v[l], state_delta[l], state_conv[l], *lw)
        sd_s.append(Ss)
        sc_s.append(cs)
        vr_s.append(vr)
    y_prompt = _rmsnorm(hp, final_norm_g)
    y_sample = _rmsnorm(hs, final_norm_g)
    state_delta_prompt = jnp.stack(sd_p)
    state_conv_prompt = jnp.stack(sc_p)
    cache_mem_k_prompt = jnp.stack(mk_p)
    cache_mem_v_prompt = jnp.stack(mv_p)
    state_delta_sample = jnp.stack(sd_s)
    state_conv_sample = jnp.stack(sc_s)
    sgu_v_rows_sample = jnp.stack(vr_s)
    return (y_prompt, y_sample, state_delta_prompt, state_conv_prompt, cache_mem_k_prompt, cache_mem_v_prompt,
            state_delta_sample, state_conv_sample, sgu_v_rows_sample)
```

```python
import functools
import math

import jax
import jax.numpy as jnp
from jax import lax
from jax.experimental import pallas as pl
from jax.experimental.pallas import tpu as pltpu

F32 = jnp.float32
BF16 = jnp.bfloat16
I32 = jnp.int32
U32 = jnp.uint32

D_MODEL = 1024
DN_HEADS = 4
DN_DK = 128
DN_DV = 128
DN_QK = DN_HEADS * DN_DK
DN_V = DN_HEADS * DN_DV
DN_CONV = 4
DN_CHUNK = 64
QKV_DIM = 2 * DN_QK + DN_V
SG_GROUPS = 4
SG_CH = 128
SG_WIDTH = SG_GROUPS * SG_CH
SG_CHUNK = 128
MEM_LEN = 256
MEM_HEADS = 4
MEM_HD = D_MODEL // MEM_HEADS
PEER_HEADS = 8
PEER_NKEYS = 128
PEER_NEXP = PEER_NKEYS * PEER_NKEYS
PEER_DKEY = 128
PEER_TOPK = 16
PEER_SLOTS = PEER_HEADS * PEER_TOPK
EPS = 1e-6

SUBLANES = 8
LANES = 128
ROW_TILES = D_MODEL // LANES
VMEM_BYTES_V7X = 64 * 1024 * 1024
HALF_EXPERTS = PEER_NEXP // 2

_HI = lax.Precision.HIGHEST


def _cparams(sem, vmem_mib=None):
    kw = dict(dimension_semantics=sem)
    if vmem_mib is not None:
        kw["vmem_limit_bytes"] = vmem_mib * 1024 * 1024
    return pltpu.CompilerParams(**kw)


def _rms(x, g):
    return x * lax.rsqrt(jnp.mean(x * x, axis=-1, keepdims=True) + EPS) * g


def _gelu(x):
    return jax.nn.gelu(x, approximate=True)


def _sigmoid(x):
    return 1.0 / (1.0 + jnp.exp(-x))


def _silu(x):
    return x * _sigmoid(x)


def _softplus(x):
    return jnp.maximum(x, 0.0) + jnp.log(1.0 + jnp.exp(-jnp.abs(x)))


def _norm_proj_kernel(*refs, widths, has_delta, emit_h):
    it = iter(refs)
    x_ref = next(it)
    d_ref = next(it) if has_delta else None
    g_ref = next(it)
    w_ref = next(it)
    outs = list(it)
    x = x_ref[...]
    if has_delta:
        x = x + d_ref[...]
    if emit_h:
        outs[0][...] = x
        outs = outs[1:]
    n = _rms(x, g_ref[...]).astype(BF16)
    y = jnp.dot(n, w_ref[...], preferred_element_type=F32)
    c = 0
    for o, wd in zip(outs, widths):
        o[...] = y[:, c:c + wd]
        c += wd


def _norm_proj(x, g, w, widths, delta=None, tm=512):
    n_tok, d = x.shape
    tm = min(tm, n_tok)
    has_delta = delta is not None
    row = pl.BlockSpec((tm, d), lambda i: (i, 0))
    in_specs = [row] + ([row] if has_delta else []) + [
        pl.BlockSpec((1, d), lambda i: (0, 0)),
        pl.BlockSpec(w.shape, lambda i: (0, 0)),
    ]
    out_shape, out_specs = [], []
    if has_delta:
        out_shape.append(jax.ShapeDtypeStruct((n_tok, d), F32))
        out_specs.append(row)
    for wd in widths:
        out_shape.append(jax.ShapeDtypeStruct((n_tok, wd), F32))
        out_specs.append(pl.BlockSpec((tm, wd), lambda i: (i, 0)))
    args = [x] + ([delta] if has_delta else []) + [g.reshape(1, d), w]
    return pl.pallas_call(
        functools.partial(_norm_proj_kernel, widths=tuple(widths), has_delta=has_delta, emit_h=has_delta),
        grid=(n_tok // tm,),
        in_specs=in_specs,
        out_specs=out_specs,
        out_shape=out_shape,
        compiler_params=_cparams(("parallel",), 48),
        name="norm_proj",
    )(*args)


def _proj_res_kernel(*refs, n_in):
    h_ref = refs[0]
    a_refs = refs[1:1 + n_in]
    w_refs = refs[1 + n_in:1 + 2 * n_in]
    o_ref = refs[1 + 2 * n_in]
    acc = h_ref[...]
    for a, w in zip(a_refs, w_refs):
        acc = acc + jnp.dot(a[...].astype(BF16), w[...], preferred_element_type=F32)
    o_ref[...] = acc


def _proj_residual(h, acts, ws, tm=512):
    n_tok, d = h.shape
    tm = min(tm, n_tok)
    in_specs = [pl.BlockSpec((tm, d), lambda i: (i, 0))]
    in_specs += [pl.BlockSpec((tm, a.shape[1]), lambda i: (i, 0)) for a in acts]
    in_specs += [pl.BlockSpec(w.shape, lambda i: (0, 0)) for w in ws]
    return pl.pallas_call(
        functools.partial(_proj_res_kernel, n_in=len(acts)),
        grid=(n_tok // tm,),
        in_specs=in_specs,
        out_specs=pl.BlockSpec((tm, d), lambda i: (i, 0)),
        out_shape=jax.ShapeDtypeStruct((n_tok, d), F32),
        compiler_params=_cparams(("parallel",), 48),
        name="proj_residual",
    )(h, *acts, *ws)


def _add_norm_kernel(x_ref, d_ref, g_ref, o_ref):
    o_ref[...] = _rms(x_ref[...] + d_ref[...], g_ref[...])


def _add_norm(x, delta, g, tm=512):
    n_tok, d = x.shape
    tm = min(tm, n_tok)
    row = pl.BlockSpec((tm, d), lambda i: (i, 0))
    return pl.pallas_call(
        _add_norm_kernel,
        grid=(n_tok // tm,),
        in_specs=[row, row, pl.BlockSpec((1, d), lambda i: (0, 0))],
        out_specs=row,
        out_shape=jax.ShapeDtypeStruct((n_tok, d), F32),
        compiler_params=_cparams(("parallel",)),
        name="add_norm",
    )(x, delta, g.reshape(1, d))


def _dot_hi(a, b):
    return jnp.dot(a, b, precision=_HI, preferred_element_type=F32)


def _dot_nt_hi(a, b):
    return lax.dot_general(a, b, (((1,), (1,)), ((), ())), precision=_HI, preferred_element_type=F32)


def _dot_tn_hi(a, b):
    return lax.dot_general(a, b, (((0,), (0,)), ((), ())), precision=_HI, preferred_element_type=F32)


def _delta_kernel(qkv_ref, z_ref, ba_ref, c0_ref, s0_ref, cw_ref, al_ref, dt_ref, og_ref,
                  o_ref, s_ref, tail_ref, y_ref, bg_ref, *, t_valid, chunk):
    tb = qkv_ref.shape[1]
    tc = y_ref.shape[0]
    j = pl.program_id(1)

    @pl.when(j == 0)
    def _():
        s_ref[0] = s0_ref[0]
        tail_ref[...] = jnp.zeros_like(tail_ref)
        tail_ref[SUBLANES - (DN_CONV - 1):, :] = c0_ref[0]

    x = qkv_ref[0]
    xx = jnp.concatenate([tail_ref[...], x], axis=0)
    acc = jnp.zeros((tb, QKV_DIM), F32)
    for jj in range(DN_CONV):
        off = SUBLANES - (DN_CONV - 1) + jj
        acc = acc + xx[off:off + tb, :] * cw_ref[jj:jj + 1, :]
    if tb >= SUBLANES:
        tail_ref[...] = x[tb - SUBLANES:, :]
    y = _silu(acc)
    ba = ba_ref[0]
    beta_all = _sigmoid(ba)
    g_all = -jnp.exp(al_ref[...]) * _softplus(ba + dt_ref[...])
    lane = lax.broadcasted_iota(I32, ba.shape, 1)
    bg = jnp.where(lane < DN_HEADS, beta_all, g_all)
    if tc > tb:
        y_ref[...] = jnp.zeros_like(y_ref)
        bg_ref[...] = jnp.zeros_like(bg_ref)
    y_ref[:tb, :] = y
    bg_ref[:tb, :] = bg

    row = lax.broadcasted_iota(I32, (chunk, chunk), 0)
    col = lax.broadcasted_iota(I32, (chunk, chunk), 1)
    eye = (row == col).astype(F32)
    tril_incl = (row >= col).astype(F32)
    strict = row > col
    causal = row >= col
    n_fac = int(math.log2(chunk)) - 1

    def chunk_body(ci, carry):
        r0 = pl.multiple_of(ci * chunk, chunk)
        yc = y_ref[pl.ds(r0, chunk), :]
        bgc = bg_ref[pl.ds(r0, chunk), :]
        gcs = _dot_hi(tril_incl, bgc)
        zc = z_ref[0, pl.ds(r0, chunk), :] if tc == tb else None
        for hd in range(DN_HEADS):
            q = yc[:, hd * DN_DK:(hd + 1) * DN_DK]
            k = yc[:, DN_QK + hd * DN_DK:DN_QK + (hd + 1) * DN_DK]
            v = yc[:, 2 * DN_QK + hd * DN_DV:2 * DN_QK + (hd + 1) * DN_DV]
            q = q * lax.rsqrt(jnp.sum(q * q, axis=-1, keepdims=True) + EPS) * (DN_DK ** -0.5)
            k = k * lax.rsqrt(jnp.sum(k * k, axis=-1, keepdims=True) + EPS)
            beta = bgc[:, hd:hd + 1]
            gc = gcs[:, DN_HEADS + hd:DN_HEADS + hd + 1]
            gc_cols = jnp.broadcast_to(gc, (chunk, chunk))
            gc_rows = _dot_nt_hi(eye, gc_cols)
            diff = gc_cols - gc_rows
            kb = k * beta
            a = _dot_nt_hi(kb, k) * jnp.exp(jnp.where(strict, diff, -jnp.inf))
            tinv = eye - a
            p = _dot_hi(a, a)
            for f in range(n_fac):
                tinv = tinv + _dot_hi(tinv, p)
                if f + 1 < n_fac:
                    p = _dot_hi(p, p)
            u = _dot_hi(tinv, v * beta)
            w = _dot_hi(tinv, kb * jnp.exp(gc))
            qk = _dot_nt_hi(q, k) * jnp.exp(jnp.where(causal, diff, -jnp.inf))
            s_old = s_ref[0, hd]
            v_new = u - _dot_hi(w, s_old)
            o = _dot_hi(q * jnp.exp(gc), s_old) + _dot_hi(qk, v_new)
            glast = gc[chunk - 1:chunk, :]
            s_ref[0, hd] = s_old * jnp.exp(glast) + _dot_tn_hi(k * jnp.exp(glast - gc), v_new)
            o = _rms(o, og_ref[...])
            if tc == tb:
                zz = zc[:, hd * DN_DV:(hd + 1) * DN_DV]
                o_ref[0, pl.ds(r0, chunk), hd * DN_DV:(hd + 1) * DN_DV] = o * _silu(zz)
            else:
                zz = z_ref[0, :, hd * DN_DV:(hd + 1) * DN_DV]
                o_ref[0, :, hd * DN_DV:(hd + 1) * DN_DV] = o[:tb, :] * _silu(zz)
        return carry

    lax.fori_loop(0, tc // chunk, chunk_body, 0)


def _delta_mixer(qkv, z, ba, c0, s0, conv_w, a_log, dt_bias, o_norm_g):
    b, t, _ = qkv.shape
    chunk = DN_CHUNK
    tb = min(t, 256)
    tc = max(tb, chunk)
    al = jnp.zeros((1, LANES), F32).at[0, DN_HEADS:2 * DN_HEADS].set(a_log)
    dt = jnp.zeros((1, LANES), F32).at[0, DN_HEADS:2 * DN_HEADS].set(dt_bias)
    kern = functools.partial(_delta_kernel, t_valid=tb, chunk=chunk)
    return pl.pallas_call(
        kern,
        grid=(b, t // tb),
        in_specs=[
            pl.BlockSpec((1, tb, QKV_DIM), lambda i, j: (i, j, 0)),
            pl.BlockSpec((1, tb, DN_V), lambda i, j: (i, j, 0)),
            pl.BlockSpec((1, tb, LANES), lambda i, j: (i, j, 0)),
            pl.BlockSpec((1, DN_CONV - 1, QKV_DIM), lambda i, j: (i, 0, 0)),
            pl.BlockSpec((1, DN_HEADS, DN_DK, DN_DV), lambda i, j: (i, 0, 0, 0)),
            pl.BlockSpec((DN_CONV, QKV_DIM), lambda i, j: (0, 0)),
            pl.BlockSpec((1, LANES), lambda i, j: (0, 0)),
            pl.BlockSpec((1, LANES), lambda i, j: (0, 0)),
            pl.BlockSpec((1, DN_DV), lambda i, j: (0, 0)),
        ],
        out_specs=[
            pl.BlockSpec((1, tb, DN_V), lambda i, j: (i, j, 0)),
            pl.BlockSpec((1, DN_HEADS, DN_DK, DN_DV), lambda i, j: (i, 0, 0, 0)),
        ],
        out_shape=[
            jax.ShapeDtypeStruct((b, t, DN_V), F32),
            jax.ShapeDtypeStruct((b, DN_HEADS, DN_DK, DN_DV), F32),
        ],
        scratch_shapes=[
            pltpu.VMEM((SUBLANES, QKV_DIM), F32),
            pltpu.VMEM((tc, QKV_DIM), F32),
            pltpu.VMEM((tc, LANES), F32),
        ],
        compiler_params=_cparams(("parallel", "arbitrary"), 48),
        name="delta_mixer",
    )(qkv, z, ba, c0, s0, conv_w, al, dt, o_norm_g.reshape(1, DN_DV))


def _sgu_kernel(gu_ref, gv_ref, w_ref, b_ref, lg_ref, lb_ref, o_ref, *rest, period):
    vv_ref = rest[0] if rest else None
    n = gu_ref.shape[0]
    row = lax.broadcasted_iota(I32, (n, n), 0)
    col = lax.broadcasted_iota(I32, (n, n), 1)
    keep = row >= col
    if period < n:
        keep = keep & ((row // period) == (col // period))
    for g in range(SG_GROUPS):
        sl = slice(g * SG_CH, (g + 1) * SG_CH)
        u = _gelu(gu_ref[:, sl])
        x = _gelu(gv_ref[:, sl])
        mu = jnp.mean(x, axis=-1, keepdims=True)
        xc = x - mu
        var = jnp.mean(xc * xc, axis=-1, keepdims=True)
        vv = xc * lax.rsqrt(var + EPS) * lg_ref[:, sl] + lb_ref[:, sl]
        if vv_ref is not None:
            vv_ref[:, sl] = vv
        wm = jnp.where(keep, w_ref[g], 0.0).astype(BF16)
        mix = jnp.dot(wm, vv.astype(BF16), preferred_element_type=F32) + b_ref[g]
        o_ref[:, sl] = u * mix


def _sgu(gu, gv, w, bias, ln_g, ln_b, period, emit_vv):
    n_tok = gu.shape[0]
    blk = pl.BlockSpec((SG_CHUNK, SG_WIDTH), lambda i: (i, 0))
    out_shape = [jax.ShapeDtypeStruct((n_tok, SG_WIDTH), F32)]
    out_specs = [blk]
    if emit_vv:
        out_shape.append(jax.ShapeDtypeStruct((n_tok, SG_WIDTH), F32))
        out_specs.append(blk)
    return pl.pallas_call(
        functools.partial(_sgu_kernel, period=period),
        grid=(n_tok // SG_CHUNK,),
        in_specs=[
            blk, blk,
            pl.BlockSpec((SG_GROUPS, SG_CHUNK, SG_CHUNK), lambda i: (0, 0, 0)),
            pl.BlockSpec((SG_GROUPS, SG_CHUNK, SG_CH), lambda i: (0, 0, 0)),
            pl.BlockSpec((1, SG_WIDTH), lambda i: (0, 0)),
            pl.BlockSpec((1, SG_WIDTH), lambda i: (0, 0)),
        ],
        out_specs=out_specs,
        out_shape=out_shape,
        compiler_params=_cparams(("parallel",)),
        name="sgu",
    )(gu, gv, w, bias, ln_g.reshape(1, SG_WIDTH), ln_b.reshape(1, SG_WIDTH))


def _mem_attn_kernel(q_ref, k_ref, v_ref, o_ref):
    for hd in range(MEM_HEADS):
        sl = slice(hd * MEM_HD, (hd + 1) * MEM_HD)
        q = q_ref[0, :, sl].astype(BF16)
        k = k_ref[0, :, sl].astype(BF16)
        v = v_ref[0, :, sl].astype(BF16)
        s = lax.dot_general(q, k, (((1,), (1,)), ((), ())), preferred_element_type=F32) * (MEM_HD ** -0.5)
        m = jnp.max(s, axis=-1, keepdims=True)
        e = jnp.exp(s - m)
        p = e / jnp.sum(e, axis=-1, keepdims=True)
        o_ref[0, :, sl] = jnp.dot(p.astype(BF16), v, preferred_element_type=F32)


def _mem_attn(q, mk, mv):
    b, t, d = q.shape
    tq = min(t, 512)
    return pl.pallas_call(
        _mem_attn_kernel,
        grid=(b, t // tq),
        in_specs=[
            pl.BlockSpec((1, tq, d), lambda i, j: (i, j, 0)),
            pl.BlockSpec((1, MEM_LEN, d), lambda i, j: (i, 0, 0)),
            pl.BlockSpec((1, MEM_LEN, d), lambda i, j: (i, 0, 0)),
        ],
        out_specs=pl.BlockSpec((1, tq, d), lambda i, j: (i, j, 0)),
        out_shape=jax.ShapeDtypeStruct((b, t, d), F32),
        compiler_params=_cparams(("parallel", "parallel")),
        name="mem_attn",
    )(q, mk, mv)


def _top_rows(s, n_top, payload=None):
    rows = lax.broadcasted_iota(I32, s.shape, 0)
    big = jnp.int32(s.shape[0])
    vals, ids, pays = [], [], []
    for _ in range(n_top):
        m = jnp.max(s, axis=0, keepdims=True)
        i = jnp.min(jnp.where(s == m, rows, big), axis=0, keepdims=True)
        hit = rows == i
        vals.append(m)
        ids.append(i)
        if payload is not None:
            pays.append(jnp.max(jnp.where(hit, payload, -1), axis=0, keepdims=True))
        s = jnp.where(hit, -jnp.inf, s)
    out = [jnp.concatenate(vals, axis=0), jnp.concatenate(ids, axis=0)]
    if payload is not None:
        out.append(jnp.concatenate(pays, axis=0))
    return out


def _peer_route_kernel(x_ref, g_ref, wq_ref, keys_ref,
                       xn_ref, roff_ref, sh_ref, gate_ref, e_scr, g_scr):
    n = _rms(x_ref[...], g_ref[...])
    xn_ref[...] = n
    q = jnp.dot(n.astype(BF16), wq_ref[...], preferred_element_type=F32)
    for hd in range(PEER_HEADS):
        qh = q[:, hd * PEER_DKEY:(hd + 1) * PEER_DKEY].astype(BF16)
        tops = []
        for p in range(2):
            s_t = lax.dot_general(keys_ref[hd, p], qh, (((1,), (1,)), ((), ())),
                                  preferred_element_type=F32)
            tops.append(_top_rows(s_t, PEER_TOPK))
        (sv0, si0), (sv1, si1) = tops
        cand = jnp.concatenate([sv0[a:a + 1, :] + sv1 for a in range(PEER_TOPK)], axis=0)
        ecand = jnp.concatenate([si0[a:a + 1, :] * PEER_NKEYS + si1 for a in range(PEER_TOPK)], axis=0)
        cv, _, ce = _top_rows(cand, PEER_TOPK, payload=ecand)
        ex = jnp.exp(cv - cv[0:1, :])
        gate = ex / jnp.sum(ex, axis=0, keepdims=True)
        e_scr[hd * PEER_TOPK:(hd + 1) * PEER_TOPK, :] = ce
        g_scr[hd * PEER_TOPK:(hd + 1) * PEER_TOPK, :] = gate
    e_t = pltpu.bitcast(pltpu.bitcast(e_scr[...], F32).T, I32)
    hi = e_t >= HALF_EXPERTS
    roff_ref[...] = jnp.where(hi, e_t - HALF_EXPERTS, e_t) * ROW_TILES
    sh_ref[...] = jnp.where(hi, 0, 16)
    gate_ref[...] = g_scr[...].T


def _peer_route(h, g, wq, keys_pad):
    n_tok, d = h.shape
    tb = LANES
    row = pl.BlockSpec((tb, d), lambda i: (i, 0))
    slot = pl.BlockSpec((tb, PEER_SLOTS), lambda i: (i, 0))
    return pl.pallas_call(
        _peer_route_kernel,
        grid=(n_tok // tb,),
        in_specs=[
            row,
            pl.BlockSpec((1, d), lambda i: (0, 0)),
            pl.BlockSpec(wq.shape, lambda i: (0, 0)),
            pl.BlockSpec(keys_pad.shape, lambda i: (0, 0, 0, 0)),
        ],
        out_specs=[row, slot, slot, slot],
        out_shape=[
            jax.ShapeDtypeStruct((n_tok, d), F32),
            jax.ShapeDtypeStruct((n_tok, PEER_SLOTS), I32),
            jax.ShapeDtypeStruct((n_tok, PEER_SLOTS), I32),
            jax.ShapeDtypeStruct((n_tok, PEER_SLOTS), F32),
        ],
        scratch_shapes=[pltpu.VMEM((PEER_SLOTS, tb), I32), pltpu.VMEM((PEER_SLOTS, tb), F32)],
        compiler_params=_cparams(("parallel",), 48),
        name="peer_route",
    )(h, g.reshape(1, d), wq, keys_pad)


def _expert_tile(tab_ref, roff, shv):
    w = tab_ref[pl.ds(pl.multiple_of(roff, SUBLANES), SUBLANES), :]
    return pltpu.bitcast((w << shv) & jnp.uint32(0xFFFF0000), F32)


def _fold_rows(ps):
    sub = lax.broadcasted_iota(I32, (SUBLANES, LANES), 0)
    dist = SUBLANES // 2
    while len(ps) > 1:
        lo = (sub % (2 * dist)) < dist
        half = len(ps) // 2
        nxt = []
        for a in range(half):
            x, y = ps[a], ps[a + half]
            nxt.append(jnp.where(lo, x, pltpu.roll(y, dist, 0))
                       + jnp.where(lo, pltpu.roll(x, SUBLANES - dist, 0), y))
        ps = nxt
        dist //= 2
    return ps[0]


def _slot_columns(block, dst_ref):
    tb = block.shape[0]
    for t in range(tb):
        dst_ref[t] = jnp.broadcast_to(block[t:t + 1, :], (PEER_SLOTS, LANES)).T


def _peer_up_kernel(roff_ref, sh_ref, x_ref, tab_ref, h_ref, shs, racc):
    tb = sh_ref.shape[0]
    _slot_columns(pltpu.bitcast(sh_ref[...], F32), shs)

    def tok(t, carry):
        x_t = x_ref[pl.ds(pl.multiple_of(t * ROW_TILES, ROW_TILES), ROW_TILES), :]
        for g in range(PEER_SLOTS // SUBLANES):
            ps = []
            for jj in range(SUBLANES):
                k = g * SUBLANES + jj
                shv = pltpu.bitcast(jnp.broadcast_to(shs[t, k:k + 1, :], (SUBLANES, LANES)), U32)
                ps.append(_expert_tile(tab_ref, roff_ref[t, k], shv) * x_t)
            racc[t, g * SUBLANES:(g + 1) * SUBLANES, :] = _fold_rows(ps)
        return carry

    lax.fori_loop(0, tb, tok, 0)
    rows = [jnp.sum(racc[t].T, axis=0, keepdims=True) for t in range(tb)]
    h_ref[...] = jnp.concatenate(rows, axis=0)


def _peer_down_kernel(roff_ref, sh_ref, hh_ref, gate_ref, tab_ref, o_ref, shs, cs):
    tb = sh_ref.shape[0]
    _slot_columns(pltpu.bitcast(sh_ref[...], F32), shs)
    _slot_columns(gate_ref[...] * _gelu(hh_ref[...]), cs)

    def tok(t, carry):
        acc = jnp.zeros((SUBLANES, LANES), F32)
        for k in range(PEER_SLOTS):
            shv = pltpu.bitcast(jnp.broadcast_to(shs[t, k:k + 1, :], (SUBLANES, LANES)), U32)
            cv = jnp.broadcast_to(cs[t, k:k + 1, :], (SUBLANES, LANES))
            acc = acc + _expert_tile(tab_ref, roff_ref[t, k], shv) * cv
        o_ref[pl.ds(pl.multiple_of(t * ROW_TILES, ROW_TILES), ROW_TILES), :] = acc
        return carry

    lax.fori_loop(0, tb, tok, 0)


_PEER_TB = 32
_PEER_VMEM_MIB = 52


def _table_spec(tab):
    return pl.BlockSpec(tab.shape, lambda i: (0, 0), pipeline_mode=pl.Buffered(1))


def _peer_up(roff, sh, x_tiles, tab):
    n_tok = roff.shape[0]
    tb = _PEER_TB
    slot = pl.BlockSpec((tb, PEER_SLOTS), lambda i: (i, 0))
    return pl.pallas_call(
        _peer_up_kernel,
        grid=(n_tok // tb,),
        in_specs=[
            pl.BlockSpec((tb, PEER_SLOTS), lambda i: (i, 0), memory_space=pltpu.SMEM),
            slot,
            pl.BlockSpec((tb * ROW_TILES, LANES), lambda i: (i, 0)),
            _table_spec(tab),
        ],
        out_specs=slot,
        out_shape=jax.ShapeDtypeStruct((n_tok, PEER_SLOTS), F32),
        scratch_shapes=[pltpu.VMEM((tb, PEER_SLOTS, LANES), F32), pltpu.VMEM((tb, PEER_SLOTS, LANES), F32)],
        compiler_params=_cparams(("arbitrary",), _PEER_VMEM_MIB),
        name="peer_up",
    )(roff, sh, x_tiles, tab)


def _peer_down(roff, sh, hh, gate, tab):
    n_tok = roff.shape[0]
    tb = _PEER_TB
    slot = pl.BlockSpec((tb, PEER_SLOTS), lambda i: (i, 0))
    return pl.pallas_call(
        _peer_down_kernel,
        grid=(n_tok // tb,),
        in_specs=[
            pl.BlockSpec((tb, PEER_SLOTS), lambda i: (i, 0), memory_space=pltpu.SMEM),
            slot, slot, slot,
            _table_spec(tab),
        ],
        out_specs=pl.BlockSpec((tb * ROW_TILES, LANES), lambda i: (i, 0)),
        out_shape=jax.ShapeDtypeStruct((n_tok * ROW_TILES, LANES), F32),
        scratch_shapes=[pltpu.VMEM((tb, PEER_SLOTS, LANES), F32), pltpu.VMEM((tb, PEER_SLOTS, LANES), F32)],
        compiler_params=_cparams(("arbitrary",), _PEER_VMEM_MIB),
        name="peer_down",
    )(roff, sh, hh, gate, tab)


def _pack_table(tab):
    bits = lax.bitcast_convert_type(tab.astype(BF16), jnp.uint16).astype(U32)
    packed = bits[:HALF_EXPERTS] | (bits[HALF_EXPERTS:] << 16)
    return packed.reshape(HALF_EXPERTS * ROW_TILES, LANES)


def _peer(h, g, wq, keys_pad, u_tab, v_tab):
    n_tok, d = h.shape
    xn, roff, sh, gate = _peer_route(h, g, wq, keys_pad)
    hh = _peer_up(roff, sh, xn.reshape(n_tok * ROW_TILES, LANES), u_tab)
    out = _peer_down(roff, sh, hh, gate, v_tab)
    return out.reshape(n_tok, d)


def _prep_layer(l, w_in, conv_w, a_log, dt_bias, o_norm_g, sg_ln_g, sg_ln_b, sg_w, sg_b, w_out,
                norm_mix_g, norm_mem_g, mem_norm_g, w_mq, w_mk, w_mv, w_mo, norm_ffn_g,
                peer_wq, peer_keys, peer_u, peer_v):
    o1 = QKV_DIM
    o2 = o1 + DN_V
    o4 = o2 + 2 * DN_HEADS
    o5 = o4 + SG_WIDTH
    wi = w_in[l]
    ba_cols = jnp.pad(wi[:, o2:o4], ((0, 0), (0, LANES - 2 * DN_HEADS)))
    w_in_r = jnp.concatenate([wi[:, :o2], wi[:, o4:], ba_cols], axis=1).astype(BF16)
    kz = jnp.zeros((PEER_HEADS, PEER_NKEYS, PEER_DKEY // 2), F32)
    keys_pad = jnp.stack([jnp.concatenate([peer_keys[l][:, 0], kz], axis=-1),
                          jnp.concatenate([kz, peer_keys[l][:, 1]], axis=-1)], axis=1).astype(BF16)
    return dict(
        w_in=w_in_r, conv_w=conv_w[l], a_log=a_log[l], dt_bias=dt_bias[l], o_norm_g=o_norm_g[l],
        sg_ln_g=sg_ln_g[l].reshape(-1), sg_ln_b=sg_ln_b[l].reshape(-1), sg_w=sg_w[l], sg_b=sg_b[l],
        w_out_a=w_out[l][:DN_V].astype(BF16), w_out_b=w_out[l][DN_V:].astype(BF16),
        norm_mix_g=norm_mix_g[l], norm_mem_g=norm_mem_g[l], mem_norm_g=mem_norm_g[l],
        w_mq=w_mq[l].astype(BF16), w_mkv=jnp.concatenate([w_mk[l], w_mv[l]], axis=1).astype(BF16),
        w_mo=w_mo[l].astype(BF16), norm_ffn_g=norm_ffn_g[l], peer_wq=peer_wq[l].astype(BF16),
        keys_pad=keys_pad, u_tab=_pack_table(peer_u[l]), v_tab=_pack_table(peer_v[l]),
    )


_IN_WIDTHS = (QKV_DIM, DN_V, SG_WIDTH, SG_WIDTH, LANES)


def _layer(h, delta, b, t, mk, mv, s0, c0, p, emit_vv):
    n_tok = b * t
    outs = _norm_proj(h, p["norm_mix_g"], p["w_in"], _IN_WIDTHS, delta=delta)
    if delta is not None:
        h, outs = outs[0], outs[1:]
    qkv, z, gu, gv, ba = outs
    qkv3 = qkv.reshape(b, t, QKV_DIM)
    o_a, s_new = _delta_mixer(qkv3, z.reshape(b, t, DN_V), ba.reshape(b, t, LANES), c0, s0,
                              p["conv_w"], p["a_log"], p["dt_bias"], p["o_norm_g"])
    conv_new = qkv3[:, t - (DN_CONV - 1):, :]
    period = min(t, SG_CHUNK)
    reps = SG_CHUNK // period
    sg_w = jnp.tile(p["sg_w"][:, :period, :period], (1, reps, reps))
    sg_bias = jnp.broadcast_to(jnp.tile(p["sg_b"][:, :period], (1, reps))[:, :, None],
                               (SG_GROUPS, SG_CHUNK, SG_CH))
    sg_out = _sgu(gu, gv, sg_w, sg_bias, p["sg_ln_g"], p["sg_ln_b"], period, emit_vv)
    o_b = sg_out[0]
    vv = sg_out[1] if emit_vv else None
    h = _proj_residual(h, [o_a.reshape(n_tok, DN_V), o_b], [p["w_out_a"], p["w_out_b"]])
    (q,) = _norm_proj(h, p["norm_mem_g"], p["w_mq"], (D_MODEL,))
    att = _mem_attn(q.reshape(b, t, D_MODEL), mk, mv)
    h = _proj_residual(h, [att.reshape(n_tok, D_MODEL)], [p["w_mo"]])
    return h, s_new, conv_new, vv


def kernel(x_prompt, x_sample, state_delta, state_conv, cache_mem_k, cache_mem_v, mem_prompt, w_in, conv_w, a_log, dt_bias, o_norm_g, sg_ln_g, sg_ln_b, sg_w, sg_b, w_out, norm_mix_g, norm_mem_g, mem_norm_g, w_mq, w_mk, w_mv, w_mo, norm_ffn_g, peer_wq, peer_keys, peer_u, peer_v, final_norm_g):
    depth = w_in.shape[0]
    bp, tp, d = x_prompt.shape
    bs, ts, _ = x_sample.shape
    hp = x_prompt.reshape(bp * tp, d)
    hs = x_sample.reshape(bs * ts, d)
    dp = ds = None
    mem_flat = mem_prompt.reshape(bp * MEM_LEN, d)
    sd_p, sc_p, mk_p, mv_p, sd_s, sc_s, vr_s = [], [], [], [], [], [], []
    for l in range(depth):
        p = _prep_layer(l, w_in, conv_w, a_log, dt_bias, o_norm_g, sg_ln_g, sg_ln_b, sg_w, sg_b, w_out,
                        norm_mix_g, norm_mem_g, mem_norm_g, w_mq, w_mk, w_mv, w_mo, norm_ffn_g,
                        peer_wq, peer_keys, peer_u, peer_v)
        mk, mv = _norm_proj(mem_flat, p["mem_norm_g"], p["w_mkv"], (D_MODEL, D_MODEL))
        mk = mk.reshape(bp, MEM_LEN, d)
        mv = mv.reshape(bp, MEM_LEN, d)
        s0 = jnp.zeros((bp, DN_HEADS, DN_DK, DN_DV), F32)
        c0 = jnp.zeros((bp, DN_CONV - 1, QKV_DIM), F32)
        hp, s_p, c_p, _ = _layer(hp, dp, bp, tp, mk, mv, s0, c0, p, False)
        hs, s_s, c_s, vv = _layer(hs, ds, bs, ts, cache_mem_k[l].reshape(bs, MEM_LEN, d),
                                  cache_mem_v[l].reshape(bs, MEM_LEN, d), state_delta[l], state_conv[l], p, True)
        dp = _peer(hp, p["norm_ffn_g"], p["peer_wq"], p["keys_pad"], p["u_tab"], p["v_tab"])
        ds = _peer(hs, p["norm_ffn_g"], p["peer_wq"], p["keys_pad"], p["u_tab"], p["v_tab"])
        sd_p.append(s_p)
        sc_p.append(c_p)
        mk_p.append(mk.reshape(bp, MEM_LEN, MEM_HEADS, MEM_HD))
        mv_p.append(mv.reshape(bp, MEM_LEN, MEM_HEADS, MEM_HD))
        sd_s.append(s_s)
        sc_s.append(c_s)
        vr_s.append(vv.reshape(bs, ts, SG_WIDTH))
    y_prompt = _add_norm(hp, dp, final_norm_g).reshape(bp, tp, d)
    y_sample = _add_norm(hs, ds, final_norm_g).reshape(bs, ts, d)
    return (y_prompt, y_sample, jnp.stack(sd_p), jnp.stack(sc_p), jnp.stack(mk_p), jnp.stack(mv_p),
            jnp.stack(sd_s), jnp.stack(sc_s), jnp.stack(vr_s))
```

```python
import functools
import math

import jax
import jax.numpy as jnp
from jax import lax
from jax.experimental import pallas as pl
from jax.experimental.pallas import tpu as pltpu

F32 = jnp.float32
BF16 = jnp.bfloat16
I32 = jnp.int32
U32 = jnp.uint32

D_MODEL = 1024
DN_HEADS = 4
DN_DK = 128
DN_DV = 128
DN_QK = DN_HEADS * DN_DK
DN_V = DN_HEADS * DN_DV
DN_CONV = 4
DN_CHUNK = 64
QKV_DIM = 2 * DN_QK + DN_V
SG_GROUPS = 4
SG_CH = 128
SG_WIDTH = SG_GROUPS * SG_CH
SG_CHUNK = 128
MEM_LEN = 256
MEM_HEADS = 4
MEM_HD = D_MODEL // MEM_HEADS
PEER_HEADS = 8
PEER_NKEYS = 128
PEER_NEXP = PEER_NKEYS * PEER_NKEYS
PEER_DKEY = 128
PEER_TOPK = 16
PEER_SLOTS = PEER_HEADS * PEER_TOPK
EPS = 1e-6

SUBLANES = 8
LANES = 128
ROW_TILES = D_MODEL // LANES
VMEM_BYTES_V7X = 64 * 1024 * 1024
HALF_EXPERTS = PEER_NEXP // 2

_HI = lax.Precision.HIGHEST


def _cparams(sem, vmem_mib=None):
    kw = dict(dimension_semantics=sem)
    if vmem_mib is not None:
        kw["vmem_limit_bytes"] = vmem_mib * 1024 * 1024
    return pltpu.CompilerParams(**kw)


def _rms(x, g):
    return x * lax.rsqrt(jnp.mean(x * x, axis=-1, keepdims=True) + EPS) * g


def _gelu(x):
    return jax.nn.gelu(x, approximate=True)


def _sigmoid(x):
    return 1.0 / (1.0 + jnp.exp(-x))


def _silu(x):
    return x * _sigmoid(x)


def _softplus(x):
    return jnp.maximum(x, 0.0) + jnp.log(1.0 + jnp.exp(-jnp.abs(x)))


def _norm_proj_kernel(*refs, widths, has_delta, emit_h):
    it = iter(refs)
    x_ref = next(it)
    d_ref = next(it) if has_delta else None
    g_ref = next(it)
    w_ref = next(it)
    outs = list(it)
    x = x_ref[...]
    if has_delta:
        x = x + d_ref[...]
    if emit_h:
        outs[0][...] = x
        outs = outs[1:]
    n = _rms(x, g_ref[...]).astype(BF16)
    y = jnp.dot(n, w_ref[...], preferred_element_type=F32)
    c = 0
    for o, wd in zip(outs, widths):
        o[...] = y[:, c:c + wd]
        c += wd


def _norm_proj(x, g, w, widths, delta=None, tm=512):
    n_tok, d = x.shape
    tm = min(tm, n_tok)
    has_delta = delta is not None
    row = pl.BlockSpec((tm, d), lambda i: (i, 0))
    in_specs = [row] + ([row] if has_delta else []) + [
        pl.BlockSpec((1, d), lambda i: (0, 0)),
        pl.BlockSpec(w.shape, lambda i: (0, 0)),
    ]
    out_shape, out_specs = [], []
    if has_delta:
        out_shape.append(jax.ShapeDtypeStruct((n_tok, d), F32))
        out_specs.append(row)
    for wd in widths:
        out_shape.append(jax.ShapeDtypeStruct((n_tok, wd), F32))
        out_specs.append(pl.BlockSpec((tm, wd), lambda i: (i, 0)))
    args = [x] + ([delta] if has_delta else []) + [g.reshape(1, d), w]
    return pl.pallas_call(
        functools.partial(_norm_proj_kernel, widths=tuple(widths), has_delta=has_delta, emit_h=has_delta),
        grid=(n_tok // tm,),
        in_specs=in_specs,
        out_specs=out_specs,
        out_shape=out_shape,
        compiler_params=_cparams(("parallel",), 48),
        name="norm_proj",
    )(*args)


def _proj_res_kernel(*refs, n_in):
    h_ref = refs[0]
    a_refs = refs[1:1 + n_in]
    w_refs = refs[1 + n_in:1 + 2 * n_in]
    o_ref = refs[1 + 2 * n_in]
    acc = h_ref[...]
    for a, w in zip(a_refs, w_refs):
        acc = acc + jnp.dot(a[...].astype(BF16), w[...], preferred_element_type=F32)
    o_ref[...] = acc


def _proj_residual(h, acts, ws, tm=512):
    n_tok, d = h.shape
    tm = min(tm, n_tok)
    in_specs = [pl.BlockSpec((tm, d), lambda i: (i, 0))]
    in_specs += [pl.BlockSpec((tm, a.shape[1]), lambda i: (i, 0)) for a in acts]
    in_specs += [pl.BlockSpec(w.shape, lambda i: (0, 0)) for w in ws]
    return pl.pallas_call(
        functools.partial(_proj_res_kernel, n_in=len(acts)),
        grid=(n_tok // tm,),
        in_specs=in_specs,
        out_specs=pl.BlockSpec((tm, d), lambda i: (i, 0)),
        out_shape=jax.ShapeDtypeStruct((n_tok, d), F32),
        compiler_params=_cparams(("parallel",), 48),
        name="proj_residual",
    )(h, *acts, *ws)


def _add_norm_kernel(x_ref, d_ref, g_ref, o_ref):
    o_ref[...] = _rms(x_ref[...] + d_ref[...], g_ref[...])


def _add_norm(x, delta, g, tm=512):
    n_tok, d = x.shape
    tm = min(tm, n_tok)
    row = pl.BlockSpec((tm, d), lambda i: (i, 0))
    return pl.pallas_call(
        _add_norm_kernel,
        grid=(n_tok // tm,),
        in_specs=[row, row, pl.BlockSpec((1, d), lambda i: (0, 0))],
        out_specs=row,
        out_shape=jax.ShapeDtypeStruct((n_tok, d), F32),
        compiler_params=_cparams(("parallel",)),
        name="add_norm",
    )(x, delta, g.reshape(1, d))


def _bdot(a, b):
    return jnp.dot(a.astype(BF16), b.astype(BF16), preferred_element_type=F32)


def _bdot_nt(a, b):
    return lax.dot_general(a.astype(BF16), b.astype(BF16), (((1,), (1,)), ((), ())), preferred_element_type=F32)


def _bdot_tn(a, b):
    return lax.dot_general(a.astype(BF16), b.astype(BF16), (((0,), (0,)), ((), ())), preferred_element_type=F32)


def _delta_kernel(qkv_ref, z_ref, ba_ref, prev_ref, s0_ref, cw_ref, al_ref, dt_ref, og_ref,
                  o_ref, s_ref, *scratch, period, carry):
    rows = qkv_ref.shape[0]
    n_sub = rows // period
    x = qkv_ref[...]
    if carry:
        tail_ref = scratch[0]
        j = pl.program_id(1)

        @pl.when(j == 0)
        def _():
            s_ref[0] = s0_ref[0]
            tail_ref[...] = jnp.zeros_like(tail_ref)
            tail_ref[SUBLANES - (DN_CONV - 1):, :] = prev_ref[0]

        xx = jnp.concatenate([tail_ref[...], x], axis=0)
        acc = jnp.zeros((rows, QKV_DIM), F32)
        for jj in range(DN_CONV):
            off = SUBLANES - (DN_CONV - 1) + jj
            acc = acc + xx[off:off + rows, :] * cw_ref[jj:jj + 1, :]
        tail_ref[...] = x[rows - SUBLANES:, :]
    else:
        pos = lax.broadcasted_iota(I32, (rows, QKV_DIM), 0) % period
        hist = prev_ref[...]
        acc = x * cw_ref[DN_CONV - 1:DN_CONV, :]
        for d in range(1, DN_CONV):
            shifted = jnp.where(pos >= d, pltpu.roll(x, d, 0), pltpu.roll(hist, d, 0))
            acc = acc + shifted * cw_ref[DN_CONV - 1 - d:DN_CONV - d, :]
    y = _silu(acc)
    ba = ba_ref[...]
    lane = lax.broadcasted_iota(I32, ba.shape, 1)
    bg = jnp.where(lane < DN_HEADS, _sigmoid(ba), -jnp.exp(al_ref[...]) * _softplus(ba + dt_ref[...]))

    row = lax.broadcasted_iota(I32, (rows, rows), 0)
    col = lax.broadcasted_iota(I32, (rows, rows), 1)
    same = (row // period) == (col // period)
    causal = same & (row >= col)
    strict = same & (row > col)
    eye = (row == col).astype(F32)
    gcs = jnp.dot(causal.astype(F32), bg, precision=_HI, preferred_element_type=F32)
    n_fac = int(math.log2(period)) - 1

    for hd in range(DN_HEADS):
        q = y[:, hd * DN_DK:(hd + 1) * DN_DK]
        k = y[:, DN_QK + hd * DN_DK:DN_QK + (hd + 1) * DN_DK]
        v = y[:, 2 * DN_QK + hd * DN_DV:2 * DN_QK + (hd + 1) * DN_DV]
        q = q * lax.rsqrt(jnp.sum(q * q, axis=-1, keepdims=True) + EPS) * (DN_DK ** -0.5)
        k = k * lax.rsqrt(jnp.sum(k * k, axis=-1, keepdims=True) + EPS)
        beta = bg[:, hd:hd + 1]
        gc = gcs[:, DN_HEADS + hd:DN_HEADS + hd + 1]
        gc_cols = jnp.broadcast_to(gc, (rows, rows))
        diff = gc_cols - gc_cols.T
        decay = jnp.exp(jnp.where(causal, diff, -jnp.inf))
        kb = k * beta
        a = jnp.where(strict, _bdot_nt(kb, k) * decay, 0.0)
        tinv = eye - a
        p = _bdot(a, a)
        for f in range(n_fac):
            tinv = tinv + _bdot(tinv, p)
            if f + 1 < n_fac:
                p = _bdot(p, p)
        u = _bdot(tinv, v * beta)
        w = _bdot(tinv, kb * jnp.exp(gc))
        qk = _bdot_nt(q, k) * decay
        qg = q * jnp.exp(gc)
        v_new, o_state = [], []
        for c in range(n_sub):
            sl = slice(c * period, (c + 1) * period)
            s_old = s_ref[0, hd] if carry else s0_ref[c, hd]
            vn = u[sl] - _bdot(w[sl], s_old)
            o_state.append(_bdot(qg[sl], s_old))
            glast = gc[(c + 1) * period - 1:(c + 1) * period, :]
            s_new = s_old * jnp.exp(glast) + _bdot_tn(k[sl] * jnp.exp(glast - gc[sl]), vn)
            if carry:
                s_ref[0, hd] = s_new
            else:
                s_ref[c, hd] = s_new
            v_new.append(vn)
        o = jnp.concatenate(o_state, axis=0) + _bdot(qk, jnp.concatenate(v_new, axis=0))
        o = _rms(o, og_ref[...])
        o_ref[:, hd * DN_DV:(hd + 1) * DN_DV] = o * _silu(z_ref[:, hd * DN_DV:(hd + 1) * DN_DV])


_DELTA_ROWS = 256
_DELTA_SEQS = 16


def _delta_mixer(qkv, z, ba, c0, s0, b, t, conv_w, a_log, dt_bias, o_norm_g):
    al = jnp.zeros((1, LANES), F32).at[0, DN_HEADS:2 * DN_HEADS].set(a_log)
    dt = jnp.zeros((1, LANES), F32).at[0, DN_HEADS:2 * DN_HEADS].set(dt_bias)
    carry = t >= DN_CHUNK
    if carry:
        period, rows, spb = DN_CHUNK, _DELTA_ROWS, 1
        nj = t // rows
        grid = (b, nj)
        tok = lambda i, j: (i * nj + j, 0)
        st = lambda i, j: (i, 0, 0, 0)
        const = lambda i, j: (0, 0)
        prev, prev_spec = c0, pl.BlockSpec((1, DN_CONV - 1, QKV_DIM), lambda i, j: (i, 0, 0))
        scratch = [pltpu.VMEM((SUBLANES, QKV_DIM), F32)]
        sem = ("parallel", "arbitrary")
    else:
        period, spb = t, _DELTA_SEQS
        rows = spb * period
        grid = (b // spb,)
        tok = lambda i: (i, 0)
        st = lambda i: (i, 0, 0, 0)
        const = lambda i: (0, 0)
        hist = jnp.zeros((b, period, QKV_DIM), F32).at[:, period - (DN_CONV - 1):].set(c0)
        prev = jnp.roll(hist.reshape(b // spb, rows, QKV_DIM), -period, axis=1).reshape(b * period, QKV_DIM)
        prev_spec = pl.BlockSpec((rows, QKV_DIM), tok)
        scratch = []
        sem = ("parallel",)
    state = pl.BlockSpec((spb, DN_HEADS, DN_DK, DN_DV), st)
    return pl.pallas_call(
        functools.partial(_delta_kernel, period=period, carry=carry),
        grid=grid,
        in_specs=[
            pl.BlockSpec((rows, QKV_DIM), tok),
            pl.BlockSpec((rows, DN_V), tok),
            pl.BlockSpec((rows, LANES), tok),
            prev_spec,
            state,
            pl.BlockSpec((DN_CONV, QKV_DIM), const),
            pl.BlockSpec((1, LANES), const),
            pl.BlockSpec((1, LANES), const),
            pl.BlockSpec((1, DN_DV), const),
        ],
        out_specs=[pl.BlockSpec((rows, DN_V), tok), state],
        out_shape=[
            jax.ShapeDtypeStruct((b * t, DN_V), F32),
            jax.ShapeDtypeStruct((b, DN_HEADS, DN_DK, DN_DV), F32),
        ],
        scratch_shapes=scratch,
        compiler_params=_cparams(sem, 48),
        name="delta_mixer",
    )(qkv, z, ba, prev, s0, conv_w, al, dt, o_norm_g.reshape(1, DN_DV))


def _sgu_kernel(gu_ref, gv_ref, w_ref, b_ref, lg_ref, lb_ref, o_ref, *rest, period):
    vv_ref = rest[0] if rest else None
    n = gu_ref.shape[0]
    row = lax.broadcasted_iota(I32, (n, n), 0)
    col = lax.broadcasted_iota(I32, (n, n), 1)
    keep = row >= col
    if period < n:
        keep = keep & ((row // period) == (col // period))
    for g in range(SG_GROUPS):
        sl = slice(g * SG_CH, (g + 1) * SG_CH)
        u = _gelu(gu_ref[:, sl])
        x = _gelu(gv_ref[:, sl])
        mu = jnp.mean(x, axis=-1, keepdims=True)
        xc = x - mu
        var = jnp.mean(xc * xc, axis=-1, keepdims=True)
        vv = xc * lax.rsqrt(var + EPS) * lg_ref[:, sl] + lb_ref[:, sl]
        if vv_ref is not None:
            vv_ref[:, sl] = vv
        wm = jnp.where(keep, w_ref[g], 0.0).astype(BF16)
        mix = jnp.dot(wm, vv.astype(BF16), preferred_element_type=F32) + b_ref[g]
        o_ref[:, sl] = u * mix


def _sgu(gu, gv, w, bias, ln_g, ln_b, period, emit_vv):
    n_tok = gu.shape[0]
    blk = pl.BlockSpec((SG_CHUNK, SG_WIDTH), lambda i: (i, 0))
    out_shape = [jax.ShapeDtypeStruct((n_tok, SG_WIDTH), F32)]
    out_specs = [blk]
    if emit_vv:
        out_shape.append(jax.ShapeDtypeStruct((n_tok, SG_WIDTH), F32))
        out_specs.append(blk)
    return pl.pallas_call(
        functools.partial(_sgu_kernel, period=period),
        grid=(n_tok // SG_CHUNK,),
        in_specs=[
            blk, blk,
            pl.BlockSpec((SG_GROUPS, SG_CHUNK, SG_CHUNK), lambda i: (0, 0, 0)),
            pl.BlockSpec((SG_GROUPS, SG_CHUNK, SG_CH), lambda i: (0, 0, 0)),
            pl.BlockSpec((1, SG_WIDTH), lambda i: (0, 0)),
            pl.BlockSpec((1, SG_WIDTH), lambda i: (0, 0)),
        ],
        out_specs=out_specs,
        out_shape=out_shape,
        compiler_params=_cparams(("parallel",)),
        name="sgu",
    )(gu, gv, w, bias, ln_g.reshape(1, SG_WIDTH), ln_b.reshape(1, SG_WIDTH))


def _mem_attn_kernel(q_ref, k_ref, v_ref, o_ref):
    for hd in range(MEM_HEADS):
        sl = slice(hd * MEM_HD, (hd + 1) * MEM_HD)
        q = q_ref[0, :, sl].astype(BF16)
        k = k_ref[0, :, sl].astype(BF16)
        v = v_ref[0, :, sl].astype(BF16)
        s = lax.dot_general(q, k, (((1,), (1,)), ((), ())), preferred_element_type=F32) * (MEM_HD ** -0.5)
        m = jnp.max(s, axis=-1, keepdims=True)
        e = jnp.exp(s - m)
        p = e / jnp.sum(e, axis=-1, keepdims=True)
        o_ref[0, :, sl] = jnp.dot(p.astype(BF16), v, preferred_element_type=F32)


def _mem_attn(q, mk, mv):
    b, t, d = q.shape
    tq = min(t, 512)
    return pl.pallas_call(
        _mem_attn_kernel,
        grid=(b, t // tq),
        in_specs=[
            pl.BlockSpec((1, tq, d), lambda i, j: (i, j, 0)),
            pl.BlockSpec((1, MEM_LEN, d), lambda i, j: (i, 0, 0)),
            pl.BlockSpec((1, MEM_LEN, d), lambda i, j: (i, 0, 0)),
        ],
        out_specs=pl.BlockSpec((1, tq, d), lambda i, j: (i, j, 0)),
        out_shape=jax.ShapeDtypeStruct((b, t, d), F32),
        compiler_params=_cparams(("parallel", "parallel")),
        name="mem_attn",
    )(q, mk, mv)


def _top_rows(s, n_top, rank=None, payload=None):
    if rank is None:
        rank = lax.broadcasted_iota(I32, s.shape, 0)
    big = jnp.int32(2 ** 30)
    vals, ids, pays = [], [], []
    for _ in range(n_top):
        m = jnp.max(s, axis=0, keepdims=True)
        i = jnp.min(jnp.where(s == m, rank, big), axis=0, keepdims=True)
        hit = rank == i
        vals.append(m)
        ids.append(i)
        if payload is not None:
            pays.append(jnp.max(jnp.where(hit, payload, -1), axis=0, keepdims=True))
        s = jnp.where(hit, -jnp.inf, s)
    out = [jnp.concatenate(vals, axis=0), jnp.concatenate(ids, axis=0)]
    if payload is not None:
        out.append(jnp.concatenate(pays, axis=0))
    return out


_PAIR_GROUPS = ((0, 0), (0, 8), (1, 0), (2, 0), (3, 0), (4, 0), (5, 0), (6, 0), (7, 0))


def _pair_candidates(sv0, si0, sv1, si1):
    sub = lax.broadcasted_iota(I32, (SUBLANES,) + sv0.shape[1:], 0)
    cand, flat, eid = [], [], []
    for a, b0 in _PAIR_GROUPS:
        cand.append(sv0[a:a + 1, :] + sv1[b0:b0 + SUBLANES, :])
        flat.append(a * PEER_TOPK + b0 + sub)
        eid.append(si0[a:a + 1, :] * PEER_NKEYS + si1[b0:b0 + SUBLANES, :])
    cand.append(sv0[SUBLANES:, :] + sv1[0:1, :])
    flat.append((SUBLANES + sub) * PEER_TOPK)
    eid.append(si0[SUBLANES:, :] * PEER_NKEYS + si1[0:1, :])
    return jnp.concatenate(cand, axis=0), jnp.concatenate(flat, axis=0), jnp.concatenate(eid, axis=0)


def _peer_route_kernel(x_ref, g_ref, wq_ref, keys_ref,
                       xn_ref, roff_ref, sh_ref, gate_ref, e_scr, g_scr):
    n = _rms(x_ref[...], g_ref[...])
    xn_ref[...] = n
    q = jnp.dot(n.astype(BF16), wq_ref[...], preferred_element_type=F32)
    for hd in range(PEER_HEADS):
        qh = q[:, hd * PEER_DKEY:(hd + 1) * PEER_DKEY].astype(BF16)
        tops = []
        for p in range(2):
            s_t = lax.dot_general(keys_ref[hd, p], qh, (((1,), (1,)), ((), ())),
                                  preferred_element_type=F32)
            tops.append(_top_rows(s_t, PEER_TOPK))
        (sv0, si0), (sv1, si1) = tops
        cand, flat, ecand = _pair_candidates(sv0, si0, sv1, si1)
        cv, _, ce = _top_rows(cand, PEER_TOPK, rank=flat, payload=ecand)
        ex = jnp.exp(cv - cv[0:1, :])
        gate = ex / jnp.sum(ex, axis=0, keepdims=True)
        e_scr[hd * PEER_TOPK:(hd + 1) * PEER_TOPK, :] = ce
        g_scr[hd * PEER_TOPK:(hd + 1) * PEER_TOPK, :] = gate
    e_t = pltpu.bitcast(pltpu.bitcast(e_scr[...], F32).T, I32)
    hi = e_t >= HALF_EXPERTS
    roff_ref[...] = jnp.where(hi, e_t - HALF_EXPERTS, e_t) * ROW_TILES
    sh_ref[...] = jnp.where(hi, 0, 16)
    gate_ref[...] = g_scr[...].T


def _peer_route(h, g, wq, keys_pad):
    n_tok, d = h.shape
    tb = LANES
    row = pl.BlockSpec((tb, d), lambda i: (i, 0))
    slot = pl.BlockSpec((tb, PEER_SLOTS), lambda i: (i, 0))
    return pl.pallas_call(
        _peer_route_kernel,
        grid=(n_tok // tb,),
        in_specs=[
            row,
            pl.BlockSpec((1, d), lambda i: (0, 0)),
            pl.BlockSpec(wq.shape, lambda i: (0, 0)),
            pl.BlockSpec(keys_pad.shape, lambda i: (0, 0, 0, 0)),
        ],
        out_specs=[row, slot, slot, slot],
        out_shape=[
            jax.ShapeDtypeStruct((n_tok, d), F32),
            jax.ShapeDtypeStruct((n_tok, PEER_SLOTS), I32),
            jax.ShapeDtypeStruct((n_tok, PEER_SLOTS), I32),
            jax.ShapeDtypeStruct((n_tok, PEER_SLOTS), F32),
        ],
        scratch_shapes=[pltpu.VMEM((PEER_SLOTS, tb), I32), pltpu.VMEM((PEER_SLOTS, tb), F32)],
        compiler_params=_cparams(("parallel",), 48),
        name="peer_route",
    )(h, g.reshape(1, d), wq, keys_pad)


def _expert_tile(tab_ref, roff, shv):
    w = tab_ref[pl.ds(pl.multiple_of(roff, SUBLANES), SUBLANES), :]
    return pltpu.bitcast((w << shv) & jnp.uint32(0xFFFF0000), F32)


def _fold_rows(ps):
    sub = lax.broadcasted_iota(I32, (SUBLANES, LANES), 0)
    dist = SUBLANES // 2
    while len(ps) > 1:
        lo = (sub % (2 * dist)) < dist
        half = len(ps) // 2
        nxt = []
        for a in range(half):
            x, y = ps[a], ps[a + half]
            nxt.append(jnp.where(lo, x, pltpu.roll(y, dist, 0))
                       + jnp.where(lo, pltpu.roll(x, SUBLANES - dist, 0), y))
        ps = nxt
        dist //= 2
    return ps[0]


def _slot_columns(rows8, dst_ref):
    for u in range(SUBLANES):
        dst_ref[u] = jnp.broadcast_to(rows8[u:u + 1, :], (PEER_SLOTS, LANES)).T


def _bcast_row(ref, u, k):
    return jnp.broadcast_to(ref[u, k:k + 1, :], (SUBLANES, LANES))


def _peer_token_loop(tb, prep, token_group):
    n_pairs = tb // (2 * SUBLANES)
    prep(0, 0)

    def pair(i, carry):
        g0 = 2 * i
        prep(g0 + 1, 1)
        token_group(g0, 0)
        prep(jnp.minimum(g0 + 2, 2 * n_pairs - 1), 0)
        token_group(g0 + 1, 1)
        return carry

    lax.fori_loop(0, n_pairs, pair, 0)


def _peer_up_kernel(roff_ref, sh_ref, x_ref, tab_ref, h_ref, shs_a, shs_b):
    tb = sh_ref.shape[0]
    shs = (shs_a, shs_b)

    def prep(grp, par):
        r0 = pl.multiple_of(grp * SUBLANES, SUBLANES)
        _slot_columns(pltpu.bitcast(sh_ref[pl.ds(r0, SUBLANES), :], F32), shs[par])

    def token_group(grp, par):
        rows = []
        for u in range(SUBLANES):
            t = grp * SUBLANES + u
            x_t = x_ref[pl.ds(pl.multiple_of(t * ROW_TILES, ROW_TILES), ROW_TILES), :]
            tiles = []
            for g in range(PEER_SLOTS // SUBLANES):
                ps = []
                for jj in range(SUBLANES):
                    k = g * SUBLANES + jj
                    shv = pltpu.bitcast(_bcast_row(shs[par], u, k), U32)
                    ps.append(_expert_tile(tab_ref, roff_ref[t, k], shv) * x_t)
                tiles.append(_fold_rows(ps))
            r = jnp.concatenate(tiles, axis=0)
            rows.append(jnp.sum(r.T, axis=0, keepdims=True))
        h_ref[pl.ds(pl.multiple_of(grp * SUBLANES, SUBLANES), SUBLANES), :] = jnp.concatenate(rows, axis=0)

    _peer_token_loop(tb, prep, token_group)


def _peer_down_kernel(roff_ref, sh_ref, hh_ref, gate_ref, tab_ref, o_ref, sh_rows, c_rows, shs_a, shs_b, cs_a, cs_b):
    tb = sh_ref.shape[0]
    shs = (shs_a, shs_b)
    cs = (cs_a, cs_b)
    sh_f = pltpu.bitcast(sh_ref[...], F32)
    coef = gate_ref[...] * _gelu(hh_ref[...])
    for t in range(tb):
        sh_rows[t] = sh_f[t:t + 1, :]
        c_rows[t] = coef[t:t + 1, :]

    def prep(t, par):
        shs[par][...] = jnp.broadcast_to(sh_rows[t], (PEER_SLOTS, LANES)).T
        cs[par][...] = jnp.broadcast_to(c_rows[t], (PEER_SLOTS, LANES)).T

    def token(t, par):
        accs = [jnp.zeros((SUBLANES, LANES), F32) for _ in range(2)]
        for k in range(PEER_SLOTS):
            shv = pltpu.bitcast(jnp.broadcast_to(shs[par][k:k + 1, :], (SUBLANES, LANES)), U32)
            cv = jnp.broadcast_to(cs[par][k:k + 1, :], (SUBLANES, LANES))
            accs[k % 2] = accs[k % 2] + _expert_tile(tab_ref, roff_ref[t, k], shv) * cv
        o_ref[pl.ds(pl.multiple_of(t * ROW_TILES, ROW_TILES), ROW_TILES), :] = accs[0] + accs[1]

    prep(0, 0)

    def pair(i, carry):
        t0 = 2 * i
        prep(t0 + 1, 1)
        token(t0, 0)
        prep(jnp.minimum(t0 + 2, tb - 1), 0)
        token(t0 + 1, 1)
        return carry

    lax.fori_loop(0, tb // 2, pair, 0)


_PEER_TB = 64
_PEER_VMEM_MIB = 52
_COL_TILES = pltpu.VMEM((SUBLANES, PEER_SLOTS, LANES), F32)


def _table_spec(tab):
    return pl.BlockSpec(tab.shape, lambda i: (0, 0), pipeline_mode=pl.Buffered(1))


def _peer_up(roff, sh, x_tiles, tab):
    n_tok = roff.shape[0]
    tb = _PEER_TB
    slot = pl.BlockSpec((tb, PEER_SLOTS), lambda i: (i, 0))
    return pl.pallas_call(
        _peer_up_kernel,
        grid=(n_tok // tb,),
        in_specs=[
            pl.BlockSpec((tb, PEER_SLOTS), lambda i: (i, 0), memory_space=pltpu.SMEM),
            slot,
            pl.BlockSpec((tb * ROW_TILES, LANES), lambda i: (i, 0)),
            _table_spec(tab),
        ],
        out_specs=slot,
        out_shape=jax.ShapeDtypeStruct((n_tok, PEER_SLOTS), F32),
        scratch_shapes=[_COL_TILES] * 2,
        compiler_params=_cparams(("arbitrary",), _PEER_VMEM_MIB),
        name="peer_up",
    )(roff, sh, x_tiles, tab)


def _peer_down(roff, sh, hh, gate, tab):
    n_tok = roff.shape[0]
    tb = _PEER_TB
    slot = pl.BlockSpec((tb, PEER_SLOTS), lambda i: (i, 0))
    return pl.pallas_call(
        _peer_down_kernel,
        grid=(n_tok // tb,),
        in_specs=[
            pl.BlockSpec((tb, PEER_SLOTS), lambda i: (i, 0), memory_space=pltpu.SMEM),
            slot, slot, slot,
            _table_spec(tab),
        ],
        out_specs=pl.BlockSpec((tb * ROW_TILES, LANES), lambda i: (i, 0)),
        out_shape=jax.ShapeDtypeStruct((n_tok * ROW_TILES, LANES), F32),
        scratch_shapes=[pltpu.VMEM((tb, 1, PEER_SLOTS), F32)] * 2 + [pltpu.VMEM((PEER_SLOTS, LANES), F32)] * 4,
        compiler_params=_cparams(("arbitrary",), _PEER_VMEM_MIB),
        name="peer_down",
    )(roff, sh, hh, gate, tab)


def _pack_table(tab):
    bits = lax.bitcast_convert_type(tab.astype(BF16), jnp.uint16).astype(U32)
    packed = bits[:HALF_EXPERTS] | (bits[HALF_EXPERTS:] << 16)
    return packed.reshape(HALF_EXPERTS * ROW_TILES, LANES)


def _peer(h, g, wq, keys_pad, u_tab, v_tab):
    n_tok, d = h.shape
    xn, roff, sh, gate = _peer_route(h, g, wq, keys_pad)
    hh = _peer_up(roff, sh, xn.reshape(n_tok * ROW_TILES, LANES), u_tab)
    out = _peer_down(roff, sh, hh, gate, v_tab)
    return out.reshape(n_tok, d)


def _prep_layer(l, w_in, conv_w, a_log, dt_bias, o_norm_g, sg_ln_g, sg_ln_b, sg_w, sg_b, w_out,
                norm_mix_g, norm_mem_g, mem_norm_g, w_mq, w_mk, w_mv, w_mo, norm_ffn_g,
                peer_wq, peer_keys, peer_u, peer_v):
    o1 = QKV_DIM
    o2 = o1 + DN_V
    o4 = o2 + 2 * DN_HEADS
    o5 = o4 + SG_WIDTH
    wi = w_in[l]
    ba_cols = jnp.pad(wi[:, o2:o4], ((0, 0), (0, LANES - 2 * DN_HEADS)))
    w_in_r = jnp.concatenate([wi[:, :o2], wi[:, o4:], ba_cols], axis=1).astype(BF16)
    kz = jnp.zeros((PEER_HEADS, PEER_NKEYS, PEER_DKEY // 2), F32)
    keys_pad = jnp.stack([jnp.concatenate([peer_keys[l][:, 0], kz], axis=-1),
                          jnp.concatenate([kz, peer_keys[l][:, 1]], axis=-1)], axis=1).astype(BF16)
    return dict(
        w_in=w_in_r, conv_w=conv_w[l], a_log=a_log[l], dt_bias=dt_bias[l], o_norm_g=o_norm_g[l],
        sg_ln_g=sg_ln_g[l].reshape(-1), sg_ln_b=sg_ln_b[l].reshape(-1), sg_w=sg_w[l], sg_b=sg_b[l],
        w_out_a=w_out[l][:DN_V].astype(BF16), w_out_b=w_out[l][DN_V:].astype(BF16),
        norm_mix_g=norm_mix_g[l], norm_mem_g=norm_mem_g[l], mem_norm_g=mem_norm_g[l],
        w_mq=w_mq[l].astype(BF16), w_mkv=jnp.concatenate([w_mk[l], w_mv[l]], axis=1).astype(BF16),
        w_mo=w_mo[l].astype(BF16), norm_ffn_g=norm_ffn_g[l], peer_wq=peer_wq[l].astype(BF16),
        keys_pad=keys_pad, u_tab=_pack_table(peer_u[l]), v_tab=_pack_table(peer_v[l]),
    )


_IN_WIDTHS = (QKV_DIM, DN_V, SG_WIDTH, SG_WIDTH, LANES)


def _layer(h, delta, b, t, mk, mv, s0, c0, p, emit_vv):
    n_tok = b * t
    outs = _norm_proj(h, p["norm_mix_g"], p["w_in"], _IN_WIDTHS, delta=delta)
    if delta is not None:
        h, outs = outs[0], outs[1:]
    qkv, z, gu, gv, ba = outs
    o_a, s_new = _delta_mixer(qkv, z, ba, c0, s0, b, t, p["conv_w"], p["a_log"], p["dt_bias"], p["o_norm_g"])
    conv_new = qkv.reshape(b, t, QKV_DIM)[:, t - (DN_CONV - 1):, :]
    period = min(t, SG_CHUNK)
    reps = SG_CHUNK // period
    sg_w = jnp.tile(p["sg_w"][:, :period, :period], (1, reps, reps))
    sg_bias = jnp.broadcast_to(jnp.tile(p["sg_b"][:, :period], (1, reps))[:, :, None],
                               (SG_GROUPS, SG_CHUNK, SG_CH))
    sg_out = _sgu(gu, gv, sg_w, sg_bias, p["sg_ln_g"], p["sg_ln_b"], period, emit_vv)
    o_b = sg_out[0]
    vv = sg_out[1] if emit_vv else None
    h = _proj_residual(h, [o_a.reshape(n_tok, DN_V), o_b], [p["w_out_a"], p["w_out_b"]])
    (q,) = _norm_proj(h, p["norm_mem_g"], p["w_mq"], (D_MODEL,))
    att = _mem_attn(q.reshape(b, t, D_MODEL), mk, mv)
    h = _proj_residual(h, [att.reshape(n_tok, D_MODEL)], [p["w_mo"]])
    return h, s_new, conv_new, vv


def kernel(x_prompt, x_sample, state_delta, state_conv, cache_mem_k, cache_mem_v, mem_prompt, w_in, conv_w, a_log, dt_bias, o_norm_g, sg_ln_g, sg_ln_b, sg_w, sg_b, w_out, norm_mix_g, norm_mem_g, mem_norm_g, w_mq, w_mk, w_mv, w_mo, norm_ffn_g, peer_wq, peer_keys, peer_u, peer_v, final_norm_g):
    depth = w_in.shape[0]
    bp, tp, d = x_prompt.shape
    bs, ts, _ = x_sample.shape
    hp = x_prompt.reshape(bp * tp, d)
    hs = x_sample.reshape(bs * ts, d)
    dp = ds = None
    mem_flat = mem_prompt.reshape(bp * MEM_LEN, d)
    sd_p, sc_p, mk_p, mv_p, sd_s, sc_s, vr_s = [], [], [], [], [], [], []
    for l in range(depth):
        p = _prep_layer(l, w_in, conv_w, a_log, dt_bias, o_norm_g, sg_ln_g, sg_ln_b, sg_w, sg_b, w_out,
                        norm_mix_g, norm_mem_g, mem_norm_g, w_mq, w_mk, w_mv, w_mo, norm_ffn_g,
                        peer_wq, peer_keys, peer_u, peer_v)
        mk, mv = _norm_proj(mem_flat, p["mem_norm_g"], p["w_mkv"], (D_MODEL, D_MODEL))
        mk = mk.reshape(bp, MEM_LEN, d)
        mv = mv.reshape(bp, MEM_LEN, d)
        s0 = jnp.zeros((bp, DN_HEADS, DN_DK, DN_DV), F32)
        c0 = jnp.zeros((bp, DN_CONV - 1, QKV_DIM), F32)
        hp, s_p, c_p, _ = _layer(hp, dp, bp, tp, mk, mv, s0, c0, p, False)
        hs, s_s, c_s, vv = _layer(hs, ds, bs, ts, cache_mem_k[l].reshape(bs, MEM_LEN, d),
                                  cache_mem_v[l].reshape(bs, MEM_LEN, d), state_delta[l], state_conv[l], p, True)
        dp = _peer(hp, p["norm_ffn_g"], p["peer_wq"], p["keys_pad"], p["u_tab"], p["v_tab"])
        ds = _peer(hs, p["norm_ffn_g"], p["peer_wq"], p["keys_pad"], p["u_tab"], p["v_tab"])
        sd_p.append(s_p)
        sc_p.append(c_p)
        mk_p.append(mk.reshape(bp, MEM_LEN, MEM_HEADS, MEM_HD))
        mv_p.append(mv.reshape(bp, MEM_LEN, MEM_HEADS, MEM_HD))
        sd_s.append(s_s)
        sc_s.append(c_s)
        vr_s.append(vv.reshape(bs, ts, SG_WIDTH))
    y_prompt = _add_norm(hp, dp, final_norm_g).reshape(bp, tp, d)
    y_sample = _add_norm(hs, ds, final_norm_g).reshape(bs, ts, d)
    return (y_prompt, y_sample, jnp.stack(sd_p), jnp.stack(sc_p), jnp.stack(mk_p), jnp.stack(mv_p),
            jnp.stack(sd_s), jnp.stack(sc_s), jnp.stack(vr_s))
```

```python
import functools
import math

import jax
import jax.numpy as jnp
from jax import lax
from jax.experimental import pallas as pl
from jax.experimental.pallas import tpu as pltpu

F32 = jnp.float32
BF16 = jnp.bfloat16
I32 = jnp.int32
U32 = jnp.uint32

D_MODEL = 1024
DN_HEADS = 4
DN_DK = 128
DN_DV = 128
DN_QK = DN_HEADS * DN_DK
DN_V = DN_HEADS * DN_DV
DN_CONV = 4
DN_CHUNK = 64
QKV_DIM = 2 * DN_QK + DN_V
SG_GROUPS = 4
SG_CH = 128
SG_WIDTH = SG_GROUPS * SG_CH
SG_CHUNK = 128
MEM_LEN = 256
MEM_HEADS = 4
MEM_HD = D_MODEL // MEM_HEADS
PEER_HEADS = 8
PEER_NKEYS = 128
PEER_NEXP = PEER_NKEYS * PEER_NKEYS
PEER_DKEY = 128
PEER_TOPK = 16
PEER_SLOTS = PEER_HEADS * PEER_TOPK
EPS = 1e-6

SUBLANES = 8
LANES = 128
ROW_TILES = D_MODEL // LANES
VMEM_BYTES_V7X = 64 * 1024 * 1024
HALF_EXPERTS = PEER_NEXP // 2

_HI = lax.Precision.HIGHEST


def _cparams(sem, vmem_mib=None):
    kw = dict(dimension_semantics=sem)
    if vmem_mib is not None:
        kw["vmem_limit_bytes"] = vmem_mib * 1024 * 1024
    return pltpu.CompilerParams(**kw)


def _rms(x, g):
    return x * lax.rsqrt(jnp.mean(x * x, axis=-1, keepdims=True) + EPS) * g


def _gelu(x):
    return jax.nn.gelu(x, approximate=True)


def _sigmoid(x):
    return 1.0 / (1.0 + jnp.exp(-x))


def _silu(x):
    return x * _sigmoid(x)


def _softplus(x):
    return jnp.maximum(x, 0.0) + jnp.log(1.0 + jnp.exp(-jnp.abs(x)))


def _rows_from_tiles(t_ref):
    return jnp.concatenate([t_ref[:, s, :] for s in range(ROW_TILES)], axis=-1)


def _norm_proj_kernel(*refs, widths, has_delta, emit_h):
    it = iter(refs)
    x_ref = next(it)
    d_ref = next(it) if has_delta else None
    g_ref = next(it)
    w_ref = next(it)
    outs = list(it)
    x = x_ref[...]
    if has_delta:
        x = x + _rows_from_tiles(d_ref)
    if emit_h:
        outs[0][...] = x
        outs = outs[1:]
    n = _rms(x, g_ref[...]).astype(BF16)
    y = jnp.dot(n, w_ref[...], preferred_element_type=F32)
    c = 0
    for o, wd in zip(outs, widths):
        o[...] = y[:, c:c + wd]
        c += wd


def _norm_proj(x, g, w, widths, delta=None, tm=512):
    n_tok, d = x.shape
    tm = min(tm, n_tok)
    has_delta = delta is not None
    row = pl.BlockSpec((tm, d), lambda i: (i, 0))
    tiles = pl.BlockSpec((tm, ROW_TILES, LANES), lambda i: (i, 0, 0))
    in_specs = [row] + ([tiles] if has_delta else []) + [
        pl.BlockSpec((1, d), lambda i: (0, 0)),
        pl.BlockSpec(w.shape, lambda i: (0, 0)),
    ]
    out_shape, out_specs = [], []
    if has_delta:
        out_shape.append(jax.ShapeDtypeStruct((n_tok, d), F32))
        out_specs.append(row)
    for wd in widths:
        out_shape.append(jax.ShapeDtypeStruct((n_tok, wd), F32))
        out_specs.append(pl.BlockSpec((tm, wd), lambda i: (i, 0)))
    args = [x] + ([delta] if has_delta else []) + [g.reshape(1, d), w]
    return pl.pallas_call(
        functools.partial(_norm_proj_kernel, widths=tuple(widths), has_delta=has_delta, emit_h=has_delta),
        grid=(n_tok // tm,),
        in_specs=in_specs,
        out_specs=out_specs,
        out_shape=out_shape,
        compiler_params=_cparams(("parallel",), 48),
        name="norm_proj",
    )(*args)


def _proj_res_kernel(*refs, n_in):
    h_ref = refs[0]
    a_refs = refs[1:1 + n_in]
    w_refs = refs[1 + n_in:1 + 2 * n_in]
    o_ref = refs[1 + 2 * n_in]
    acc = h_ref[...]
    for a, w in zip(a_refs, w_refs):
        acc = acc + jnp.dot(a[...].astype(BF16), w[...], preferred_element_type=F32)
    o_ref[...] = acc


def _proj_residual(h, acts, ws, tm=512):
    n_tok, d = h.shape
    tm = min(tm, n_tok)
    in_specs = [pl.BlockSpec((tm, d), lambda i: (i, 0))]
    in_specs += [pl.BlockSpec((tm, a.shape[1]), lambda i: (i, 0)) for a in acts]
    in_specs += [pl.BlockSpec(w.shape, lambda i: (0, 0)) for w in ws]
    return pl.pallas_call(
        functools.partial(_proj_res_kernel, n_in=len(acts)),
        grid=(n_tok // tm,),
        in_specs=in_specs,
        out_specs=pl.BlockSpec((tm, d), lambda i: (i, 0)),
        out_shape=jax.ShapeDtypeStruct((n_tok, d), F32),
        compiler_params=_cparams(("parallel",), 48),
        name="proj_residual",
    )(h, *acts, *ws)


def _add_norm_kernel(x_ref, d_ref, g_ref, o_ref):
    o_ref[...] = _rms(x_ref[...] + _rows_from_tiles(d_ref), g_ref[...])


def _add_norm(x, delta, g, tm=512):
    n_tok, d = x.shape
    tm = min(tm, n_tok)
    row = pl.BlockSpec((tm, d), lambda i: (i, 0))
    tiles = pl.BlockSpec((tm, ROW_TILES, LANES), lambda i: (i, 0, 0))
    return pl.pallas_call(
        _add_norm_kernel,
        grid=(n_tok // tm,),
        in_specs=[row, tiles, pl.BlockSpec((1, d), lambda i: (0, 0))],
        out_specs=row,
        out_shape=jax.ShapeDtypeStruct((n_tok, d), F32),
        compiler_params=_cparams(("parallel",)),
        name="add_norm",
    )(x, delta, g.reshape(1, d))


def _bdot(a, b):
    return jnp.dot(a.astype(BF16), b.astype(BF16), preferred_element_type=F32)


def _bdot_nt(a, b):
    return lax.dot_general(a.astype(BF16), b.astype(BF16), (((1,), (1,)), ((), ())), preferred_element_type=F32)


def _bdot_tn(a, b):
    return lax.dot_general(a.astype(BF16), b.astype(BF16), (((0,), (0,)), ((), ())), preferred_element_type=F32)


def _delta_kernel(qkv_ref, z_ref, ba_ref, prev_ref, s0_ref, cw_ref, al_ref, dt_ref, og_ref,
                  o_ref, s_ref, *scratch, period, carry):
    rows = qkv_ref.shape[0]
    n_sub = rows // period
    x = qkv_ref[...]
    if carry:
        tail_ref = scratch[0]
        j = pl.program_id(1)

        @pl.when(j == 0)
        def _():
            s_ref[0] = s0_ref[0]
            tail_ref[...] = jnp.zeros_like(tail_ref)
            tail_ref[SUBLANES - (DN_CONV - 1):, :] = prev_ref[0]

        xx = jnp.concatenate([tail_ref[...], x], axis=0)
        acc = jnp.zeros((rows, QKV_DIM), F32)
        for jj in range(DN_CONV):
            off = SUBLANES - (DN_CONV - 1) + jj
            acc = acc + xx[off:off + rows, :] * cw_ref[jj:jj + 1, :]
        tail_ref[...] = x[rows - SUBLANES:, :]
    else:
        pos = lax.broadcasted_iota(I32, (rows, QKV_DIM), 0) % period
        hist = prev_ref[...]
        acc = x * cw_ref[DN_CONV - 1:DN_CONV, :]
        for d in range(1, DN_CONV):
            shifted = jnp.where(pos >= d, pltpu.roll(x, d, 0), pltpu.roll(hist, d, 0))
            acc = acc + shifted * cw_ref[DN_CONV - 1 - d:DN_CONV - d, :]
    y = _silu(acc)
    ba = ba_ref[...]
    lane = lax.broadcasted_iota(I32, ba.shape, 1)
    bg = jnp.where(lane < DN_HEADS, _sigmoid(ba), -jnp.exp(al_ref[...]) * _softplus(ba + dt_ref[...]))

    row = lax.broadcasted_iota(I32, (rows, rows), 0)
    col = lax.broadcasted_iota(I32, (rows, rows), 1)
    same = (row // period) == (col // period)
    causal = same & (row >= col)
    strict = same & (row > col)
    eye = (row == col).astype(F32)
    gcs = jnp.dot(causal.astype(F32), bg, precision=_HI, preferred_element_type=F32)
    n_fac = int(math.log2(period)) - 1

    for hd in range(DN_HEADS):
        q = y[:, hd * DN_DK:(hd + 1) * DN_DK]
        k = y[:, DN_QK + hd * DN_DK:DN_QK + (hd + 1) * DN_DK]
        v = y[:, 2 * DN_QK + hd * DN_DV:2 * DN_QK + (hd + 1) * DN_DV]
        q = q * lax.rsqrt(jnp.sum(q * q, axis=-1, keepdims=True) + EPS) * (DN_DK ** -0.5)
        k = k * lax.rsqrt(jnp.sum(k * k, axis=-1, keepdims=True) + EPS)
        beta = bg[:, hd:hd + 1]
        gc = gcs[:, DN_HEADS + hd:DN_HEADS + hd + 1]
        gc_cols = jnp.broadcast_to(gc, (rows, rows))
        diff = gc_cols - gc_cols.T
        decay = jnp.exp(jnp.where(causal, diff, -jnp.inf))
        kb = k * beta
        a = jnp.where(strict, _bdot_nt(kb, k) * decay, 0.0)
        tinv = eye - a
        p = _bdot(a, a)
        for f in range(n_fac):
            tinv = tinv + _bdot(tinv, p)
            if f + 1 < n_fac:
                p = _bdot(p, p)
        u = _bdot(tinv, v * beta)
        w = _bdot(tinv, kb * jnp.exp(gc))
        qk = _bdot_nt(q, k) * decay
        qg = q * jnp.exp(gc)
        v_new, o_state = [], []
        for c in range(n_sub):
            sl = slice(c * period, (c + 1) * period)
            s_old = s_ref[0, hd] if carry else s0_ref[c, hd]
            vn = u[sl] - _bdot(w[sl], s_old)
            o_state.append(_bdot(qg[sl], s_old))
            glast = gc[(c + 1) * period - 1:(c + 1) * period, :]
            s_new = s_old * jnp.exp(glast) + _bdot_tn(k[sl] * jnp.exp(glast - gc[sl]), vn)
            if carry:
                s_ref[0, hd] = s_new
            else:
                s_ref[c, hd] = s_new
            v_new.append(vn)
        o = jnp.concatenate(o_state, axis=0) + _bdot(qk, jnp.concatenate(v_new, axis=0))
        o = _rms(o, og_ref[...])
        o_ref[:, hd * DN_DV:(hd + 1) * DN_DV] = o * _silu(z_ref[:, hd * DN_DV:(hd + 1) * DN_DV])


_DELTA_ROWS = 256
_DELTA_SEQS = 16


def _delta_mixer(qkv, z, ba, c0, s0, b, t, conv_w, a_log, dt_bias, o_norm_g):
    al = jnp.zeros((1, LANES), F32).at[0, DN_HEADS:2 * DN_HEADS].set(a_log)
    dt = jnp.zeros((1, LANES), F32).at[0, DN_HEADS:2 * DN_HEADS].set(dt_bias)
    carry = t >= DN_CHUNK
    if carry:
        period, rows, spb = DN_CHUNK, _DELTA_ROWS, 1
        nj = t // rows
        grid = (b, nj)
        tok = lambda i, j: (i * nj + j, 0)
        st = lambda i, j: (i, 0, 0, 0)
        const = lambda i, j: (0, 0)
        prev, prev_spec = c0, pl.BlockSpec((1, DN_CONV - 1, QKV_DIM), lambda i, j: (i, 0, 0))
        scratch = [pltpu.VMEM((SUBLANES, QKV_DIM), F32)]
        sem = ("parallel", "arbitrary")
    else:
        period, spb = t, _DELTA_SEQS
        rows = spb * period
        grid = (b // spb,)
        tok = lambda i: (i, 0)
        st = lambda i: (i, 0, 0, 0)
        const = lambda i: (0, 0)
        hist = jnp.zeros((b, period, QKV_DIM), F32).at[:, period - (DN_CONV - 1):].set(c0)
        prev = jnp.roll(hist.reshape(b // spb, rows, QKV_DIM), -period, axis=1).reshape(b * period, QKV_DIM)
        prev_spec = pl.BlockSpec((rows, QKV_DIM), tok)
        scratch = []
        sem = ("parallel",)
    state = pl.BlockSpec((spb, DN_HEADS, DN_DK, DN_DV), st)
    return pl.pallas_call(
        functools.partial(_delta_kernel, period=period, carry=carry),
        grid=grid,
        in_specs=[
            pl.BlockSpec((rows, QKV_DIM), tok),
            pl.BlockSpec((rows, DN_V), tok),
            pl.BlockSpec((rows, LANES), tok),
            prev_spec,
            state,
            pl.BlockSpec((DN_CONV, QKV_DIM), const),
            pl.BlockSpec((1, LANES), const),
            pl.BlockSpec((1, LANES), const),
            pl.BlockSpec((1, DN_DV), const),
        ],
        out_specs=[pl.BlockSpec((rows, DN_V), tok), state],
        out_shape=[
            jax.ShapeDtypeStruct((b * t, DN_V), F32),
            jax.ShapeDtypeStruct((b, DN_HEADS, DN_DK, DN_DV), F32),
        ],
        scratch_shapes=scratch,
        compiler_params=_cparams(sem, 48),
        name="delta_mixer",
    )(qkv, z, ba, prev, s0, conv_w, al, dt, o_norm_g.reshape(1, DN_DV))


def _sgu_kernel(gu_ref, gv_ref, w_ref, b_ref, lg_ref, lb_ref, o_ref, *rest, period):
    vv_ref = rest[0] if rest else None
    n = gu_ref.shape[0]
    row = lax.broadcasted_iota(I32, (n, n), 0)
    col = lax.broadcasted_iota(I32, (n, n), 1)
    keep = row >= col
    if period < n:
        keep = keep & ((row // period) == (col // period))
    for g in range(SG_GROUPS):
        sl = slice(g * SG_CH, (g + 1) * SG_CH)
        u = _gelu(gu_ref[:, sl])
        x = _gelu(gv_ref[:, sl])
        mu = jnp.mean(x, axis=-1, keepdims=True)
        xc = x - mu
        var = jnp.mean(xc * xc, axis=-1, keepdims=True)
        vv = xc * lax.rsqrt(var + EPS) * lg_ref[:, sl] + lb_ref[:, sl]
        if vv_ref is not None:
            vv_ref[:, sl] = vv
        wm = jnp.where(keep, w_ref[g], 0.0).astype(BF16)
        mix = jnp.dot(wm, vv.astype(BF16), preferred_element_type=F32) + b_ref[g]
        o_ref[:, sl] = u * mix


def _sgu(gu, gv, w, bias, ln_g, ln_b, period, emit_vv):
    n_tok = gu.shape[0]
    blk = pl.BlockSpec((SG_CHUNK, SG_WIDTH), lambda i: (i, 0))
    out_shape = [jax.ShapeDtypeStruct((n_tok, SG_WIDTH), F32)]
    out_specs = [blk]
    if emit_vv:
        out_shape.append(jax.ShapeDtypeStruct((n_tok, SG_WIDTH), F32))
        out_specs.append(blk)
    return pl.pallas_call(
        functools.partial(_sgu_kernel, period=period),
        grid=(n_tok // SG_CHUNK,),
        in_specs=[
            blk, blk,
            pl.BlockSpec((SG_GROUPS, SG_CHUNK, SG_CHUNK), lambda i: (0, 0, 0)),
            pl.BlockSpec((SG_GROUPS, SG_CHUNK, SG_CH), lambda i: (0, 0, 0)),
            pl.BlockSpec((1, SG_WIDTH), lambda i: (0, 0)),
            pl.BlockSpec((1, SG_WIDTH), lambda i: (0, 0)),
        ],
        out_specs=out_specs,
        out_shape=out_shape,
        compiler_params=_cparams(("parallel",)),
        name="sgu",
    )(gu, gv, w, bias, ln_g.reshape(1, SG_WIDTH), ln_b.reshape(1, SG_WIDTH))


def _mem_attn_kernel(q_ref, k_ref, v_ref, o_ref):
    for hd in range(MEM_HEADS):
        sl = slice(hd * MEM_HD, (hd + 1) * MEM_HD)
        q = q_ref[0, :, sl].astype(BF16)
        if len(k_ref.shape) == 3 and k_ref.shape[1] == MEM_HEADS:
            k = k_ref[:, hd, :].astype(BF16)
            v = v_ref[:, hd, :].astype(BF16)
        else:
            k = k_ref[0, :, sl].astype(BF16)
            v = v_ref[0, :, sl].astype(BF16)
        s = lax.dot_general(q, k, (((1,), (1,)), ((), ())), preferred_element_type=F32) * (MEM_HD ** -0.5)
        m = jnp.max(s, axis=-1, keepdims=True)
        e = jnp.exp(s - m)
        p = e / jnp.sum(e, axis=-1, keepdims=True)
        o_ref[0, :, sl] = jnp.dot(p.astype(BF16), v, preferred_element_type=F32)


def _mem_attn(q, mk, mv, layer=None):
    b, t, d = q.shape
    tq = min(t, 512)
    if layer is None:
        kv = pl.BlockSpec((1, MEM_LEN, d), lambda i, j: (i, 0, 0))
    else:
        kv = pl.BlockSpec((None, None, MEM_LEN, MEM_HEADS, MEM_HD), lambda i, j: (layer, i, 0, 0, 0))
    return pl.pallas_call(
        _mem_attn_kernel,
        grid=(b, t // tq),
        in_specs=[pl.BlockSpec((1, tq, d), lambda i, j: (i, j, 0)), kv, kv],
        out_specs=pl.BlockSpec((1, tq, d), lambda i, j: (i, j, 0)),
        out_shape=jax.ShapeDtypeStruct((b, t, d), F32),
        compiler_params=_cparams(("parallel", "parallel")),
        name="mem_attn",
    )(q, mk, mv)


def _top_rows(s, n_top, rank=None, payload=None):
    if rank is None:
        rank = lax.broadcasted_iota(I32, s.shape, 0)
    big = jnp.int32(2 ** 30)
    vals, ids, pays = [], [], []
    for _ in range(n_top):
        m = jnp.max(s, axis=0, keepdims=True)
        i = jnp.min(jnp.where(s == m, rank, big), axis=0, keepdims=True)
        hit = rank == i
        vals.append(m)
        ids.append(i)
        if payload is not None:
            pays.append(jnp.max(jnp.where(hit, payload, -1), axis=0, keepdims=True))
        s = jnp.where(hit, -jnp.inf, s)
    out = [jnp.concatenate(vals, axis=0), jnp.concatenate(ids, axis=0)]
    if payload is not None:
        out.append(jnp.concatenate(pays, axis=0))
    return out


_PAIR_GROUPS = ((0, 0), (0, 8), (1, 0), (2, 0), (3, 0), (4, 0), (5, 0), (6, 0), (7, 0))


def _pair_candidates(sv0, si0, sv1, si1):
    sub = lax.broadcasted_iota(I32, (SUBLANES,) + sv0.shape[1:], 0)
    cand, flat, eid = [], [], []
    for a, b0 in _PAIR_GROUPS:
        cand.append(sv0[a:a + 1, :] + sv1[b0:b0 + SUBLANES, :])
        flat.append(a * PEER_TOPK + b0 + sub)
        eid.append(si0[a:a + 1, :] * PEER_NKEYS + si1[b0:b0 + SUBLANES, :])
    cand.append(sv0[SUBLANES:, :] + sv1[0:1, :])
    flat.append((SUBLANES + sub) * PEER_TOPK)
    eid.append(si0[SUBLANES:, :] * PEER_NKEYS + si1[0:1, :])
    return jnp.concatenate(cand, axis=0), jnp.concatenate(flat, axis=0), jnp.concatenate(eid, axis=0)


def _peer_route_kernel(x_ref, g_ref, wq_ref, keys_ref,
                       xn_ref, roff_ref, sh_ref, gate_ref, e_scr, g_scr):
    n = _rms(x_ref[...], g_ref[...])
    for s in range(ROW_TILES):
        xn_ref[:, s, :] = n[:, s * LANES:(s + 1) * LANES]
    q = jnp.dot(n.astype(BF16), wq_ref[...], preferred_element_type=F32)
    for hd in range(PEER_HEADS):
        qh = q[:, hd * PEER_DKEY:(hd + 1) * PEER_DKEY].astype(BF16)
        tops = []
        for p in range(2):
            s_t = lax.dot_general(keys_ref[hd, p], qh, (((1,), (1,)), ((), ())),
                                  preferred_element_type=F32)
            tops.append(_top_rows(s_t, PEER_TOPK))
        (sv0, si0), (sv1, si1) = tops
        cand, flat, ecand = _pair_candidates(sv0, si0, sv1, si1)
        cv, _, ce = _top_rows(cand, PEER_TOPK, rank=flat, payload=ecand)
        ex = jnp.exp(cv - cv[0:1, :])
        gate = ex / jnp.sum(ex, axis=0, keepdims=True)
        e_scr[hd * PEER_TOPK:(hd + 1) * PEER_TOPK, :] = ce
        g_scr[hd * PEER_TOPK:(hd + 1) * PEER_TOPK, :] = gate
    e_t = pltpu.bitcast(pltpu.bitcast(e_scr[...], F32).T, I32)
    hi = e_t >= HALF_EXPERTS
    roff_ref[...] = jnp.where(hi, e_t - HALF_EXPERTS, e_t) * ROW_TILES
    sh_ref[...] = jnp.where(hi, 0, 16)
    gate_ref[...] = g_scr[...].T


def _peer_route(h, g, wq, keys_pad):
    n_tok, d = h.shape
    tb = LANES
    row = pl.BlockSpec((tb, d), lambda i: (i, 0))
    slot = pl.BlockSpec((tb, PEER_SLOTS), lambda i: (i, 0))
    return pl.pallas_call(
        _peer_route_kernel,
        grid=(n_tok // tb,),
        in_specs=[
            row,
            pl.BlockSpec((1, d), lambda i: (0, 0)),
            pl.BlockSpec(wq.shape, lambda i: (0, 0)),
            pl.BlockSpec(keys_pad.shape, lambda i: (0, 0, 0, 0)),
        ],
        out_specs=[pl.BlockSpec((tb, ROW_TILES, LANES), lambda i: (i, 0, 0)), slot, slot, slot],
        out_shape=[
            jax.ShapeDtypeStruct((n_tok, ROW_TILES, LANES), F32),
            jax.ShapeDtypeStruct((n_tok, PEER_SLOTS), I32),
            jax.ShapeDtypeStruct((n_tok, PEER_SLOTS), I32),
            jax.ShapeDtypeStruct((n_tok, PEER_SLOTS), F32),
        ],
        scratch_shapes=[pltpu.VMEM((PEER_SLOTS, tb), I32), pltpu.VMEM((PEER_SLOTS, tb), F32)],
        compiler_params=_cparams(("parallel",), 48),
        name="peer_route",
    )(h, g.reshape(1, d), wq, keys_pad)


def _expert_tile(tab_ref, roff, shv):
    w = tab_ref[pl.ds(pl.multiple_of(roff, SUBLANES), SUBLANES), :]
    return pltpu.bitcast((w << shv) & jnp.uint32(0xFFFF0000), F32)


def _fold_rows(ps):
    sub = lax.broadcasted_iota(I32, (SUBLANES, LANES), 0)
    dist = SUBLANES // 2
    while len(ps) > 1:
        lo = (sub % (2 * dist)) < dist
        half = len(ps) // 2
        nxt = []
        for a in range(half):
            x, y = ps[a], ps[a + half]
            nxt.append(jnp.where(lo, x, pltpu.roll(y, dist, 0))
                       + jnp.where(lo, pltpu.roll(x, SUBLANES - dist, 0), y))
        ps = nxt
        dist //= 2
    return ps[0]


def _slot_columns(rows8, dst_ref):
    for u in range(SUBLANES):
        dst_ref[u] = jnp.broadcast_to(rows8[u:u + 1, :], (PEER_SLOTS, LANES)).T


def _bcast_row(ref, u, k):
    return jnp.broadcast_to(ref[u, k:k + 1, :], (SUBLANES, LANES))


def _peer_token_loop(tb, prep, token_group):
    n_pairs = tb // (2 * SUBLANES)
    prep(0, 0)

    def pair(i, carry):
        g0 = 2 * i
        prep(g0 + 1, 1)
        token_group(g0, 0)
        prep(jnp.minimum(g0 + 2, 2 * n_pairs - 1), 0)
        token_group(g0 + 1, 1)
        return carry

    lax.fori_loop(0, n_pairs, pair, 0)


def _peer_up_kernel(roff_ref, sh_ref, gate_ref, x_ref, tab_ref, c_ref, shs_a, shs_b):
    tb = sh_ref.shape[0]
    shs = (shs_a, shs_b)

    def prep(grp, par):
        r0 = pl.multiple_of(grp * SUBLANES, SUBLANES)
        _slot_columns(pltpu.bitcast(sh_ref[pl.ds(r0, SUBLANES), :], F32), shs[par])

    def token_group(grp, par):
        rows = []
        for u in range(SUBLANES):
            t = grp * SUBLANES + u
            x_t = x_ref[t]
            tiles = []
            for g in range(PEER_SLOTS // SUBLANES):
                ps = []
                for jj in range(SUBLANES):
                    k = g * SUBLANES + jj
                    shv = pltpu.bitcast(_bcast_row(shs[par], u, k), U32)
                    ps.append(_expert_tile(tab_ref, roff_ref[t, k], shv) * x_t)
                tiles.append(_fold_rows(ps))
            r = jnp.concatenate(tiles, axis=0)
            rows.append(jnp.sum(r.T, axis=0, keepdims=True))
        out = pl.ds(pl.multiple_of(grp * SUBLANES, SUBLANES), SUBLANES)
        c_ref[out, :] = gate_ref[out, :] * _gelu(jnp.concatenate(rows, axis=0))

    _peer_token_loop(tb, prep, token_group)


def _peer_down_kernel(roff_hbm, sh_ref, sh_next, c_ref, c_next, tab_ref, o_ref,
                      idx_a, idx_b, sh_a, sh_b, c_a, c_b, sem):
    i = pl.program_id(0)
    n = pl.num_programs(0)
    tb = sh_ref.shape[0]
    th = tb // 2
    idx = (idx_a, idx_b)

    def idx_copy(step, half):
        row0 = pl.multiple_of(step * tb + half * th, th)
        return pltpu.make_async_copy(roff_hbm.at[pl.ds(row0, th)], idx[half], sem.at[half])

    def build(src_sh, src_c, row, sh_t, c_t, u, after=None):
        sh_row = src_sh[row:row + 1, :]
        c_row = src_c[row:row + 1, :]
        if after is not None:
            zero = lax.shift_right_logical(pltpu.bitcast(after[0:1, :], U32), jnp.uint32(32))
            sh_row = sh_row + pltpu.bitcast(zero, I32)
            c_row = c_row + pltpu.bitcast(zero, F32)
        sh_t[u] = jnp.broadcast_to(pltpu.bitcast(sh_row, F32), (PEER_SLOTS, LANES)).T
        c_t[u] = jnp.broadcast_to(c_row, (PEER_SLOTS, LANES)).T

    def consume(half, sh_t, c_t, u):
        accs = [jnp.zeros((SUBLANES, LANES), F32) for _ in range(2)]
        for k in range(PEER_SLOTS):
            shv = pltpu.bitcast(_bcast_row(sh_t, u, k), U32)
            accs[k % 2] = accs[k % 2] + _expert_tile(tab_ref, idx[half][u, k], shv) * _bcast_row(c_t, u, k)
        acc = accs[0] + accs[1]
        o_ref[half * th + u] = acc
        return acc

    def half_step(half, use, fill, src_sh, src_c, row0):
        @pl.when(i >= 0)
        def _():
            acc = None
            for u in range(th):
                build(src_sh, src_c, row0 + u, fill[0], fill[1], u, after=acc)
                acc = consume(half, use[0], use[1], u)

    tiles = ((sh_a, c_a), (sh_b, c_b))

    @pl.when(i == 0)
    def _():
        idx_copy(0, 0).start()
        idx_copy(0, 1).start()
        for u in range(th):
            build(sh_ref, c_ref, u, sh_a, c_a, u)

    idx_copy(i, 0).wait()
    half_step(0, tiles[0], tiles[1], sh_ref, c_ref, th)

    @pl.when(i + 1 < n)
    def _():
        idx_copy(i + 1, 0).start()

    idx_copy(i, 1).wait()
    half_step(1, tiles[1], tiles[0], sh_next, c_next, 0)

    @pl.when(i + 1 < n)
    def _():
        idx_copy(i + 1, 1).start()


_PEER_TB = 64
_PEER_DOWN_TB = 32
_PEER_VMEM_MIB = 52
_COL_TILES = pltpu.VMEM((SUBLANES, PEER_SLOTS, LANES), F32)


def _table_spec(tab):
    return pl.BlockSpec(tab.shape, lambda i: (0, 0), pipeline_mode=pl.Buffered(1))


def _peer_up(roff, sh, gate, x_tiles, tab):
    n_tok = roff.shape[0]
    tb = _PEER_TB
    slot = pl.BlockSpec((tb, PEER_SLOTS), lambda i: (i, 0))
    return pl.pallas_call(
        _peer_up_kernel,
        grid=(n_tok // tb,),
        in_specs=[
            pl.BlockSpec((tb, PEER_SLOTS), lambda i: (i, 0), memory_space=pltpu.SMEM),
            slot, slot,
            pl.BlockSpec((tb, ROW_TILES, LANES), lambda i: (i, 0, 0)),
            _table_spec(tab),
        ],
        out_specs=slot,
        out_shape=jax.ShapeDtypeStruct((n_tok, PEER_SLOTS), F32),
        scratch_shapes=[_COL_TILES] * 2,
        compiler_params=_cparams(("arbitrary",), _PEER_VMEM_MIB),
        name="peer_up",
    )(roff, sh, gate, x_tiles, tab)


def _peer_down(roff, sh, coef, tab):
    n_tok = roff.shape[0]
    tb = _PEER_DOWN_TB
    th = tb // 2
    n_steps = n_tok // tb
    cur = pl.BlockSpec((tb, PEER_SLOTS), lambda i: (i, 0))
    nxt = pl.BlockSpec((tb, PEER_SLOTS), lambda i: (jnp.minimum(i + 1, n_steps - 1), 0))
    half_tiles = pltpu.VMEM((th, PEER_SLOTS, LANES), F32)
    return pl.pallas_call(
        _peer_down_kernel,
        grid=(n_steps,),
        in_specs=[pl.BlockSpec(memory_space=pl.ANY), cur, nxt, cur, nxt, _table_spec(tab)],
        out_specs=pl.BlockSpec((tb, ROW_TILES, LANES), lambda i: (i, 0, 0)),
        out_shape=jax.ShapeDtypeStruct((n_tok, ROW_TILES, LANES), F32),
        scratch_shapes=[pltpu.SMEM((th, PEER_SLOTS), I32)] * 2 + [half_tiles] * 4
        + [pltpu.SemaphoreType.DMA((2,))],
        compiler_params=_cparams(("arbitrary",), _PEER_VMEM_MIB),
        name="peer_down",
    )(roff, sh, sh, coef, coef, tab)


def _pack_kernel(lo_ref, hi_ref, o_ref):
    def bf16_bits(x):
        return pltpu.bitcast(x.astype(BF16).astype(F32), U32)

    word = bf16_bits(hi_ref[...]) | (bf16_bits(lo_ref[...]) >> 16)
    for s in range(ROW_TILES):
        o_ref[:, s, :] = word[:, s * LANES:(s + 1) * LANES]


def _pack_table(tab, rows=256):
    d = tab.shape[1]
    n_blk = HALF_EXPERTS // rows
    packed = pl.pallas_call(
        _pack_kernel,
        grid=(n_blk,),
        in_specs=[pl.BlockSpec((rows, d), lambda i: (i, 0)), pl.BlockSpec((rows, d), lambda i: (i + n_blk, 0))],
        out_specs=pl.BlockSpec((rows, ROW_TILES, LANES), lambda i: (i, 0, 0)),
        out_shape=jax.ShapeDtypeStruct((HALF_EXPERTS, ROW_TILES, LANES), U32),
        compiler_params=_cparams(("parallel",)),
        name="pack_table",
    )(tab, tab)
    return packed.reshape(HALF_EXPERTS * ROW_TILES, LANES)


def _peer(h, g, wq, keys_pad, u_tab, v_tab):
    xn, roff, sh, gate = _peer_route(h, g, wq, keys_pad)
    coef = _peer_up(roff, sh, gate, xn, u_tab)
    return _peer_down(roff, sh, coef, v_tab)


def _prep_layer(l, w_in, conv_w, a_log, dt_bias, o_norm_g, sg_ln_g, sg_ln_b, sg_w, sg_b, w_out,
                norm_mix_g, norm_mem_g, mem_norm_g, w_mq, w_mk, w_mv, w_mo, norm_ffn_g,
                peer_wq, peer_keys, peer_u, peer_v):
    o1 = QKV_DIM
    o2 = o1 + DN_V
    o4 = o2 + 2 * DN_HEADS
    o5 = o4 + SG_WIDTH
    wi = w_in[l]
    ba_cols = jnp.pad(wi[:, o2:o4], ((0, 0), (0, LANES - 2 * DN_HEADS)))
    w_in_r = jnp.concatenate([wi[:, :o2], wi[:, o4:], ba_cols], axis=1).astype(BF16)
    kz = jnp.zeros((PEER_HEADS, PEER_NKEYS, PEER_DKEY // 2), F32)
    keys_pad = jnp.stack([jnp.concatenate([peer_keys[l][:, 0], kz], axis=-1),
                          jnp.concatenate([kz, peer_keys[l][:, 1]], axis=-1)], axis=1).astype(BF16)
    return dict(
        w_in=w_in_r, conv_w=conv_w[l], a_log=a_log[l], dt_bias=dt_bias[l], o_norm_g=o_norm_g[l],
        sg_ln_g=sg_ln_g[l].reshape(-1), sg_ln_b=sg_ln_b[l].reshape(-1), sg_w=sg_w[l], sg_b=sg_b[l],
        w_out_a=w_out[l][:DN_V].astype(BF16), w_out_b=w_out[l][DN_V:].astype(BF16),
        norm_mix_g=norm_mix_g[l], norm_mem_g=norm_mem_g[l], mem_norm_g=mem_norm_g[l],
        w_mq=w_mq[l].astype(BF16), w_mkv=jnp.concatenate([w_mk[l], w_mv[l]], axis=1).astype(BF16),
        w_mo=w_mo[l].astype(BF16), norm_ffn_g=norm_ffn_g[l], peer_wq=peer_wq[l].astype(BF16),
        keys_pad=keys_pad, u_tab=_pack_table(peer_u[l]), v_tab=_pack_table(peer_v[l]),
    )


_IN_WIDTHS = (QKV_DIM, DN_V, SG_WIDTH, SG_WIDTH, LANES)


def _layer(h, delta, b, t, mk, mv, s0, c0, p, emit_vv, cache_layer=None):
    n_tok = b * t
    outs = _norm_proj(h, p["norm_mix_g"], p["w_in"], _IN_WIDTHS, delta=delta)
    if delta is not None:
        h, outs = outs[0], outs[1:]
    qkv, z, gu, gv, ba = outs
    o_a, s_new = _delta_mixer(qkv, z, ba, c0, s0, b, t, p["conv_w"], p["a_log"], p["dt_bias"], p["o_norm_g"])
    conv_new = qkv.reshape(b, t, QKV_DIM)[:, t - (DN_CONV - 1):, :]
    period = min(t, SG_CHUNK)
    reps = SG_CHUNK // period
    sg_w = jnp.tile(p["sg_w"][:, :period, :period], (1, reps, reps))
    sg_bias = jnp.broadcast_to(jnp.tile(p["sg_b"][:, :period], (1, reps))[:, :, None],
                               (SG_GROUPS, SG_CHUNK, SG_CH))
    sg_out = _sgu(gu, gv, sg_w, sg_bias, p["sg_ln_g"], p["sg_ln_b"], period, emit_vv)
    o_b = sg_out[0]
    vv = sg_out[1] if emit_vv else None
    h = _proj_residual(h, [o_a.reshape(n_tok, DN_V), o_b], [p["w_out_a"], p["w_out_b"]])
    (q,) = _norm_proj(h, p["norm_mem_g"], p["w_mq"], (D_MODEL,))
    att = _mem_attn(q.reshape(b, t, D_MODEL), mk, mv, layer=cache_layer)
    h = _proj_residual(h, [att.reshape(n_tok, D_MODEL)], [p["w_mo"]])
    return h, s_new, conv_new, vv


def kernel(x_prompt, x_sample, state_delta, state_conv, cache_mem_k, cache_mem_v, mem_prompt, w_in, conv_w, a_log, dt_bias, o_norm_g, sg_ln_g, sg_ln_b, sg_w, sg_b, w_out, norm_mix_g, norm_mem_g, mem_norm_g, w_mq, w_mk, w_mv, w_mo, norm_ffn_g, peer_wq, peer_keys, peer_u, peer_v, final_norm_g):
    depth = w_in.shape[0]
    bp, tp, d = x_prompt.shape
    bs, ts, _ = x_sample.shape
    hp = x_prompt.reshape(bp * tp, d)
    hs = x_sample.reshape(bs * ts, d)
    dp = ds = None
    mem_flat = mem_prompt.reshape(bp * MEM_LEN, d)
    sd_p, sc_p, mk_p, mv_p, sd_s, sc_s, vr_s = [], [], [], [], [], [], []
    for l in range(depth):
        p = _prep_layer(l, w_in, conv_w, a_log, dt_bias, o_norm_g, sg_ln_g, sg_ln_b, sg_w, sg_b, w_out,
                        norm_mix_g, norm_mem_g, mem_norm_g, w_mq, w_mk, w_mv, w_mo, norm_ffn_g,
                        peer_wq, peer_keys, peer_u, peer_v)
        mk, mv = _norm_proj(mem_flat, p["mem_norm_g"], p["w_mkv"], (D_MODEL, D_MODEL))
        mk = mk.reshape(bp, MEM_LEN, d)
        mv = mv.reshape(bp, MEM_LEN, d)
        s0 = jnp.zeros((bp, DN_HEADS, DN_DK, DN_DV), F32)
        c0 = jnp.zeros((bp, DN_CONV - 1, QKV_DIM), F32)
        hp, s_p, c_p, _ = _layer(hp, dp, bp, tp, mk, mv, s0, c0, p, False)
        hs, s_s, c_s, vv = _layer(hs, ds, bs, ts, cache_mem_k, cache_mem_v, state_delta[l], state_conv[l], p, True,
                                  cache_layer=l)
        dp = _peer(hp, p["norm_ffn_g"], p["peer_wq"], p["keys_pad"], p["u_tab"], p["v_tab"])
        ds = _peer(hs, p["norm_ffn_g"], p["peer_wq"], p["keys_pad"], p["u_tab"], p["v_tab"])
        sd_p.append(s_p)
        sc_p.append(c_p)
        mk_p.append(mk.reshape(bp, MEM_LEN, MEM_HEADS, MEM_HD))
        mv_p.append(mv.reshape(bp, MEM_LEN, MEM_HEADS, MEM_HD))
        sd_s.append(s_s)
        sc_s.append(c_s)
        vr_s.append(vv.reshape(bs, ts, SG_WIDTH))
    y_prompt = _add_norm(hp, dp, final_norm_g).reshape(bp, tp, d)
    y_sample = _add_norm(hs, ds, final_norm_g).reshape(bs, ts, d)
    return (y_prompt, y_sample, jnp.stack(sd_p), jnp.stack(sc_p), jnp.stack(mk_p), jnp.stack(mv_p),
            jnp.stack(sd_s), jnp.stack(sc_s), jnp.stack(vr_s))
```

```python
import functools
import math

import jax
import jax.numpy as jnp
from jax import lax
from jax.experimental import pallas as pl
from jax.experimental.pallas import tpu as pltpu

F32 = jnp.float32
BF16 = jnp.bfloat16
I32 = jnp.int32
U32 = jnp.uint32

D_MODEL = 1024
DN_HEADS = 4
DN_DK = 128
DN_DV = 128
DN_QK = DN_HEADS * DN_DK
DN_V = DN_HEADS * DN_DV
DN_CONV = 4
DN_CHUNK = 64
QKV_DIM = 2 * DN_QK + DN_V
SG_GROUPS = 4
SG_CH = 128
SG_WIDTH = SG_GROUPS * SG_CH
SG_CHUNK = 128
MEM_LEN = 256
MEM_HEADS = 4
MEM_HD = D_MODEL // MEM_HEADS
PEER_HEADS = 8
PEER_NKEYS = 128
PEER_NEXP = PEER_NKEYS * PEER_NKEYS
PEER_DKEY = 128
PEER_TOPK = 16
PEER_SLOTS = PEER_HEADS * PEER_TOPK
EPS = 1e-6

SUBLANES = 8
LANES = 128
ROW_TILES = D_MODEL // LANES
VMEM_BYTES_V7X = 64 * 1024 * 1024
HALF_EXPERTS = PEER_NEXP // 2

_HI = lax.Precision.HIGHEST


def _cparams(sem, vmem_mib=None):
    kw = dict(dimension_semantics=sem)
    if vmem_mib is not None:
        kw["vmem_limit_bytes"] = vmem_mib * 1024 * 1024
    return pltpu.CompilerParams(**kw)


def _rms(x, g):
    return x * lax.rsqrt(jnp.mean(x * x, axis=-1, keepdims=True) + EPS) * g


def _gelu(x):
    return jax.nn.gelu(x, approximate=True)


def _sigmoid(x):
    return 1.0 / (1.0 + jnp.exp(-x))


def _silu(x):
    return x * _sigmoid(x)


def _softplus(x):
    return jnp.maximum(x, 0.0) + jnp.log(1.0 + jnp.exp(-jnp.abs(x)))


def _rows_from_tiles(t_ref):
    return jnp.concatenate([t_ref[:, s, :] for s in range(ROW_TILES)], axis=-1)


def _norm_proj_kernel(*refs, widths, has_delta, emit_h):
    it = iter(refs)
    x_ref = next(it)
    d_ref = next(it) if has_delta else None
    g_ref = next(it)
    w_ref = next(it)
    outs = list(it)
    x = x_ref[...]
    if has_delta:
        x = x + _rows_from_tiles(d_ref)
    if emit_h:
        outs[0][...] = x
        outs = outs[1:]
    n = _rms(x, g_ref[...]).astype(BF16)
    y = jnp.dot(n, w_ref[...], preferred_element_type=F32)
    c = 0
    for o, wd in zip(outs, widths):
        o[...] = y[:, c:c + wd]
        c += wd


def _norm_proj(x, g, w, widths, delta=None, tm=512):
    n_tok, d = x.shape
    tm = min(tm, n_tok)
    has_delta = delta is not None
    row = pl.BlockSpec((tm, d), lambda i: (i, 0))
    tiles = pl.BlockSpec((tm, ROW_TILES, LANES), lambda i: (i, 0, 0))
    in_specs = [row] + ([tiles] if has_delta else []) + [
        pl.BlockSpec((1, d), lambda i: (0, 0)),
        pl.BlockSpec(w.shape, lambda i: (0, 0)),
    ]
    out_shape, out_specs = [], []
    if has_delta:
        out_shape.append(jax.ShapeDtypeStruct((n_tok, d), F32))
        out_specs.append(row)
    for wd in widths:
        out_shape.append(jax.ShapeDtypeStruct((n_tok, wd), F32))
        out_specs.append(pl.BlockSpec((tm, wd), lambda i: (i, 0)))
    args = [x] + ([delta] if has_delta else []) + [g.reshape(1, d), w]
    return pl.pallas_call(
        functools.partial(_norm_proj_kernel, widths=tuple(widths), has_delta=has_delta, emit_h=has_delta),
        grid=(n_tok // tm,),
        in_specs=in_specs,
        out_specs=out_specs,
        out_shape=out_shape,
        compiler_params=_cparams(("parallel",), 48),
        name="norm_proj",
    )(*args)


def _proj_res_kernel(*refs, n_in):
    h_ref = refs[0]
    a_refs = refs[1:1 + n_in]
    w_refs = refs[1 + n_in:1 + 2 * n_in]
    o_ref = refs[1 + 2 * n_in]
    acc = h_ref[...]
    for a, w in zip(a_refs, w_refs):
        acc = acc + jnp.dot(a[...].astype(BF16), w[...], preferred_element_type=F32)
    o_ref[...] = acc


def _proj_residual(h, acts, ws, tm=512):
    n_tok, d = h.shape
    tm = min(tm, n_tok)
    in_specs = [pl.BlockSpec((tm, d), lambda i: (i, 0))]
    in_specs += [pl.BlockSpec((tm, a.shape[1]), lambda i: (i, 0)) for a in acts]
    in_specs += [pl.BlockSpec(w.shape, lambda i: (0, 0)) for w in ws]
    return pl.pallas_call(
        functools.partial(_proj_res_kernel, n_in=len(acts)),
        grid=(n_tok // tm,),
        in_specs=in_specs,
        out_specs=pl.BlockSpec((tm, d), lambda i: (i, 0)),
        out_shape=jax.ShapeDtypeStruct((n_tok, d), F32),
        compiler_params=_cparams(("parallel",), 48),
        name="proj_residual",
    )(h, *acts, *ws)


def _add_norm_kernel(x_ref, d_ref, g_ref, o_ref):
    o_ref[...] = _rms(x_ref[...] + _rows_from_tiles(d_ref), g_ref[...])


def _add_norm(x, delta, g, tm=512):
    n_tok, d = x.shape
    tm = min(tm, n_tok)
    row = pl.BlockSpec((tm, d), lambda i: (i, 0))
    tiles = pl.BlockSpec((tm, ROW_TILES, LANES), lambda i: (i, 0, 0))
    return pl.pallas_call(
        _add_norm_kernel,
        grid=(n_tok // tm,),
        in_specs=[row, tiles, pl.BlockSpec((1, d), lambda i: (0, 0))],
        out_specs=row,
        out_shape=jax.ShapeDtypeStruct((n_tok, d), F32),
        compiler_params=_cparams(("parallel",)),
        name="add_norm",
    )(x, delta, g.reshape(1, d))


def _bdot(a, b):
    return jnp.dot(a.astype(BF16), b.astype(BF16), preferred_element_type=F32)


def _bdot_nt(a, b):
    return lax.dot_general(a.astype(BF16), b.astype(BF16), (((1,), (1,)), ((), ())), preferred_element_type=F32)


def _bdot_tn(a, b):
    return lax.dot_general(a.astype(BF16), b.astype(BF16), (((0,), (0,)), ((), ())), preferred_element_type=F32)


def _delta_kernel(qkv_ref, z_ref, ba_ref, prev_ref, s0_ref, cw_ref, al_ref, dt_ref, og_ref,
                  o_ref, s_ref, *scratch, period, carry):
    rows = qkv_ref.shape[0]
    n_sub = rows // period
    x = qkv_ref[...]
    if carry:
        tail_ref = scratch[0]
        j = pl.program_id(1)

        @pl.when(j == 0)
        def _():
            s_ref[0] = s0_ref[0]
            tail_ref[...] = jnp.zeros_like(tail_ref)
            tail_ref[SUBLANES - (DN_CONV - 1):, :] = prev_ref[0]

        xx = jnp.concatenate([tail_ref[...], x], axis=0)
        acc = jnp.zeros((rows, QKV_DIM), F32)
        for jj in range(DN_CONV):
            off = SUBLANES - (DN_CONV - 1) + jj
            acc = acc + xx[off:off + rows, :] * cw_ref[jj:jj + 1, :]
        tail_ref[...] = x[rows - SUBLANES:, :]
    else:
        pos = lax.broadcasted_iota(I32, (rows, QKV_DIM), 0) % period
        hist = prev_ref[...]
        acc = x * cw_ref[DN_CONV - 1:DN_CONV, :]
        for d in range(1, DN_CONV):
            shifted = jnp.where(pos >= d, pltpu.roll(x, d, 0), pltpu.roll(hist, d, 0))
            acc = acc + shifted * cw_ref[DN_CONV - 1 - d:DN_CONV - d, :]
    y = _silu(acc)
    ba = ba_ref[...]
    lane = lax.broadcasted_iota(I32, ba.shape, 1)
    bg = jnp.where(lane < DN_HEADS, _sigmoid(ba), -jnp.exp(al_ref[...]) * _softplus(ba + dt_ref[...]))

    row = lax.broadcasted_iota(I32, (rows, rows), 0)
    col = lax.broadcasted_iota(I32, (rows, rows), 1)
    same = (row // period) == (col // period)
    causal = same & (row >= col)
    strict = same & (row > col)
    eye = (row == col).astype(F32)
    gcs = jnp.dot(causal.astype(F32), bg, precision=_HI, preferred_element_type=F32)
    n_fac = int(math.log2(period)) - 1

    heads = range(DN_HEADS)
    qs, ks, vs, betas, gcl, decays, kbs = [], [], [], [], [], [], []
    for hd in heads:
        q = y[:, hd * DN_DK:(hd + 1) * DN_DK]
        k = y[:, DN_QK + hd * DN_DK:DN_QK + (hd + 1) * DN_DK]
        qs.append(q * lax.rsqrt(jnp.sum(q * q, axis=-1, keepdims=True) + EPS) * (DN_DK ** -0.5))
        ks.append(k * lax.rsqrt(jnp.sum(k * k, axis=-1, keepdims=True) + EPS))
        vs.append(y[:, 2 * DN_QK + hd * DN_DV:2 * DN_QK + (hd + 1) * DN_DV])
        betas.append(bg[:, hd:hd + 1])
        gc = gcs[:, DN_HEADS + hd:DN_HEADS + hd + 1]
        gc_cols = jnp.broadcast_to(gc, (rows, rows))
        decays.append(jnp.exp(jnp.where(causal, gc_cols - gc_cols.T, -jnp.inf)))
        gcl.append(gc)
        kbs.append(ks[hd] * betas[hd])
    a_s = [jnp.where(strict, _bdot_nt(kbs[hd], ks[hd]) * decays[hd], 0.0) for hd in heads]
    tinv = [eye - a for a in a_s]
    pw = [_bdot(a, a) for a in a_s]
    for f in range(n_fac):
        tinv = [t + _bdot(t, p) for t, p in zip(tinv, pw)]
        if f + 1 < n_fac:
            pw = [_bdot(p, p) for p in pw]
    us = [_bdot(tinv[hd], vs[hd] * betas[hd]) for hd in heads]
    ws = [_bdot(tinv[hd], kbs[hd] * jnp.exp(gcl[hd])) for hd in heads]
    qks = [_bdot_nt(qs[hd], ks[hd]) * decays[hd] for hd in heads]
    qgs = [qs[hd] * jnp.exp(gcl[hd]) for hd in heads]
    v_new = [[] for _ in heads]
    o_state = [[] for _ in heads]
    for c in range(n_sub):
        sl = slice(c * period, (c + 1) * period)
        for hd in heads:
            s_old = s_ref[0, hd] if carry else s0_ref[c, hd]
            vn = us[hd][sl] - _bdot(ws[hd][sl], s_old)
            o_state[hd].append(_bdot(qgs[hd][sl], s_old))
            glast = gcl[hd][(c + 1) * period - 1:(c + 1) * period, :]
            s_new = s_old * jnp.exp(glast) + _bdot_tn(ks[hd][sl] * jnp.exp(glast - gcl[hd][sl]), vn)
            if carry:
                s_ref[0, hd] = s_new
            else:
                s_ref[c, hd] = s_new
            v_new[hd].append(vn)
    for hd in heads:
        o = jnp.concatenate(o_state[hd], axis=0) + _bdot(qks[hd], jnp.concatenate(v_new[hd], axis=0))
        o = _rms(o, og_ref[...])
        o_ref[:, hd * DN_DV:(hd + 1) * DN_DV] = o * _silu(z_ref[:, hd * DN_DV:(hd + 1) * DN_DV])


_DELTA_ROWS = 256
_DELTA_SEQS = 16


def _delta_mixer(qkv, z, ba, c0, s0, b, t, conv_w, a_log, dt_bias, o_norm_g):
    al = jnp.zeros((1, LANES), F32).at[0, DN_HEADS:2 * DN_HEADS].set(a_log)
    dt = jnp.zeros((1, LANES), F32).at[0, DN_HEADS:2 * DN_HEADS].set(dt_bias)
    carry = t >= DN_CHUNK
    if carry:
        period, rows, spb = DN_CHUNK, _DELTA_ROWS, 1
        nj = t // rows
        grid = (b, nj)
        tok = lambda i, j: (i * nj + j, 0)
        st = lambda i, j: (i, 0, 0, 0)
        const = lambda i, j: (0, 0)
        prev, prev_spec = c0, pl.BlockSpec((1, DN_CONV - 1, QKV_DIM), lambda i, j: (i, 0, 0))
        scratch = [pltpu.VMEM((SUBLANES, QKV_DIM), F32)]
        sem = ("parallel", "arbitrary")
    else:
        period, spb = t, _DELTA_SEQS
        rows = spb * period
        grid = (b // spb,)
        tok = lambda i: (i, 0)
        st = lambda i: (i, 0, 0, 0)
        const = lambda i: (0, 0)
        hist = jnp.zeros((b, period, QKV_DIM), F32).at[:, period - (DN_CONV - 1):].set(c0)
        prev = jnp.roll(hist.reshape(b // spb, rows, QKV_DIM), -period, axis=1).reshape(b * period, QKV_DIM)
        prev_spec = pl.BlockSpec((rows, QKV_DIM), tok)
        scratch = []
        sem = ("parallel",)
    state = pl.BlockSpec((spb, DN_HEADS, DN_DK, DN_DV), st)
    return pl.pallas_call(
        functools.partial(_delta_kernel, period=period, carry=carry),
        grid=grid,
        in_specs=[
            pl.BlockSpec((rows, QKV_DIM), tok),
            pl.BlockSpec((rows, DN_V), tok),
            pl.BlockSpec((rows, LANES), tok),
            prev_spec,
            state,
            pl.BlockSpec((DN_CONV, QKV_DIM), const),
            pl.BlockSpec((1, LANES), const),
            pl.BlockSpec((1, LANES), const),
            pl.BlockSpec((1, DN_DV), const),
        ],
        out_specs=[pl.BlockSpec((rows, DN_V), tok), state],
        out_shape=[
            jax.ShapeDtypeStruct((b * t, DN_V), F32),
            jax.ShapeDtypeStruct((b, DN_HEADS, DN_DK, DN_DV), F32),
        ],
        scratch_shapes=scratch,
        compiler_params=_cparams(sem, 48),
        name="delta_mixer",
    )(qkv, z, ba, prev, s0, conv_w, al, dt, o_norm_g.reshape(1, DN_DV))


def _sgu_kernel(gu_ref, gv_ref, w_ref, b_ref, lg_ref, lb_ref, o_ref, *rest, period):
    vv_ref = rest[0] if rest else None
    n = gu_ref.shape[0]
    row = lax.broadcasted_iota(I32, (n, n), 0)
    col = lax.broadcasted_iota(I32, (n, n), 1)
    keep = row >= col
    if period < n:
        keep = keep & ((row // period) == (col // period))
    for g in range(SG_GROUPS):
        sl = slice(g * SG_CH, (g + 1) * SG_CH)
        u = _gelu(gu_ref[:, sl])
        x = _gelu(gv_ref[:, sl])
        mu = jnp.mean(x, axis=-1, keepdims=True)
        xc = x - mu
        var = jnp.mean(xc * xc, axis=-1, keepdims=True)
        vv = xc * lax.rsqrt(var + EPS) * lg_ref[:, sl] + lb_ref[:, sl]
        if vv_ref is not None:
            vv_ref[:, sl] = vv
        wm = jnp.where(keep, w_ref[g], 0.0).astype(BF16)
        mix = jnp.dot(wm, vv.astype(BF16), preferred_element_type=F32) + b_ref[g]
        o_ref[:, sl] = u * mix


def _sgu(gu, gv, w, bias, ln_g, ln_b, period, emit_vv):
    n_tok = gu.shape[0]
    blk = pl.BlockSpec((SG_CHUNK, SG_WIDTH), lambda i: (i, 0))
    out_shape = [jax.ShapeDtypeStruct((n_tok, SG_WIDTH), F32)]
    out_specs = [blk]
    if emit_vv:
        out_shape.append(jax.ShapeDtypeStruct((n_tok, SG_WIDTH), F32))
        out_specs.append(blk)
    return pl.pallas_call(
        functools.partial(_sgu_kernel, period=period),
        grid=(n_tok // SG_CHUNK,),
        in_specs=[
            blk, blk,
            pl.BlockSpec((SG_GROUPS, SG_CHUNK, SG_CHUNK), lambda i: (0, 0, 0)),
            pl.BlockSpec((SG_GROUPS, SG_CHUNK, SG_CH), lambda i: (0, 0, 0)),
            pl.BlockSpec((1, SG_WIDTH), lambda i: (0, 0)),
            pl.BlockSpec((1, SG_WIDTH), lambda i: (0, 0)),
        ],
        out_specs=out_specs,
        out_shape=out_shape,
        compiler_params=_cparams(("parallel",)),
        name="sgu",
    )(gu, gv, w, bias, ln_g.reshape(1, SG_WIDTH), ln_b.reshape(1, SG_WIDTH))


def _mem_attn_kernel(q_ref, k_ref, v_ref, o_ref):
    for hd in range(MEM_HEADS):
        sl = slice(hd * MEM_HD, (hd + 1) * MEM_HD)
        q = q_ref[0, :, sl].astype(BF16)
        if len(k_ref.shape) == 3 and k_ref.shape[1] == MEM_HEADS:
            k = k_ref[:, hd, :].astype(BF16)
            v = v_ref[:, hd, :].astype(BF16)
        else:
            k = k_ref[0, :, sl].astype(BF16)
            v = v_ref[0, :, sl].astype(BF16)
        s = lax.dot_general(q, k, (((1,), (1,)), ((), ())), preferred_element_type=F32) * (MEM_HD ** -0.5)
        m = jnp.max(s, axis=-1, keepdims=True)
        e = jnp.exp(s - m)
        p = e / jnp.sum(e, axis=-1, keepdims=True)
        o_ref[0, :, sl] = jnp.dot(p.astype(BF16), v, preferred_element_type=F32)


def _mem_attn(q, mk, mv, layer=None):
    b, t, d = q.shape
    tq = min(t, 512)
    if layer is None:
        kv = pl.BlockSpec((1, MEM_LEN, d), lambda i, j: (i, 0, 0))
    else:
        kv = pl.BlockSpec((None, None, MEM_LEN, MEM_HEADS, MEM_HD), lambda i, j: (layer, i, 0, 0, 0))
    return pl.pallas_call(
        _mem_attn_kernel,
        grid=(b, t // tq),
        in_specs=[pl.BlockSpec((1, tq, d), lambda i, j: (i, j, 0)), kv, kv],
        out_specs=pl.BlockSpec((1, tq, d), lambda i, j: (i, j, 0)),
        out_shape=jax.ShapeDtypeStruct((b, t, d), F32),
        compiler_params=_cparams(("parallel", "parallel")),
        name="mem_attn",
    )(q, mk, mv)


def _top_rows(s, n_top, rank=None, payload=None):
    if rank is None:
        rank = lax.broadcasted_iota(I32, s.shape, 0)
    big = jnp.int32(2 ** 30)
    vals, ids, pays = [], [], []
    for _ in range(n_top):
        m = jnp.max(s, axis=0, keepdims=True)
        i = jnp.min(jnp.where(s == m, rank, big), axis=0, keepdims=True)
        hit = rank == i
        vals.append(m)
        ids.append(i)
        if payload is not None:
            pays.append(jnp.max(jnp.where(hit, payload, -1), axis=0, keepdims=True))
        s = jnp.where(hit, -jnp.inf, s)
    out = [jnp.concatenate(vals, axis=0), jnp.concatenate(ids, axis=0)]
    if payload is not None:
        out.append(jnp.concatenate(pays, axis=0))
    return out


_PAIR_GROUPS = ((0, 0), (0, 8), (1, 0), (2, 0), (3, 0), (4, 0), (5, 0), (6, 0), (7, 0))


def _pair_candidates(sv0, si0, sv1, si1):
    sub = lax.broadcasted_iota(I32, (SUBLANES,) + sv0.shape[1:], 0)
    cand, flat, eid = [], [], []
    for a, b0 in _PAIR_GROUPS:
        cand.append(sv0[a:a + 1, :] + sv1[b0:b0 + SUBLANES, :])
        flat.append(a * PEER_TOPK + b0 + sub)
        eid.append(si0[a:a + 1, :] * PEER_NKEYS + si1[b0:b0 + SUBLANES, :])
    cand.append(sv0[SUBLANES:, :] + sv1[0:1, :])
    flat.append((SUBLANES + sub) * PEER_TOPK)
    eid.append(si0[SUBLANES:, :] * PEER_NKEYS + si1[0:1, :])
    return jnp.concatenate(cand, axis=0), jnp.concatenate(flat, axis=0), jnp.concatenate(eid, axis=0)


def _peer_route_kernel(x_ref, g_ref, wq_ref, keys_ref,
                       xn_ref, roff_ref, sh_ref, gate_ref, e_scr, g_scr):
    n = _rms(x_ref[...], g_ref[...])
    for s in range(ROW_TILES):
        xn_ref[:, s, :] = n[:, s * LANES:(s + 1) * LANES]
    q = jnp.dot(n.astype(BF16), wq_ref[...], preferred_element_type=F32)
    for hd in range(PEER_HEADS):
        qh = q[:, hd * PEER_DKEY:(hd + 1) * PEER_DKEY].astype(BF16)
        tops = []
        for p in range(2):
            s_t = lax.dot_general(keys_ref[hd, p], qh, (((1,), (1,)), ((), ())),
                                  preferred_element_type=F32)
            tops.append(_top_rows(s_t, PEER_TOPK))
        (sv0, si0), (sv1, si1) = tops
        cand, flat, ecand = _pair_candidates(sv0, si0, sv1, si1)
        cv, _, ce = _top_rows(cand, PEER_TOPK, rank=flat, payload=ecand)
        ex = jnp.exp(cv - cv[0:1, :])
        gate = ex / jnp.sum(ex, axis=0, keepdims=True)
        e_scr[hd * PEER_TOPK:(hd + 1) * PEER_TOPK, :] = ce
        g_scr[hd * PEER_TOPK:(hd + 1) * PEER_TOPK, :] = gate
    e_t = pltpu.bitcast(pltpu.bitcast(e_scr[...], F32).T, I32)
    hi = e_t >= HALF_EXPERTS
    roff_ref[...] = jnp.where(hi, e_t - HALF_EXPERTS, e_t) * ROW_TILES
    sh_ref[...] = jnp.where(hi, 0, 16)
    gate_ref[...] = g_scr[...].T


def _peer_route(h, g, wq, keys_pad):
    n_tok, d = h.shape
    tb = LANES
    row = pl.BlockSpec((tb, d), lambda i: (i, 0))
    slot = pl.BlockSpec((tb, PEER_SLOTS), lambda i: (i, 0))
    return pl.pallas_call(
        _peer_route_kernel,
        grid=(n_tok // tb,),
        in_specs=[
            row,
            pl.BlockSpec((1, d), lambda i: (0, 0)),
            pl.BlockSpec(wq.shape, lambda i: (0, 0)),
            pl.BlockSpec(keys_pad.shape, lambda i: (0, 0, 0, 0)),
        ],
        out_specs=[pl.BlockSpec((tb, ROW_TILES, LANES), lambda i: (i, 0, 0)), slot, slot, slot],
        out_shape=[
            jax.ShapeDtypeStruct((n_tok, ROW_TILES, LANES), F32),
            jax.ShapeDtypeStruct((n_tok, PEER_SLOTS), I32),
            jax.ShapeDtypeStruct((n_tok, PEER_SLOTS), I32),
            jax.ShapeDtypeStruct((n_tok, PEER_SLOTS), F32),
        ],
        scratch_shapes=[pltpu.VMEM((PEER_SLOTS, tb), I32), pltpu.VMEM((PEER_SLOTS, tb), F32)],
        compiler_params=_cparams(("parallel",), 48),
        name="peer_route",
    )(h, g.reshape(1, d), wq, keys_pad)


def _expert_tile(tab_ref, roff, shv):
    w = tab_ref[pl.ds(pl.multiple_of(roff, SUBLANES), SUBLANES), :]
    return pltpu.bitcast((w << shv) & jnp.uint32(0xFFFF0000), F32)


def _fold_rows(ps):
    sub = lax.broadcasted_iota(I32, (SUBLANES, LANES), 0)
    dist = SUBLANES // 2
    while len(ps) > 1:
        lo = (sub % (2 * dist)) < dist
        half = len(ps) // 2
        nxt = []
        for a in range(half):
            x, y = ps[a], ps[a + half]
            nxt.append(jnp.where(lo, x, pltpu.roll(y, dist, 0))
                       + jnp.where(lo, pltpu.roll(x, SUBLANES - dist, 0), y))
        ps = nxt
        dist //= 2
    return ps[0]


def _slot_columns(rows8, dst_ref):
    for u in range(SUBLANES):
        dst_ref[u] = jnp.broadcast_to(rows8[u:u + 1, :], (PEER_SLOTS, LANES)).T


def _bcast_row(ref, u, k):
    return jnp.broadcast_to(ref[u, k:k + 1, :], (SUBLANES, LANES))


def _peer_token_loop(tb, prep, token_group):
    n_pairs = tb // (2 * SUBLANES)
    prep(0, 0)

    def pair(i, carry):
        g0 = 2 * i
        prep(g0 + 1, 1)
        token_group(g0, 0)
        prep(jnp.minimum(g0 + 2, 2 * n_pairs - 1), 0)
        token_group(g0 + 1, 1)
        return carry

    lax.fori_loop(0, n_pairs, pair, 0)


def _peer_up_kernel(roff_ref, sh_ref, gate_ref, x_ref, tab_ref, c_ref, shs_a, shs_b):
    tb = sh_ref.shape[0]
    shs = (shs_a, shs_b)

    def prep(grp, par):
        r0 = pl.multiple_of(grp * SUBLANES, SUBLANES)
        _slot_columns(pltpu.bitcast(sh_ref[pl.ds(r0, SUBLANES), :], F32), shs[par])

    def token_group(grp, par):
        rows = []
        for u in range(SUBLANES):
            t = grp * SUBLANES + u
            x_t = x_ref[t]
            tiles = []
            for g in range(PEER_SLOTS // SUBLANES):
                ps = []
                for jj in range(SUBLANES):
                    k = g * SUBLANES + jj
                    shv = pltpu.bitcast(_bcast_row(shs[par], u, k), U32)
                    ps.append(_expert_tile(tab_ref, roff_ref[t, k], shv) * x_t)
                tiles.append(_fold_rows(ps))
            r = jnp.concatenate(tiles, axis=0)
            rows.append(jnp.sum(r.T, axis=0, keepdims=True))
        out = pl.ds(pl.multiple_of(grp * SUBLANES, SUBLANES), SUBLANES)
        c_ref[out, :] = gate_ref[out, :] * _gelu(jnp.concatenate(rows, axis=0))

    _peer_token_loop(tb, prep, token_group)


def _peer_down_kernel(roff_hbm, sh_ref, sh_next, c_ref, c_next, tab_ref, o_ref,
                      idx_a, idx_b, sh_a, sh_b, c_a, c_b, sem):
    i = pl.program_id(0)
    n = pl.num_programs(0)
    tb = sh_ref.shape[0]
    th = tb // 2
    idx = (idx_a, idx_b)

    def idx_copy(step, half):
        row0 = pl.multiple_of(step * tb + half * th, th)
        return pltpu.make_async_copy(roff_hbm.at[pl.ds(row0, th)], idx[half], sem.at[half])

    def build(src_sh, src_c, row, sh_t, c_t, u, after=None):
        sh_row = src_sh[row:row + 1, :]
        c_row = src_c[row:row + 1, :]
        if after is not None:
            zero = lax.shift_right_logical(pltpu.bitcast(after[0:1, :], U32), jnp.uint32(32))
            sh_row = sh_row + pltpu.bitcast(zero, I32)
            c_row = c_row + pltpu.bitcast(zero, F32)
        sh_t[u] = jnp.broadcast_to(pltpu.bitcast(sh_row, F32), (PEER_SLOTS, LANES)).T
        c_t[u] = jnp.broadcast_to(c_row, (PEER_SLOTS, LANES)).T

    def consume(half, sh_t, c_t, u):
        accs = [jnp.zeros((SUBLANES, LANES), F32) for _ in range(2)]
        for k in range(PEER_SLOTS):
            shv = pltpu.bitcast(_bcast_row(sh_t, u, k), U32)
            accs[k % 2] = accs[k % 2] + _expert_tile(tab_ref, idx[half][u, k], shv) * _bcast_row(c_t, u, k)
        acc = accs[0] + accs[1]
        o_ref[half * th + u] = acc
        return acc

    def half_step(half, use, fill, src_sh, src_c, row0):
        @pl.when(i >= 0)
        def _():
            acc = None
            for u in range(th):
                build(src_sh, src_c, row0 + u, fill[0], fill[1], u, after=acc)
                acc = consume(half, use[0], use[1], u)

    tiles = ((sh_a, c_a), (sh_b, c_b))

    @pl.when(i == 0)
    def _():
        idx_copy(0, 0).start()
        idx_copy(0, 1).start()
        for u in range(th):
            build(sh_ref, c_ref, u, sh_a, c_a, u)

    idx_copy(i, 0).wait()
    half_step(0, tiles[0], tiles[1], sh_ref, c_ref, th)

    @pl.when(i + 1 < n)
    def _():
        idx_copy(i + 1, 0).start()

    idx_copy(i, 1).wait()
    half_step(1, tiles[1], tiles[0], sh_next, c_next, 0)

    @pl.when(i + 1 < n)
    def _():
        idx_copy(i + 1, 1).start()


_PEER_TB = 64
_PEER_DOWN_TB = 64
_PEER_VMEM_MIB = 52
_COL_TILES = pltpu.VMEM((SUBLANES, PEER_SLOTS, LANES), F32)


def _table_spec(tab):
    return pl.BlockSpec(tab.shape, lambda i: (0, 0), pipeline_mode=pl.Buffered(1))


def _peer_up(roff, sh, gate, x_tiles, tab):
    n_tok = roff.shape[0]
    tb = _PEER_TB
    slot = pl.BlockSpec((tb, PEER_SLOTS), lambda i: (i, 0))
    return pl.pallas_call(
        _peer_up_kernel,
        grid=(n_tok // tb,),
        in_specs=[
            pl.BlockSpec((tb, PEER_SLOTS), lambda i: (i, 0), memory_space=pltpu.SMEM),
            slot, slot,
            pl.BlockSpec((tb, ROW_TILES, LANES), lambda i: (i, 0, 0)),
            _table_spec(tab),
        ],
        out_specs=slot,
        out_shape=jax.ShapeDtypeStruct((n_tok, PEER_SLOTS), F32),
        scratch_shapes=[_COL_TILES] * 2,
        compiler_params=_cparams(("arbitrary",), _PEER_VMEM_MIB),
        name="peer_up",
    )(roff, sh, gate, x_tiles, tab)


def _peer_down(roff, sh, coef, tab):
    n_tok = roff.shape[0]
    tb = _PEER_DOWN_TB
    th = tb // 2
    n_steps = n_tok // tb
    cur = pl.BlockSpec((tb, PEER_SLOTS), lambda i: (i, 0))
    nxt = pl.BlockSpec((tb, PEER_SLOTS), lambda i: (jnp.minimum(i + 1, n_steps - 1), 0))
    half_tiles = pltpu.VMEM((th, PEER_SLOTS, LANES), F32)
    return pl.pallas_call(
        _peer_down_kernel,
        grid=(n_steps,),
        in_specs=[pl.BlockSpec(memory_space=pl.ANY), cur, nxt, cur, nxt, _table_spec(tab)],
        out_specs=pl.BlockSpec((tb, ROW_TILES, LANES), lambda i: (i, 0, 0)),
        out_shape=jax.ShapeDtypeStruct((n_tok, ROW_TILES, LANES), F32),
        scratch_shapes=[pltpu.SMEM((th, PEER_SLOTS), I32)] * 2 + [half_tiles] * 4
        + [pltpu.SemaphoreType.DMA((2,))],
        compiler_params=_cparams(("arbitrary",), _PEER_VMEM_MIB),
        name="peer_down",
    )(roff, sh, sh, coef, coef, tab)


def _pack_kernel(lo_ref, hi_ref, o_ref):
    def bf16_bits(x):
        return pltpu.bitcast(x.astype(BF16).astype(F32), U32)

    word = bf16_bits(hi_ref[...]) | (bf16_bits(lo_ref[...]) >> 16)
    for s in range(ROW_TILES):
        o_ref[:, s, :] = word[:, s * LANES:(s + 1) * LANES]


def _pack_table(tabs, layer, rows=256):
    d = tabs.shape[2]
    n_blk = HALF_EXPERTS // rows
    packed = pl.pallas_call(
        _pack_kernel,
        grid=(n_blk,),
        in_specs=[pl.BlockSpec((None, rows, d), lambda i: (layer, i, 0)),
                  pl.BlockSpec((None, rows, d), lambda i: (layer, i + n_blk, 0))],
        out_specs=pl.BlockSpec((rows, ROW_TILES, LANES), lambda i: (i, 0, 0)),
        out_shape=jax.ShapeDtypeStruct((HALF_EXPERTS, ROW_TILES, LANES), U32),
        compiler_params=_cparams(("parallel",)),
        name="pack_table",
    )(tabs, tabs)
    return packed.reshape(HALF_EXPERTS * ROW_TILES, LANES)


def _peer(h, g, wq, keys_pad, u_tab, v_tab):
    xn, roff, sh, gate = _peer_route(h, g, wq, keys_pad)
    coef = _peer_up(roff, sh, gate, xn, u_tab)
    return _peer_down(roff, sh, coef, v_tab)


def _prep_layer(l, w_in, conv_w, a_log, dt_bias, o_norm_g, sg_ln_g, sg_ln_b, sg_w, sg_b, w_out,
                norm_mix_g, norm_mem_g, mem_norm_g, w_mq, w_mk, w_mv, w_mo, norm_ffn_g,
                peer_wq, peer_keys, peer_u, peer_v):
    o1 = QKV_DIM
    o2 = o1 + DN_V
    o4 = o2 + 2 * DN_HEADS
    o5 = o4 + SG_WIDTH
    wi = w_in[l]
    ba_cols = jnp.pad(wi[:, o2:o4], ((0, 0), (0, LANES - 2 * DN_HEADS)))
    w_in_r = jnp.concatenate([wi[:, :o2], wi[:, o4:], ba_cols], axis=1).astype(BF16)
    kz = jnp.zeros((PEER_HEADS, PEER_NKEYS, PEER_DKEY // 2), F32)
    keys_pad = jnp.stack([jnp.concatenate([peer_keys[l][:, 0], kz], axis=-1),
                          jnp.concatenate([kz, peer_keys[l][:, 1]], axis=-1)], axis=1).astype(BF16)
    return dict(
        w_in=w_in_r, conv_w=conv_w[l], a_log=a_log[l], dt_bias=dt_bias[l], o_norm_g=o_norm_g[l],
        sg_ln_g=sg_ln_g[l].reshape(-1), sg_ln_b=sg_ln_b[l].reshape(-1), sg_w=sg_w[l], sg_b=sg_b[l],
        w_out_a=w_out[l][:DN_V].astype(BF16), w_out_b=w_out[l][DN_V:].astype(BF16),
        norm_mix_g=norm_mix_g[l], norm_mem_g=norm_mem_g[l], mem_norm_g=mem_norm_g[l],
        w_mq=w_mq[l].astype(BF16), w_mkv=jnp.concatenate([w_mk[l], w_mv[l]], axis=1).astype(BF16),
        w_mo=w_mo[l].astype(BF16), norm_ffn_g=norm_ffn_g[l], peer_wq=peer_wq[l].astype(BF16),
        keys_pad=keys_pad, u_tab=_pack_table(peer_u, l), v_tab=_pack_table(peer_v, l),
    )


_IN_WIDTHS = (QKV_DIM, DN_V, SG_WIDTH, SG_WIDTH, LANES)


def _layer(h, delta, b, t, mk, mv, s0, c0, p, emit_vv, cache_layer=None):
    n_tok = b * t
    outs = _norm_proj(h, p["norm_mix_g"], p["w_in"], _IN_WIDTHS, delta=delta)
    if delta is not None:
        h, outs = outs[0], outs[1:]
    qkv, z, gu, gv, ba = outs
    o_a, s_new = _delta_mixer(qkv, z, ba, c0, s0, b, t, p["conv_w"], p["a_log"], p["dt_bias"], p["o_norm_g"])
    conv_new = qkv.reshape(b, t, QKV_DIM)[:, t - (DN_CONV - 1):, :]
    period = min(t, SG_CHUNK)
    reps = SG_CHUNK // period
    sg_w = jnp.tile(p["sg_w"][:, :period, :period], (1, reps, reps))
    sg_bias = jnp.broadcast_to(jnp.tile(p["sg_b"][:, :period], (1, reps))[:, :, None],
                               (SG_GROUPS, SG_CHUNK, SG_CH))
    sg_out = _sgu(gu, gv, sg_w, sg_bias, p["sg_ln_g"], p["sg_ln_b"], period, emit_vv)
    o_b = sg_out[0]
    vv = sg_out[1] if emit_vv else None
    h = _proj_residual(h, [o_a.reshape(n_tok, DN_V), o_b], [p["w_out_a"], p["w_out_b"]])
    (q,) = _norm_proj(h, p["norm_mem_g"], p["w_mq"], (D_MODEL,))
    att = _mem_attn(q.reshape(b, t, D_MODEL), mk, mv, layer=cache_layer)
    h = _proj_residual(h, [att.reshape(n_tok, D_MODEL)], [p["w_mo"]])
    return h, s_new, conv_new, vv


def kernel(x_prompt, x_sample, state_delta, state_conv, cache_mem_k, cache_mem_v, mem_prompt, w_in, conv_w, a_log, dt_bias, o_norm_g, sg_ln_g, sg_ln_b, sg_w, sg_b, w_out, norm_mix_g, norm_mem_g, mem_norm_g, w_mq, w_mk, w_mv, w_mo, norm_ffn_g, peer_wq, peer_keys, peer_u, peer_v, final_norm_g):
    depth = w_in.shape[0]
    bp, tp, d = x_prompt.shape
    bs, ts, _ = x_sample.shape
    hp = x_prompt.reshape(bp * tp, d)
    hs = x_sample.reshape(bs * ts, d)
    dp = ds = None
    mem_flat = mem_prompt.reshape(bp * MEM_LEN, d)
    sd_p, sc_p, mk_p, mv_p, sd_s, sc_s, vr_s = [], [], [], [], [], [], []
    for l in range(depth):
        p = _prep_layer(l, w_in, conv_w, a_log, dt_bias, o_norm_g, sg_ln_g, sg_ln_b, sg_w, sg_b, w_out,
                        norm_mix_g, norm_mem_g, mem_norm_g, w_mq, w_mk, w_mv, w_mo, norm_ffn_g,
                        peer_wq, peer_keys, peer_u, peer_v)
        mk, mv = _norm_proj(mem_flat, p["mem_norm_g"], p["w_mkv"], (D_MODEL, D_MODEL))
        mk = mk.reshape(bp, MEM_LEN, d)
        mv = mv.reshape(bp, MEM_LEN, d)
        s0 = jnp.zeros((bp, DN_HEADS, DN_DK, DN_DV), F32)
        c0 = jnp.zeros((bp, DN_CONV - 1, QKV_DIM), F32)
        hp, s_p, c_p, _ = _layer(hp, dp, bp, tp, mk, mv, s0, c0, p, False)
        hs, s_s, c_s, vv = _layer(hs, ds, bs, ts, cache_mem_k, cache_mem_v, state_delta[l], state_conv[l], p, True,
                                  cache_layer=l)
        dp = _peer(hp, p["norm_ffn_g"], p["peer_wq"], p["keys_pad"], p["u_tab"], p["v_tab"])
        ds = _peer(hs, p["norm_ffn_g"], p["peer_wq"], p["keys_pad"], p["u_tab"], p["v_tab"])
        sd_p.append(s_p)
        sc_p.append(c_p)
        mk_p.append(mk.reshape(bp, MEM_LEN, MEM_HEADS, MEM_HD))
        mv_p.append(mv.reshape(bp, MEM_LEN, MEM_HEADS, MEM_HD))
        sd_s.append(s_s)
        sc_s.append(c_s)
        vr_s.append(vv.reshape(bs, ts, SG_WIDTH))
    y_prompt = _add_norm(hp, dp, final_norm_g).reshape(bp, tp, d)
    y_sample = _add_norm(hs, ds, final_norm_g).reshape(bs, ts, d)
    return (y_prompt, y_sample, jnp.stack(sd_p), jnp.stack(sc_p), jnp.stack(mk_p), jnp.stack(mv_p),
            jnp.stack(sd_s), jnp.stack(sc_s), jnp.stack(vr_s))
```

```python
import functools
import math

import jax
import jax.numpy as jnp
from jax import lax
from jax.experimental import pallas as pl
from jax.experimental.pallas import tpu as pltpu

F32 = jnp.float32
BF16 = jnp.bfloat16
I32 = jnp.int32
U32 = jnp.uint32

D_MODEL = 1024
DN_HEADS = 4
DN_DK = 128
DN_DV = 128
DN_QK = DN_HEADS * DN_DK
DN_V = DN_HEADS * DN_DV
DN_CONV = 4
DN_CHUNK = 64
QKV_DIM = 2 * DN_QK + DN_V
SG_GROUPS = 4
SG_CH = 128
SG_WIDTH = SG_GROUPS * SG_CH
SG_CHUNK = 128
MEM_LEN = 256
MEM_HEADS = 4
MEM_HD = D_MODEL // MEM_HEADS
PEER_HEADS = 8
PEER_NKEYS = 128
PEER_NEXP = PEER_NKEYS * PEER_NKEYS
PEER_DKEY = 128
PEER_TOPK = 16
PEER_SLOTS = PEER_HEADS * PEER_TOPK
EPS = 1e-6

SUBLANES = 8
LANES = 128
ROW_TILES = D_MODEL // LANES
VMEM_BYTES_V7X = 64 * 1024 * 1024
HALF_EXPERTS = PEER_NEXP // 2

_HI = lax.Precision.HIGHEST


def _cparams(sem, vmem_mib=None):
    kw = dict(dimension_semantics=sem)
    if vmem_mib is not None:
        kw["vmem_limit_bytes"] = vmem_mib * 1024 * 1024
    return pltpu.CompilerParams(**kw)


def _rms(x, g):
    return x * lax.rsqrt(jnp.mean(x * x, axis=-1, keepdims=True) + EPS) * g


def _gelu(x):
    return jax.nn.gelu(x, approximate=True)


def _sigmoid(x):
    return 1.0 / (1.0 + jnp.exp(-x))


def _silu(x):
    return x * _sigmoid(x)


def _softplus(x):
    return jnp.maximum(x, 0.0) + jnp.log(1.0 + jnp.exp(-jnp.abs(x)))


def _rows_from_tiles(t_ref):
    return jnp.concatenate([t_ref[:, s, :] for s in range(ROW_TILES)], axis=-1)


def _norm_proj_kernel(*refs, widths, has_delta, emit_h):
    it = iter(refs)
    x_ref = next(it)
    d_ref = next(it) if has_delta else None
    g_ref = next(it)
    w_ref = next(it)
    outs = list(it)
    x = x_ref[...]
    if has_delta:
        x = x + _rows_from_tiles(d_ref)
    if emit_h:
        outs[0][...] = x
        outs = outs[1:]
    n = _rms(x, g_ref[...]).astype(BF16)
    y = jnp.dot(n, w_ref[...], preferred_element_type=F32)
    c = 0
    for o, wd in zip(outs, widths):
        o[...] = y[:, c:c + wd]
        c += wd


def _norm_proj(x, g, w, widths, delta=None, tm=512):
    n_tok, d = x.shape
    tm = min(tm, n_tok)
    has_delta = delta is not None
    row = pl.BlockSpec((tm, d), lambda i: (i, 0))
    tiles = pl.BlockSpec((tm, ROW_TILES, LANES), lambda i: (i, 0, 0))
    in_specs = [row] + ([tiles] if has_delta else []) + [
        pl.BlockSpec((1, d), lambda i: (0, 0)),
        pl.BlockSpec(w.shape, lambda i: (0, 0)),
    ]
    out_shape, out_specs = [], []
    if has_delta:
        out_shape.append(jax.ShapeDtypeStruct((n_tok, d), F32))
        out_specs.append(row)
    for wd in widths:
        out_shape.append(jax.ShapeDtypeStruct((n_tok, wd), F32))
        out_specs.append(pl.BlockSpec((tm, wd), lambda i: (i, 0)))
    args = [x] + ([delta] if has_delta else []) + [g.reshape(1, d), w]
    return pl.pallas_call(
        functools.partial(_norm_proj_kernel, widths=tuple(widths), has_delta=has_delta, emit_h=has_delta),
        grid=(n_tok // tm,),
        in_specs=in_specs,
        out_specs=out_specs,
        out_shape=out_shape,
        compiler_params=_cparams(("parallel",), 48),
        name="norm_proj",
    )(*args)


def _proj_res_kernel(*refs, n_in):
    h_ref = refs[0]
    a_refs = refs[1:1 + n_in]
    w_refs = refs[1 + n_in:1 + 2 * n_in]
    o_ref = refs[1 + 2 * n_in]
    acc = h_ref[...]
    for a, w in zip(a_refs, w_refs):
        acc = acc + jnp.dot(a[...].astype(BF16), w[...], preferred_element_type=F32)
    o_ref[...] = acc


def _proj_residual(h, acts, ws, tm=512):
    n_tok, d = h.shape
    tm = min(tm, n_tok)
    in_specs = [pl.BlockSpec((tm, d), lambda i: (i, 0))]
    in_specs += [pl.BlockSpec((tm, a.shape[1]), lambda i: (i, 0)) for a in acts]
    in_specs += [pl.BlockSpec(w.shape, lambda i: (0, 0)) for w in ws]
    return pl.pallas_call(
        functools.partial(_proj_res_kernel, n_in=len(acts)),
        grid=(n_tok // tm,),
        in_specs=in_specs,
        out_specs=pl.BlockSpec((tm, d), lambda i: (i, 0)),
        out_shape=jax.ShapeDtypeStruct((n_tok, d), F32),
        compiler_params=_cparams(("parallel",), 48),
        name="proj_residual",
    )(h, *acts, *ws)


def _add_norm_kernel(x_ref, d_ref, g_ref, o_ref):
    o_ref[...] = _rms(x_ref[...] + _rows_from_tiles(d_ref), g_ref[...])


def _add_norm(x, delta, g, tm=512):
    n_tok, d = x.shape
    tm = min(tm, n_tok)
    row = pl.BlockSpec((tm, d), lambda i: (i, 0))
    tiles = pl.BlockSpec((tm, ROW_TILES, LANES), lambda i: (i, 0, 0))
    return pl.pallas_call(
        _add_norm_kernel,
        grid=(n_tok // tm,),
        in_specs=[row, tiles, pl.BlockSpec((1, d), lambda i: (0, 0))],
        out_specs=row,
        out_shape=jax.ShapeDtypeStruct((n_tok, d), F32),
        compiler_params=_cparams(("parallel",)),
        name="add_norm",
    )(x, delta, g.reshape(1, d))


def _bdot(a, b):
    return jnp.dot(a.astype(BF16), b.astype(BF16), preferred_element_type=F32)


def _bdot_nt(a, b):
    return lax.dot_general(a.astype(BF16), b.astype(BF16), (((1,), (1,)), ((), ())), preferred_element_type=F32)


def _bdot_tn(a, b):
    return lax.dot_general(a.astype(BF16), b.astype(BF16), (((0,), (0,)), ((), ())), preferred_element_type=F32)


def _delta_kernel(qkv_ref, z_ref, ba_ref, prev_ref, s0_ref, cw_ref, al_ref, dt_ref, og_ref,
                  o_ref, s_ref, *scratch, period, carry):
    rows = qkv_ref.shape[0]
    n_sub = rows // period
    x = qkv_ref[...]
    if carry:
        tail_ref = scratch[0]
        j = pl.program_id(1)

        @pl.when(j == 0)
        def _():
            s_ref[0] = s0_ref[0]
            tail_ref[...] = jnp.zeros_like(tail_ref)
            tail_ref[SUBLANES - (DN_CONV - 1):, :] = prev_ref[0]

        xx = jnp.concatenate([tail_ref[...], x], axis=0)
        acc = jnp.zeros((rows, QKV_DIM), F32)
        for jj in range(DN_CONV):
            off = SUBLANES - (DN_CONV - 1) + jj
            acc = acc + xx[off:off + rows, :] * cw_ref[jj:jj + 1, :]
        tail_ref[...] = x[rows - SUBLANES:, :]
    else:
        pos = lax.broadcasted_iota(I32, (rows, QKV_DIM), 0) % period
        hist = prev_ref[...]
        acc = x * cw_ref[DN_CONV - 1:DN_CONV, :]
        for d in range(1, DN_CONV):
            shifted = jnp.where(pos >= d, pltpu.roll(x, d, 0), pltpu.roll(hist, d, 0))
            acc = acc + shifted * cw_ref[DN_CONV - 1 - d:DN_CONV - d, :]
    y = _silu(acc)
    ba = ba_ref[...]
    lane = lax.broadcasted_iota(I32, ba.shape, 1)
    bg = jnp.where(lane < DN_HEADS, _sigmoid(ba), -jnp.exp(al_ref[...]) * _softplus(ba + dt_ref[...]))

    row = lax.broadcasted_iota(I32, (rows, rows), 0)
    col = lax.broadcasted_iota(I32, (rows, rows), 1)
    same = (row // period) == (col // period)
    causal = same & (row >= col)
    strict = same & (row > col)
    eye = (row == col).astype(F32)
    gcs = jnp.dot(causal.astype(F32), bg, precision=_HI, preferred_element_type=F32)
    n_fac = int(math.log2(period)) - 1

    heads = range(DN_HEADS)
    qs, ks, vs, betas, gcl, decays, kbs = [], [], [], [], [], [], []
    for hd in heads:
        q = y[:, hd * DN_DK:(hd + 1) * DN_DK]
        k = y[:, DN_QK + hd * DN_DK:DN_QK + (hd + 1) * DN_DK]
        qs.append(q * lax.rsqrt(jnp.sum(q * q, axis=-1, keepdims=True) + EPS) * (DN_DK ** -0.5))
        ks.append(k * lax.rsqrt(jnp.sum(k * k, axis=-1, keepdims=True) + EPS))
        vs.append(y[:, 2 * DN_QK + hd * DN_DV:2 * DN_QK + (hd + 1) * DN_DV])
        betas.append(bg[:, hd:hd + 1])
        gc = gcs[:, DN_HEADS + hd:DN_HEADS + hd + 1]
        gc_cols = jnp.broadcast_to(gc, (rows, rows))
        decays.append(jnp.exp(jnp.where(causal, gc_cols - gc_cols.T, -jnp.inf)))
        gcl.append(gc)
        kbs.append(ks[hd] * betas[hd])
    a_s = [jnp.where(strict, _bdot_nt(kbs[hd], ks[hd]) * decays[hd], 0.0) for hd in heads]
    tinv = [eye - a for a in a_s]
    pw = [_bdot(a, a) for a in a_s]
    for f in range(n_fac):
        tinv = [t + _bdot(t, p) for t, p in zip(tinv, pw)]
        if f + 1 < n_fac:
            pw = [_bdot(p, p) for p in pw]
    us = [_bdot(tinv[hd], vs[hd] * betas[hd]) for hd in heads]
    ws = [_bdot(tinv[hd], kbs[hd] * jnp.exp(gcl[hd])) for hd in heads]
    qks = [_bdot_nt(qs[hd], ks[hd]) * decays[hd] for hd in heads]
    qgs = [qs[hd] * jnp.exp(gcl[hd]) for hd in heads]
    v_new = [[] for _ in heads]
    o_state = [[] for _ in heads]
    for c in range(n_sub):
        sl = slice(c * period, (c + 1) * period)
        for hd in heads:
            s_old = s_ref[0, hd] if carry else s0_ref[c, hd]
            vn = us[hd][sl] - _bdot(ws[hd][sl], s_old)
            o_state[hd].append(_bdot(qgs[hd][sl], s_old))
            glast = gcl[hd][(c + 1) * period - 1:(c + 1) * period, :]
            s_new = s_old * jnp.exp(glast) + _bdot_tn(ks[hd][sl] * jnp.exp(glast - gcl[hd][sl]), vn)
            if carry:
                s_ref[0, hd] = s_new
            else:
                s_ref[c, hd] = s_new
            v_new[hd].append(vn)
    for hd in heads:
        o = jnp.concatenate(o_state[hd], axis=0) + _bdot(qks[hd], jnp.concatenate(v_new[hd], axis=0))
        o = _rms(o, og_ref[...])
        o_ref[:, hd * DN_DV:(hd + 1) * DN_DV] = o * _silu(z_ref[:, hd * DN_DV:(hd + 1) * DN_DV])


_DELTA_ROWS = 256
_DELTA_SEQS = 16


def _delta_mixer(qkv, z, ba, c0, s0, b, t, conv_w, a_log, dt_bias, o_norm_g):
    al = jnp.zeros((1, LANES), F32).at[0, DN_HEADS:2 * DN_HEADS].set(a_log)
    dt = jnp.zeros((1, LANES), F32).at[0, DN_HEADS:2 * DN_HEADS].set(dt_bias)
    carry = t >= DN_CHUNK
    if carry:
        period, rows, spb = DN_CHUNK, _DELTA_ROWS, 1
        nj = t // rows
        grid = (b, nj)
        tok = lambda i, j: (i * nj + j, 0)
        st = lambda i, j: (i, 0, 0, 0)
        const = lambda i, j: (0, 0)
        prev, prev_spec = c0, pl.BlockSpec((1, DN_CONV - 1, QKV_DIM), lambda i, j: (i, 0, 0))
        scratch = [pltpu.VMEM((SUBLANES, QKV_DIM), F32)]
        sem = ("parallel", "arbitrary")
    else:
        period, spb = t, _DELTA_SEQS
        rows = spb * period
        grid = (b // spb,)
        tok = lambda i: (i, 0)
        st = lambda i: (i, 0, 0, 0)
        const = lambda i: (0, 0)
        hist = jnp.zeros((b, period, QKV_DIM), F32).at[:, period - (DN_CONV - 1):].set(c0)
        prev = jnp.roll(hist.reshape(b // spb, rows, QKV_DIM), -period, axis=1).reshape(b * period, QKV_DIM)
        prev_spec = pl.BlockSpec((rows, QKV_DIM), tok)
        scratch = []
        sem = ("parallel",)
    state = pl.BlockSpec((spb, DN_HEADS, DN_DK, DN_DV), st)
    return pl.pallas_call(
        functools.partial(_delta_kernel, period=period, carry=carry),
        grid=grid,
        in_specs=[
            pl.BlockSpec((rows, QKV_DIM), tok),
            pl.BlockSpec((rows, DN_V), tok),
            pl.BlockSpec((rows, LANES), tok),
            prev_spec,
            state,
            pl.BlockSpec((DN_CONV, QKV_DIM), const),
            pl.BlockSpec((1, LANES), const),
            pl.BlockSpec((1, LANES), const),
            pl.BlockSpec((1, DN_DV), const),
        ],
        out_specs=[pl.BlockSpec((rows, DN_V), tok), state],
        out_shape=[
            jax.ShapeDtypeStruct((b * t, DN_V), F32),
            jax.ShapeDtypeStruct((b, DN_HEADS, DN_DK, DN_DV), F32),
        ],
        scratch_shapes=scratch,
        compiler_params=_cparams(sem, 48),
        name="delta_mixer",
    )(qkv, z, ba, prev, s0, conv_w, al, dt, o_norm_g.reshape(1, DN_DV))


def _sgu_kernel(gu_ref, gv_ref, w_ref, b_ref, lg_ref, lb_ref, o_ref, *rest, period):
    vv_ref = rest[0] if rest else None
    n = gu_ref.shape[0]
    row = lax.broadcasted_iota(I32, (n, n), 0)
    col = lax.broadcasted_iota(I32, (n, n), 1)
    keep = row >= col
    if period < n:
        keep = keep & ((row // period) == (col // period))
    for g in range(SG_GROUPS):
        sl = slice(g * SG_CH, (g + 1) * SG_CH)
        u = _gelu(gu_ref[:, sl])
        x = _gelu(gv_ref[:, sl])
        mu = jnp.mean(x, axis=-1, keepdims=True)
        xc = x - mu
        var = jnp.mean(xc * xc, axis=-1, keepdims=True)
        vv = xc * lax.rsqrt(var + EPS) * lg_ref[:, sl] + lb_ref[:, sl]
        if vv_ref is not None:
            vv_ref[:, sl] = vv
        wm = jnp.where(keep, w_ref[g], 0.0).astype(BF16)
        mix = jnp.dot(wm, vv.astype(BF16), preferred_element_type=F32) + b_ref[g]
        o_ref[:, sl] = u * mix


def _sgu(gu, gv, w, bias, ln_g, ln_b, period, emit_vv):
    n_tok = gu.shape[0]
    blk = pl.BlockSpec((SG_CHUNK, SG_WIDTH), lambda i: (i, 0))
    out_shape = [jax.ShapeDtypeStruct((n_tok, SG_WIDTH), F32)]
    out_specs = [blk]
    if emit_vv:
        out_shape.append(jax.ShapeDtypeStruct((n_tok, SG_WIDTH), F32))
        out_specs.append(blk)
    return pl.pallas_call(
        functools.partial(_sgu_kernel, period=period),
        grid=(n_tok // SG_CHUNK,),
        in_specs=[
            blk, blk,
            pl.BlockSpec((SG_GROUPS, SG_CHUNK, SG_CHUNK), lambda i: (0, 0, 0)),
            pl.BlockSpec((SG_GROUPS, SG_CHUNK, SG_CH), lambda i: (0, 0, 0)),
            pl.BlockSpec((1, SG_WIDTH), lambda i: (0, 0)),
            pl.BlockSpec((1, SG_WIDTH), lambda i: (0, 0)),
        ],
        out_specs=out_specs,
        out_shape=out_shape,
        compiler_params=_cparams(("parallel",)),
        name="sgu",
    )(gu, gv, w, bias, ln_g.reshape(1, SG_WIDTH), ln_b.reshape(1, SG_WIDTH))


def _mem_attn_kernel(q_ref, k_ref, v_ref, o_ref):
    for hd in range(MEM_HEADS):
        sl = slice(hd * MEM_HD, (hd + 1) * MEM_HD)
        q = q_ref[0, :, sl].astype(BF16)
        if len(k_ref.shape) == 3 and k_ref.shape[1] == MEM_HEADS:
            k = k_ref[:, hd, :].astype(BF16)
            v = v_ref[:, hd, :].astype(BF16)
        else:
            k = k_ref[0, :, sl].astype(BF16)
            v = v_ref[0, :, sl].astype(BF16)
        s = lax.dot_general(q, k, (((1,), (1,)), ((), ())), preferred_element_type=F32) * (MEM_HD ** -0.5)
        m = jnp.max(s, axis=-1, keepdims=True)
        e = jnp.exp(s - m)
        p = e / jnp.sum(e, axis=-1, keepdims=True)
        o_ref[0, :, sl] = jnp.dot(p.astype(BF16), v, preferred_element_type=F32)


def _mem_attn(q, mk, mv, layer=None):
    b, t, d = q.shape
    tq = min(t, 512)
    if layer is None:
        kv = pl.BlockSpec((1, MEM_LEN, d), lambda i, j: (i, 0, 0))
    else:
        kv = pl.BlockSpec((None, None, MEM_LEN, MEM_HEADS, MEM_HD), lambda i, j: (layer, i, 0, 0, 0))
    return pl.pallas_call(
        _mem_attn_kernel,
        grid=(b, t // tq),
        in_specs=[pl.BlockSpec((1, tq, d), lambda i, j: (i, j, 0)), kv, kv],
        out_specs=pl.BlockSpec((1, tq, d), lambda i, j: (i, j, 0)),
        out_shape=jax.ShapeDtypeStruct((b, t, d), F32),
        compiler_params=_cparams(("parallel", "parallel")),
        name="mem_attn",
    )(q, mk, mv)


def _top_rows(s, n_top, rank=None, payload=None):
    if rank is None:
        rank = lax.broadcasted_iota(I32, s.shape, 0)
    big = jnp.int32(2 ** 30)
    vals, ids, pays = [], [], []
    for _ in range(n_top):
        m = jnp.max(s, axis=0, keepdims=True)
        i = jnp.min(jnp.where(s == m, rank, big), axis=0, keepdims=True)
        hit = rank == i
        vals.append(m)
        ids.append(i)
        if payload is not None:
            pays.append(jnp.max(jnp.where(hit, payload, -1), axis=0, keepdims=True))
        s = jnp.where(hit, -jnp.inf, s)
    out = [jnp.concatenate(vals, axis=0), jnp.concatenate(ids, axis=0)]
    if payload is not None:
        out.append(jnp.concatenate(pays, axis=0))
    return out


_PAIR_GROUPS = ((0, 0), (0, 8), (1, 0), (2, 0), (3, 0), (4, 0), (5, 0), (6, 0), (7, 0))


def _pair_candidates(sv0, si0, sv1, si1):
    sub = lax.broadcasted_iota(I32, (SUBLANES,) + sv0.shape[1:], 0)
    cand, flat, eid = [], [], []
    for a, b0 in _PAIR_GROUPS:
        cand.append(sv0[a:a + 1, :] + sv1[b0:b0 + SUBLANES, :])
        flat.append(a * PEER_TOPK + b0 + sub)
        eid.append(si0[a:a + 1, :] * PEER_NKEYS + si1[b0:b0 + SUBLANES, :])
    cand.append(sv0[SUBLANES:, :] + sv1[0:1, :])
    flat.append((SUBLANES + sub) * PEER_TOPK)
    eid.append(si0[SUBLANES:, :] * PEER_NKEYS + si1[0:1, :])
    return jnp.concatenate(cand, axis=0), jnp.concatenate(flat, axis=0), jnp.concatenate(eid, axis=0)


def _peer_route_kernel(x_ref, g_ref, wq_ref, keys_ref,
                       xn_ref, roff_ref, sh_ref, gate_ref, e_scr, g_scr):
    n = _rms(x_ref[...], g_ref[...])
    for s in range(ROW_TILES):
        xn_ref[:, s, :] = n[:, s * LANES:(s + 1) * LANES]
    q = jnp.dot(n.astype(BF16), wq_ref[...], preferred_element_type=F32)
    for hd in range(PEER_HEADS):
        qh = q[:, hd * PEER_DKEY:(hd + 1) * PEER_DKEY].astype(BF16)
        tops = []
        for p in range(2):
            s_t = lax.dot_general(keys_ref[hd, p], qh, (((1,), (1,)), ((), ())),
                                  preferred_element_type=F32)
            tops.append(_top_rows(s_t, PEER_TOPK))
        (sv0, si0), (sv1, si1) = tops
        cand, flat, ecand = _pair_candidates(sv0, si0, sv1, si1)
        cv, _, ce = _top_rows(cand, PEER_TOPK, rank=flat, payload=ecand)
        ex = jnp.exp(cv - cv[0:1, :])
        gate = ex / jnp.sum(ex, axis=0, keepdims=True)
        e_scr[hd * PEER_TOPK:(hd + 1) * PEER_TOPK, :] = ce
        g_scr[hd * PEER_TOPK:(hd + 1) * PEER_TOPK, :] = gate
    e_t = pltpu.bitcast(pltpu.bitcast(e_scr[...], F32).T, I32)
    hi = e_t >= HALF_EXPERTS
    roff_ref[...] = jnp.where(hi, e_t - HALF_EXPERTS, e_t) * ROW_TILES
    sh_ref[...] = jnp.where(hi, 0, 16)
    gate_ref[...] = g_scr[...].T


def _peer_route(h, g, wq, keys_pad):
    n_tok, d = h.shape
    tb = LANES
    row = pl.BlockSpec((tb, d), lambda i: (i, 0))
    slot = pl.BlockSpec((tb, PEER_SLOTS), lambda i: (i, 0))
    return pl.pallas_call(
        _peer_route_kernel,
        grid=(n_tok // tb,),
        in_specs=[
            row,
            pl.BlockSpec((1, d), lambda i: (0, 0)),
            pl.BlockSpec(wq.shape, lambda i: (0, 0)),
            pl.BlockSpec(keys_pad.shape, lambda i: (0, 0, 0, 0)),
        ],
        out_specs=[pl.BlockSpec((tb, ROW_TILES, LANES), lambda i: (i, 0, 0)), slot, slot, slot],
        out_shape=[
            jax.ShapeDtypeStruct((n_tok, ROW_TILES, LANES), F32),
            jax.ShapeDtypeStruct((n_tok, PEER_SLOTS), I32),
            jax.ShapeDtypeStruct((n_tok, PEER_SLOTS), I32),
            jax.ShapeDtypeStruct((n_tok, PEER_SLOTS), F32),
        ],
        scratch_shapes=[pltpu.VMEM((PEER_SLOTS, tb), I32), pltpu.VMEM((PEER_SLOTS, tb), F32)],
        compiler_params=_cparams(("parallel",), 48),
        name="peer_route",
    )(h, g.reshape(1, d), wq, keys_pad)


def _expert_tile(tab_ref, roff, shv):
    w = tab_ref[pl.ds(pl.multiple_of(roff, SUBLANES), SUBLANES), :]
    return pltpu.bitcast(w << shv, F32)


def _fold_rows(ps):
    sub = lax.broadcasted_iota(I32, (SUBLANES, LANES), 0)
    dist = SUBLANES // 2
    while len(ps) > 1:
        lo = (sub % (2 * dist)) < dist
        half = len(ps) // 2
        nxt = []
        for a in range(half):
            x, y = ps[a], ps[a + half]
            nxt.append(jnp.where(lo, x, pltpu.roll(y, dist, 0))
                       + jnp.where(lo, pltpu.roll(x, SUBLANES - dist, 0), y))
        ps = nxt
        dist //= 2
    return ps[0]


def _slot_columns(rows8, dst_ref):
    for u in range(SUBLANES):
        dst_ref[u] = jnp.broadcast_to(rows8[u:u + 1, :], (PEER_SLOTS, LANES)).T


def _bcast_row(ref, u, k):
    return jnp.broadcast_to(ref[u, k:k + 1, :], (SUBLANES, LANES))


def _peer_token_loop(tb, prep, token_group):
    n_pairs = tb // (2 * SUBLANES)
    prep(0, 0)

    def pair(i, carry):
        g0 = 2 * i
        prep(g0 + 1, 1)
        token_group(g0, 0)
        prep(jnp.minimum(g0 + 2, 2 * n_pairs - 1), 0)
        token_group(g0 + 1, 1)
        return carry

    lax.fori_loop(0, n_pairs, pair, 0)


def _peer_up_kernel(roff_ref, sh_ref, gate_ref, x_ref, tab_ref, c_ref, shs_a, shs_b):
    tb = sh_ref.shape[0]
    shs = (shs_a, shs_b)

    def prep(grp, par):
        r0 = pl.multiple_of(grp * SUBLANES, SUBLANES)
        _slot_columns(pltpu.bitcast(sh_ref[pl.ds(r0, SUBLANES), :], F32), shs[par])

    def token_group(grp, par):
        rows = []
        for u in range(SUBLANES):
            t = grp * SUBLANES + u
            x_t = x_ref[t]
            tiles = []
            for g in range(PEER_SLOTS // SUBLANES):
                ps = []
                for jj in range(SUBLANES):
                    k = g * SUBLANES + jj
                    shv = pltpu.bitcast(_bcast_row(shs[par], u, k), U32)
                    ps.append(_expert_tile(tab_ref, roff_ref[t, k], shv) * x_t)
                tiles.append(_fold_rows(ps))
            r = jnp.concatenate(tiles, axis=0)
            rows.append(jnp.sum(r.T, axis=0, keepdims=True))
        out = pl.ds(pl.multiple_of(grp * SUBLANES, SUBLANES), SUBLANES)
        c_ref[out, :] = gate_ref[out, :] * _gelu(jnp.concatenate(rows, axis=0))

    _peer_token_loop(tb, prep, token_group)


def _peer_down_kernel(roff_hbm, sh_ref, sh_next, c_ref, c_next, tab_ref, o_ref,
                      idx_a, idx_b, sh_a, sh_b, c_a, c_b, sem):
    i = pl.program_id(0)
    n = pl.num_programs(0)
    tb = sh_ref.shape[0]
    th = tb // 2
    idx = (idx_a, idx_b)

    def idx_copy(step, half):
        row0 = pl.multiple_of(step * tb + half * th, th)
        return pltpu.make_async_copy(roff_hbm.at[pl.ds(row0, th)], idx[half], sem.at[half])

    def build(src_sh, src_c, row, sh_t, c_t, u, after=None):
        sh_row = src_sh[row:row + 1, :]
        c_row = src_c[row:row + 1, :]
        if after is not None:
            zero = lax.shift_right_logical(pltpu.bitcast(after[0:1, :], U32), jnp.uint32(32))
            sh_row = sh_row + pltpu.bitcast(zero, I32)
            c_row = c_row + pltpu.bitcast(zero, F32)
        sh_t[u] = jnp.broadcast_to(pltpu.bitcast(sh_row, F32), (PEER_SLOTS, LANES)).T
        c_t[u] = jnp.broadcast_to(c_row, (PEER_SLOTS, LANES)).T

    def consume(half, sh_t, c_t, u):
        accs = [jnp.zeros((SUBLANES, LANES), F32) for _ in range(2)]
        for k in range(PEER_SLOTS):
            shv = pltpu.bitcast(_bcast_row(sh_t, u, k), U32)
            accs[k % 2] = accs[k % 2] + _expert_tile(tab_ref, idx[half][u, k], shv) * _bcast_row(c_t, u, k)
        acc = accs[0] + accs[1]
        o_ref[half * th + u] = acc
        return acc

    def half_step(half, use, fill, src_sh, src_c, row0):
        @pl.when(i >= 0)
        def _():
            acc = None
            for u in range(th):
                build(src_sh, src_c, row0 + u, fill[0], fill[1], u, after=acc)
                acc = consume(half, use[0], use[1], u)

    tiles = ((sh_a, c_a), (sh_b, c_b))

    @pl.when(i == 0)
    def _():
        idx_copy(0, 0).start()
        idx_copy(0, 1).start()
        for u in range(th):
            build(sh_ref, c_ref, u, sh_a, c_a, u)

    idx_copy(i, 0).wait()
    half_step(0, tiles[0], tiles[1], sh_ref, c_ref, th)

    @pl.when(i + 1 < n)
    def _():
        idx_copy(i + 1, 0).start()

    idx_copy(i, 1).wait()
    half_step(1, tiles[1], tiles[0], sh_next, c_next, 0)

    @pl.when(i + 1 < n)
    def _():
        idx_copy(i + 1, 1).start()


_PEER_TB = 64
_PEER_DOWN_TB = 64
_PEER_VMEM_MIB = 52
_COL_TILES = pltpu.VMEM((SUBLANES, PEER_SLOTS, LANES), F32)


def _table_spec(tab):
    return pl.BlockSpec(tab.shape, lambda i: (0, 0), pipeline_mode=pl.Buffered(1))


def _peer_up(roff, sh, gate, x_tiles, tab):
    n_tok = roff.shape[0]
    tb = _PEER_TB
    slot = pl.BlockSpec((tb, PEER_SLOTS), lambda i: (i, 0))
    return pl.pallas_call(
        _peer_up_kernel,
        grid=(n_tok // tb,),
        in_specs=[
            pl.BlockSpec((tb, PEER_SLOTS), lambda i: (i, 0), memory_space=pltpu.SMEM),
            slot, slot,
            pl.BlockSpec((tb, ROW_TILES, LANES), lambda i: (i, 0, 0)),
            _table_spec(tab),
        ],
        out_specs=slot,
        out_shape=jax.ShapeDtypeStruct((n_tok, PEER_SLOTS), F32),
        scratch_shapes=[_COL_TILES] * 2,
        compiler_params=_cparams(("arbitrary",), _PEER_VMEM_MIB),
        name="peer_up",
    )(roff, sh, gate, x_tiles, tab)


def _peer_down(roff, sh, coef, tab):
    n_tok = roff.shape[0]
    tb = _PEER_DOWN_TB
    th = tb // 2
    n_steps = n_tok // tb
    cur = pl.BlockSpec((tb, PEER_SLOTS), lambda i: (i, 0))
    nxt = pl.BlockSpec((tb, PEER_SLOTS), lambda i: (jnp.minimum(i + 1, n_steps - 1), 0))
    half_tiles = pltpu.VMEM((th, PEER_SLOTS, LANES), F32)
    return pl.pallas_call(
        _peer_down_kernel,
        grid=(n_steps,),
        in_specs=[pl.BlockSpec(memory_space=pl.ANY), cur, nxt, cur, nxt, _table_spec(tab)],
        out_specs=pl.BlockSpec((tb, ROW_TILES, LANES), lambda i: (i, 0, 0)),
        out_shape=jax.ShapeDtypeStruct((n_tok, ROW_TILES, LANES), F32),
        scratch_shapes=[pltpu.SMEM((th, PEER_SLOTS), I32)] * 2 + [half_tiles] * 4
        + [pltpu.SemaphoreType.DMA((2,))],
        compiler_params=_cparams(("arbitrary",), _PEER_VMEM_MIB),
        name="peer_down",
    )(roff, sh, sh, coef, coef, tab)


def _pack_kernel(lo_ref, hi_ref, o_ref):
    low = pltpu.bitcast(pltpu.bitcast(lo_ref[...].astype(BF16).astype(F32), U32) >> 16, I32)
    bits = pltpu.bitcast(hi_ref[...], I32)
    sign = bits & jnp.int32(-2 ** 31)
    mag = jnp.minimum(bits & jnp.int32(0x7FFFFFFF), jnp.int32(0x7F7F0000))
    top = jnp.maximum(mag + jnp.int32(0x8000) - low, 0) & jnp.int32(-65536)
    word = pltpu.bitcast(sign | top | low, U32)
    for s in range(ROW_TILES):
        o_ref[:, s, :] = word[:, s * LANES:(s + 1) * LANES]


def _pack_table(tabs, layer, rows=256):
    d = tabs.shape[2]
    n_blk = HALF_EXPERTS // rows
    packed = pl.pallas_call(
        _pack_kernel,
        grid=(n_blk,),
        in_specs=[pl.BlockSpec((None, rows, d), lambda i: (layer, i, 0)),
                  pl.BlockSpec((None, rows, d), lambda i: (layer, i + n_blk, 0))],
        out_specs=pl.BlockSpec((rows, ROW_TILES, LANES), lambda i: (i, 0, 0)),
        out_shape=jax.ShapeDtypeStruct((HALF_EXPERTS, ROW_TILES, LANES), U32),
        compiler_params=_cparams(("parallel",)),
        name="pack_table",
    )(tabs, tabs)
    return packed.reshape(HALF_EXPERTS * ROW_TILES, LANES)


def _peer(h, g, wq, keys_pad, u_tab, v_tab):
    xn, roff, sh, gate = _peer_route(h, g, wq, keys_pad)
    coef = _peer_up(roff, sh, gate, xn, u_tab)
    return _peer_down(roff, sh, coef, v_tab)


def _prep_layer(l, w_in, conv_w, a_log, dt_bias, o_norm_g, sg_ln_g, sg_ln_b, sg_w, sg_b, w_out,
                norm_mix_g, norm_mem_g, mem_norm_g, w_mq, w_mk, w_mv, w_mo, norm_ffn_g,
                peer_wq, peer_keys, peer_u, peer_v):
    o1 = QKV_DIM
    o2 = o1 + DN_V
    o4 = o2 + 2 * DN_HEADS
    o5 = o4 + SG_WIDTH
    wi = w_in[l]
    ba_cols = jnp.pad(wi[:, o2:o4], ((0, 0), (0, LANES - 2 * DN_HEADS)))
    w_in_r = jnp.concatenate([wi[:, :o2], wi[:, o4:], ba_cols], axis=1).astype(BF16)
    kz = jnp.zeros((PEER_HEADS, PEER_NKEYS, PEER_DKEY // 2), F32)
    keys_pad = jnp.stack([jnp.concatenate([peer_keys[l][:, 0], kz], axis=-1),
                          jnp.concatenate([kz, peer_keys[l][:, 1]], axis=-1)], axis=1).astype(BF16)
    return dict(
        w_in=w_in_r, conv_w=conv_w[l], a_log=a_log[l], dt_bias=dt_bias[l], o_norm_g=o_norm_g[l],
        sg_ln_g=sg_ln_g[l].reshape(-1), sg_ln_b=sg_ln_b[l].reshape(-1), sg_w=sg_w[l], sg_b=sg_b[l],
        w_out_a=w_out[l][:DN_V].astype(BF16), w_out_b=w_out[l][DN_V:].astype(BF16),
        norm_mix_g=norm_mix_g[l], norm_mem_g=norm_mem_g[l], mem_norm_g=mem_norm_g[l],
        w_mq=w_mq[l].astype(BF16), w_mkv=jnp.concatenate([w_mk[l], w_mv[l]], axis=1).astype(BF16),
        w_mo=w_mo[l].astype(BF16), norm_ffn_g=norm_ffn_g[l], peer_wq=peer_wq[l].astype(BF16),
        keys_pad=keys_pad, u_tab=_pack_table(peer_u, l), v_tab=_pack_table(peer_v, l),
    )


_IN_WIDTHS = (QKV_DIM, DN_V, SG_WIDTH, SG_WIDTH, LANES)


def _layer(h, delta, b, t, mk, mv, s0, c0, p, emit_vv, cache_layer=None):
    n_tok = b * t
    outs = _norm_proj(h, p["norm_mix_g"], p["w_in"], _IN_WIDTHS, delta=delta)
    if delta is not None:
        h, outs = outs[0], outs[1:]
    qkv, z, gu, gv, ba = outs
    o_a, s_new = _delta_mixer(qkv, z, ba, c0, s0, b, t, p["conv_w"], p["a_log"], p["dt_bias"], p["o_norm_g"])
    conv_new = qkv.reshape(b, t, QKV_DIM)[:, t - (DN_CONV - 1):, :]
    period = min(t, SG_CHUNK)
    reps = SG_CHUNK // period
    sg_w = jnp.tile(p["sg_w"][:, :period, :period], (1, reps, reps))
    sg_bias = jnp.broadcast_to(jnp.tile(p["sg_b"][:, :period], (1, reps))[:, :, None],
                               (SG_GROUPS, SG_CHUNK, SG_CH))
    sg_out = _sgu(gu, gv, sg_w, sg_bias, p["sg_ln_g"], p["sg_ln_b"], period, emit_vv)
    o_b = sg_out[0]
    vv = sg_out[1] if emit_vv else None
    h = _proj_residual(h, [o_a.reshape(n_tok, DN_V), o_b], [p["w_out_a"], p["w_out_b"]])
    (q,) = _norm_proj(h, p["norm_mem_g"], p["w_mq"], (D_MODEL,))
    att = _mem_attn(q.reshape(b, t, D_MODEL), mk, mv, layer=cache_layer)
    h = _proj_residual(h, [att.reshape(n_tok, D_MODEL)], [p["w_mo"]])
    return h, s_new, conv_new, vv


def kernel(x_prompt, x_sample, state_delta, state_conv, cache_mem_k, cache_mem_v, mem_prompt, w_in, conv_w, a_log, dt_bias, o_norm_g, sg_ln_g, sg_ln_b, sg_w, sg_b, w_out, norm_mix_g, norm_mem_g, mem_norm_g, w_mq, w_mk, w_mv, w_mo, norm_ffn_g, peer_wq, peer_keys, peer_u, peer_v, final_norm_g):
    depth = w_in.shape[0]
    bp, tp, d = x_prompt.shape
    bs, ts, _ = x_sample.shape
    hp = x_prompt.reshape(bp * tp, d)
    hs = x_sample.reshape(bs * ts, d)
    dp = ds = None
    mem_flat = mem_prompt.reshape(bp * MEM_LEN, d)
    sd_p, sc_p, mk_p, mv_p, sd_s, sc_s, vr_s = [], [], [], [], [], [], []
    for l in range(depth):
        p = _prep_layer(l, w_in, conv_w, a_log, dt_bias, o_norm_g, sg_ln_g, sg_ln_b, sg_w, sg_b, w_out,
                        norm_mix_g, norm_mem_g, mem_norm_g, w_mq, w_mk, w_mv, w_mo, norm_ffn_g,
                        peer_wq, peer_keys, peer_u, peer_v)
        mk, mv = _norm_proj(mem_flat, p["mem_norm_g"], p["w_mkv"], (D_MODEL, D_MODEL))
        mk = mk.reshape(bp, MEM_LEN, d)
        mv = mv.reshape(bp, MEM_LEN, d)
        s0 = jnp.zeros((bp, DN_HEADS, DN_DK, DN_DV), F32)
        c0 = jnp.zeros((bp, DN_CONV - 1, QKV_DIM), F32)
        hp, s_p, c_p, _ = _layer(hp, dp, bp, tp, mk, mv, s0, c0, p, False)
        hs, s_s, c_s, vv = _layer(hs, ds, bs, ts, cache_mem_k, cache_mem_v, state_delta[l], state_conv[l], p, True,
                                  cache_layer=l)
        dp = _peer(hp, p["norm_ffn_g"], p["peer_wq"], p["keys_pad"], p["u_tab"], p["v_tab"])
        ds = _peer(hs, p["norm_ffn_g"], p["peer_wq"], p["keys_pad"], p["u_tab"], p["v_tab"])
        sd_p.append(s_p)
        sc_p.append(c_p)
        mk_p.append(mk.reshape(bp, MEM_LEN, MEM_HEADS, MEM_HD))
        mv_p.append(mv.reshape(bp, MEM_LEN, MEM_HEADS, MEM_HD))
        sd_s.append(s_s)
        sc_s.append(c_s)
        vr_s.append(vv.reshape(bs, ts, SG_WIDTH))
    y_prompt = _add_norm(hp, dp, final_norm_g).reshape(bp, tp, d)
    y_sample = _add_norm(hs, ds, final_norm_g).reshape(bs, ts, d)
    return (y_prompt, y_sample, jnp.stack(sd_p), jnp.stack(sc_p), jnp.stack(mk_p), jnp.stack(mv_p),
            jnp.stack(sd_s), jnp.stack(sc_s), jnp.stack(vr_s))
```

```python
import functools
import math

import jax
import jax.numpy as jnp
from jax import lax
from jax.experimental import pallas as pl
from jax.experimental.pallas import tpu as pltpu

F32 = jnp.float32
BF16 = jnp.bfloat16
I32 = jnp.int32
U32 = jnp.uint32

D_MODEL = 1024
DN_HEADS = 4
DN_DK = 128
DN_DV = 128
DN_QK = DN_HEADS * DN_DK
DN_V = DN_HEADS * DN_DV
DN_CONV = 4
DN_CHUNK = 64
QKV_DIM = 2 * DN_QK + DN_V
SG_GROUPS = 4
SG_CH = 128
SG_WIDTH = SG_GROUPS * SG_CH
SG_CHUNK = 128
MEM_LEN = 256
MEM_HEADS = 4
MEM_HD = D_MODEL // MEM_HEADS
PEER_HEADS = 8
PEER_NKEYS = 128
PEER_NEXP = PEER_NKEYS * PEER_NKEYS
PEER_DKEY = 128
PEER_TOPK = 16
PEER_SLOTS = PEER_HEADS * PEER_TOPK
EPS = 1e-6

SUBLANES = 8
LANES = 128
ROW_TILES = D_MODEL // LANES
VMEM_BYTES_V7X = 64 * 1024 * 1024
HALF_EXPERTS = PEER_NEXP // 2

_HI = lax.Precision.HIGHEST


def _cparams(sem, vmem_mib=None):
    kw = dict(dimension_semantics=sem)
    if vmem_mib is not None:
        kw["vmem_limit_bytes"] = vmem_mib * 1024 * 1024
    return pltpu.CompilerParams(**kw)


def _rms(x, g):
    return x * lax.rsqrt(jnp.mean(x * x, axis=-1, keepdims=True) + EPS) * g


def _gelu(x):
    return jax.nn.gelu(x, approximate=True)


def _sigmoid(x):
    return 1.0 / (1.0 + jnp.exp(-x))


def _silu(x):
    return x * _sigmoid(x)


def _softplus(x):
    return jnp.maximum(x, 0.0) + jnp.log(1.0 + jnp.exp(-jnp.abs(x)))


def _rows_from_tiles(t_ref):
    return jnp.concatenate([t_ref[:, s, :] for s in range(ROW_TILES)], axis=-1)


def _norm_proj_kernel(*refs, widths, has_delta, emit_h):
    it = iter(refs)
    x_ref = next(it)
    d_ref = next(it) if has_delta else None
    g_ref = next(it)
    w_ref = next(it)
    outs = list(it)
    x = x_ref[...]
    if has_delta:
        x = x + _rows_from_tiles(d_ref)
    if emit_h:
        outs[0][...] = x
        outs = outs[1:]
    n = _rms(x, g_ref[...]).astype(BF16)
    y = jnp.dot(n, w_ref[...], preferred_element_type=F32)
    c = 0
    for o, wd in zip(outs, widths):
        o[...] = y[:, c:c + wd]
        c += wd


def _norm_proj(x, g, w, widths, delta=None, tm=512):
    n_tok, d = x.shape
    tm = min(tm, n_tok)
    has_delta = delta is not None
    row = pl.BlockSpec((tm, d), lambda i: (i, 0))
    tiles = pl.BlockSpec((tm, ROW_TILES, LANES), lambda i: (i, 0, 0))
    in_specs = [row] + ([tiles] if has_delta else []) + [
        pl.BlockSpec((1, d), lambda i: (0, 0)),
        pl.BlockSpec(w.shape, lambda i: (0, 0)),
    ]
    out_shape, out_specs = [], []
    if has_delta:
        out_shape.append(jax.ShapeDtypeStruct((n_tok, d), F32))
        out_specs.append(row)
    for wd in widths:
        out_shape.append(jax.ShapeDtypeStruct((n_tok, wd), F32))
        out_specs.append(pl.BlockSpec((tm, wd), lambda i: (i, 0)))
    args = [x] + ([delta] if has_delta else []) + [g.reshape(1, d), w]
    return pl.pallas_call(
        functools.partial(_norm_proj_kernel, widths=tuple(widths), has_delta=has_delta, emit_h=has_delta),
        grid=(n_tok // tm,),
        in_specs=in_specs,
        out_specs=out_specs,
        out_shape=out_shape,
        compiler_params=_cparams(("parallel",), 48),
        name="norm_proj",
    )(*args)


def _proj_res_kernel(*refs, n_in):
    h_ref = refs[0]
    a_refs = refs[1:1 + n_in]
    w_refs = refs[1 + n_in:1 + 2 * n_in]
    o_ref = refs[1 + 2 * n_in]
    acc = h_ref[...]
    for a, w in zip(a_refs, w_refs):
        acc = acc + jnp.dot(a[...].astype(BF16), w[...], preferred_element_type=F32)
    o_ref[...] = acc


def _proj_residual(h, acts, ws, tm=512):
    n_tok, d = h.shape
    tm = min(tm, n_tok)
    in_specs = [pl.BlockSpec((tm, d), lambda i: (i, 0))]
    in_specs += [pl.BlockSpec((tm, a.shape[1]), lambda i: (i, 0)) for a in acts]
    in_specs += [pl.BlockSpec(w.shape, lambda i: (0, 0)) for w in ws]
    return pl.pallas_call(
        functools.partial(_proj_res_kernel, n_in=len(acts)),
        grid=(n_tok // tm,),
        in_specs=in_specs,
        out_specs=pl.BlockSpec((tm, d), lambda i: (i, 0)),
        out_shape=jax.ShapeDtypeStruct((n_tok, d), F32),
        compiler_params=_cparams(("parallel",), 48),
        name="proj_residual",
    )(h, *acts, *ws)


def _add_norm_kernel(x_ref, d_ref, g_ref, o_ref):
    o_ref[...] = _rms(x_ref[...] + _rows_from_tiles(d_ref), g_ref[...])


def _add_norm(x, delta, g, tm=512):
    n_tok, d = x.shape
    tm = min(tm, n_tok)
    row = pl.BlockSpec((tm, d), lambda i: (i, 0))
    tiles = pl.BlockSpec((tm, ROW_TILES, LANES), lambda i: (i, 0, 0))
    return pl.pallas_call(
        _add_norm_kernel,
        grid=(n_tok // tm,),
        in_specs=[row, tiles, pl.BlockSpec((1, d), lambda i: (0, 0))],
        out_specs=row,
        out_shape=jax.ShapeDtypeStruct((n_tok, d), F32),
        compiler_params=_cparams(("parallel",)),
        name="add_norm",
    )(x, delta, g.reshape(1, d))


def _bdot(a, b):
    return jnp.dot(a.astype(BF16), b.astype(BF16), preferred_element_type=F32)


def _bdot_nt(a, b):
    return lax.dot_general(a.astype(BF16), b.astype(BF16), (((1,), (1,)), ((), ())), preferred_element_type=F32)


def _bdot_tn(a, b):
    return lax.dot_general(a.astype(BF16), b.astype(BF16), (((0,), (0,)), ((), ())), preferred_element_type=F32)


def _delta_kernel(qkv_ref, z_ref, ba_ref, prev_ref, s0_ref, cw_ref, al_ref, dt_ref, og_ref,
                  o_ref, s_ref, *scratch, period, carry):
    rows = qkv_ref.shape[0]
    n_sub = rows // period
    x = qkv_ref[...]
    if carry:
        tail_ref = scratch[0]
        j = pl.program_id(1)

        @pl.when(j == 0)
        def _():
            s_ref[0] = s0_ref[0]
            tail_ref[...] = jnp.zeros_like(tail_ref)
            tail_ref[SUBLANES - (DN_CONV - 1):, :] = prev_ref[0]

        xx = jnp.concatenate([tail_ref[...], x], axis=0)
        acc = jnp.zeros((rows, QKV_DIM), F32)
        for jj in range(DN_CONV):
            off = SUBLANES - (DN_CONV - 1) + jj
            acc = acc + xx[off:off + rows, :] * cw_ref[jj:jj + 1, :]
        tail_ref[...] = x[rows - SUBLANES:, :]
    else:
        pos = lax.broadcasted_iota(I32, (rows, QKV_DIM), 0) % period
        hist = prev_ref[...]
        acc = x * cw_ref[DN_CONV - 1:DN_CONV, :]
        for d in range(1, DN_CONV):
            shifted = jnp.where(pos >= d, pltpu.roll(x, d, 0), pltpu.roll(hist, d, 0))
            acc = acc + shifted * cw_ref[DN_CONV - 1 - d:DN_CONV - d, :]
    y = _silu(acc)
    ba = ba_ref[...]
    lane = lax.broadcasted_iota(I32, ba.shape, 1)
    bg = jnp.where(lane < DN_HEADS, _sigmoid(ba), -jnp.exp(al_ref[...]) * _softplus(ba + dt_ref[...]))

    row = lax.broadcasted_iota(I32, (rows, rows), 0)
    col = lax.broadcasted_iota(I32, (rows, rows), 1)
    same = (row // period) == (col // period)
    causal = same & (row >= col)
    strict = same & (row > col)
    eye = (row == col).astype(F32)
    gcs = jnp.dot(causal.astype(F32), bg, precision=_HI, preferred_element_type=F32)
    n_fac = int(math.log2(period)) - 1

    heads = range(DN_HEADS)
    qs, ks, vs, betas, gcl, decays, kbs = [], [], [], [], [], [], []
    for hd in heads:
        q = y[:, hd * DN_DK:(hd + 1) * DN_DK]
        k = y[:, DN_QK + hd * DN_DK:DN_QK + (hd + 1) * DN_DK]
        qs.append(q * lax.rsqrt(jnp.sum(q * q, axis=-1, keepdims=True) + EPS) * (DN_DK ** -0.5))
        ks.append(k * lax.rsqrt(jnp.sum(k * k, axis=-1, keepdims=True) + EPS))
        vs.append(y[:, 2 * DN_QK + hd * DN_DV:2 * DN_QK + (hd + 1) * DN_DV])
        betas.append(bg[:, hd:hd + 1])
        gc = gcs[:, DN_HEADS + hd:DN_HEADS + hd + 1]
        gc_cols = jnp.broadcast_to(gc, (rows, rows))
        decays.append(jnp.exp(jnp.where(causal, gc_cols - gc_cols.T, -jnp.inf)))
        gcl.append(gc)
        kbs.append(ks[hd] * betas[hd])
    a_s = [jnp.where(strict, _bdot_nt(kbs[hd], ks[hd]) * decays[hd], 0.0) for hd in heads]
    tinv = [eye - a for a in a_s]
    pw = [_bdot(a, a) for a in a_s]
    for f in range(n_fac):
        tinv = [t + _bdot(t, p) for t, p in zip(tinv, pw)]
        if f + 1 < n_fac:
            pw = [_bdot(p, p) for p in pw]
    us = [_bdot(tinv[hd], vs[hd] * betas[hd]) for hd in heads]
    ws = [_bdot(tinv[hd], kbs[hd] * jnp.exp(gcl[hd])) for hd in heads]
    qks = [_bdot_nt(qs[hd], ks[hd]) * decays[hd] for hd in heads]
    qgs = [qs[hd] * jnp.exp(gcl[hd]) for hd in heads]
    v_new = [[] for _ in heads]
    o_state = [[] for _ in heads]
    for c in range(n_sub):
        sl = slice(c * period, (c + 1) * period)
        for hd in heads:
            s_old = s_ref[0, hd] if carry else s0_ref[c, hd]
            vn = us[hd][sl] - _bdot(ws[hd][sl], s_old)
            o_state[hd].append(_bdot(qgs[hd][sl], s_old))
            glast = gcl[hd][(c + 1) * period - 1:(c + 1) * period, :]
            s_new = s_old * jnp.exp(glast) + _bdot_tn(ks[hd][sl] * jnp.exp(glast - gcl[hd][sl]), vn)
            if carry:
                s_ref[0, hd] = s_new
            else:
                s_ref[c, hd] = s_new
            v_new[hd].append(vn)
    for hd in heads:
        o = jnp.concatenate(o_state[hd], axis=0) + _bdot(qks[hd], jnp.concatenate(v_new[hd], axis=0))
        o = _rms(o, og_ref[...])
        o_ref[:, hd * DN_DV:(hd + 1) * DN_DV] = o * _silu(z_ref[:, hd * DN_DV:(hd + 1) * DN_DV])


_DELTA_ROWS = 256
_DELTA_SEQS = 16


def _delta_mixer(qkv, z, ba, c0, s0, b, t, conv_w, a_log, dt_bias, o_norm_g):
    al = jnp.zeros((1, LANES), F32).at[0, DN_HEADS:2 * DN_HEADS].set(a_log)
    dt = jnp.zeros((1, LANES), F32).at[0, DN_HEADS:2 * DN_HEADS].set(dt_bias)
    carry = t >= DN_CHUNK
    if carry:
        period, rows, spb = DN_CHUNK, _DELTA_ROWS, 1
        nj = t // rows
        grid = (b, nj)
        tok = lambda i, j: (i * nj + j, 0)
        st = lambda i, j: (i, 0, 0, 0)
        const = lambda i, j: (0, 0)
        prev, prev_spec = c0, pl.BlockSpec((1, DN_CONV - 1, QKV_DIM), lambda i, j: (i, 0, 0))
        scratch = [pltpu.VMEM((SUBLANES, QKV_DIM), F32)]
        sem = ("parallel", "arbitrary")
    else:
        period, spb = t, _DELTA_SEQS
        rows = spb * period
        grid = (b // spb,)
        tok = lambda i: (i, 0)
        st = lambda i: (i, 0, 0, 0)
        const = lambda i: (0, 0)
        hist = jnp.zeros((b, period, QKV_DIM), F32).at[:, period - (DN_CONV - 1):].set(c0)
        prev = jnp.roll(hist.reshape(b // spb, rows, QKV_DIM), -period, axis=1).reshape(b * period, QKV_DIM)
        prev_spec = pl.BlockSpec((rows, QKV_DIM), tok)
        scratch = []
        sem = ("parallel",)
    state = pl.BlockSpec((spb, DN_HEADS, DN_DK, DN_DV), st)
    return pl.pallas_call(
        functools.partial(_delta_kernel, period=period, carry=carry),
        grid=grid,
        in_specs=[
            pl.BlockSpec((rows, QKV_DIM), tok),
            pl.BlockSpec((rows, DN_V), tok),
            pl.BlockSpec((rows, LANES), tok),
            prev_spec,
            state,
            pl.BlockSpec((DN_CONV, QKV_DIM), const),
            pl.BlockSpec((1, LANES), const),
            pl.BlockSpec((1, LANES), const),
            pl.BlockSpec((1, DN_DV), const),
        ],
        out_specs=[pl.BlockSpec((rows, DN_V), tok), state],
        out_shape=[
            jax.ShapeDtypeStruct((b * t, DN_V), F32),
            jax.ShapeDtypeStruct((b, DN_HEADS, DN_DK, DN_DV), F32),
        ],
        scratch_shapes=scratch,
        compiler_params=_cparams(sem, 48),
        name="delta_mixer",
    )(qkv, z, ba, prev, s0, conv_w, al, dt, o_norm_g.reshape(1, DN_DV))


def _sgu_kernel(gu_ref, gv_ref, w_ref, b_ref, lg_ref, lb_ref, o_ref, *rest, period):
    vv_ref = rest[0] if rest else None
    n = gu_ref.shape[0]
    row = lax.broadcasted_iota(I32, (n, n), 0)
    col = lax.broadcasted_iota(I32, (n, n), 1)
    keep = row >= col
    if period < n:
        keep = keep & ((row // period) == (col // period))
    for g in range(SG_GROUPS):
        sl = slice(g * SG_CH, (g + 1) * SG_CH)
        u = _gelu(gu_ref[:, sl])
        x = _gelu(gv_ref[:, sl])
        mu = jnp.mean(x, axis=-1, keepdims=True)
        xc = x - mu
        var = jnp.mean(xc * xc, axis=-1, keepdims=True)
        vv = xc * lax.rsqrt(var + EPS) * lg_ref[:, sl] + lb_ref[:, sl]
        if vv_ref is not None:
            vv_ref[:, sl] = vv
        wm = jnp.where(keep, w_ref[g], 0.0).astype(BF16)
        mix = jnp.dot(wm, vv.astype(BF16), preferred_element_type=F32) + b_ref[g]
        o_ref[:, sl] = u * mix


def _sgu(gu, gv, w, bias, ln_g, ln_b, period, emit_vv):
    n_tok = gu.shape[0]
    blk = pl.BlockSpec((SG_CHUNK, SG_WIDTH), lambda i: (i, 0))
    out_shape = [jax.ShapeDtypeStruct((n_tok, SG_WIDTH), F32)]
    out_specs = [blk]
    if emit_vv:
        out_shape.append(jax.ShapeDtypeStruct((n_tok, SG_WIDTH), F32))
        out_specs.append(blk)
    return pl.pallas_call(
        functools.partial(_sgu_kernel, period=period),
        grid=(n_tok // SG_CHUNK,),
        in_specs=[
            blk, blk,
            pl.BlockSpec((SG_GROUPS, SG_CHUNK, SG_CHUNK), lambda i: (0, 0, 0)),
            pl.BlockSpec((SG_GROUPS, SG_CHUNK, SG_CH), lambda i: (0, 0, 0)),
            pl.BlockSpec((1, SG_WIDTH), lambda i: (0, 0)),
            pl.BlockSpec((1, SG_WIDTH), lambda i: (0, 0)),
        ],
        out_specs=out_specs,
        out_shape=out_shape,
        compiler_params=_cparams(("parallel",)),
        name="sgu",
    )(gu, gv, w, bias, ln_g.reshape(1, SG_WIDTH), ln_b.reshape(1, SG_WIDTH))


def _mem_attn_kernel(q_ref, k_ref, v_ref, o_ref):
    for hd in range(MEM_HEADS):
        sl = slice(hd * MEM_HD, (hd + 1) * MEM_HD)
        q = q_ref[0, :, sl].astype(BF16)
        k = k_ref[0, :, sl].astype(BF16)
        v = v_ref[0, :, sl].astype(BF16)
        s = lax.dot_general(q, k, (((1,), (1,)), ((), ())), preferred_element_type=F32) * (MEM_HD ** -0.5)
        m = jnp.max(s, axis=-1, keepdims=True)
        e = jnp.exp(s - m)
        p = e / jnp.sum(e, axis=-1, keepdims=True)
        o_ref[0, :, sl] = jnp.dot(p.astype(BF16), v, preferred_element_type=F32)


def _mem_attn_cache_kernel(q_ref, k_ref, v_ref, o_ref):
    t = q_ref.shape[1]
    q_all = jnp.concatenate([q_ref[0, :, hd * MEM_HD:(hd + 1) * MEM_HD] for hd in range(MEM_HEADS)], axis=0)
    k2 = k_ref[...].reshape(MEM_LEN * MEM_HEADS, MEM_HD).astype(BF16)
    v2 = v_ref[...].reshape(MEM_LEN * MEM_HEADS, MEM_HD).astype(BF16)
    s = lax.dot_general(q_all.astype(BF16), k2, (((1,), (1,)), ((), ())),
                        preferred_element_type=F32) * (MEM_HD ** -0.5)
    q_head = lax.broadcasted_iota(I32, s.shape, 0) // t
    k_head = lax.broadcasted_iota(I32, s.shape, 1) % MEM_HEADS
    s = jnp.where(q_head == k_head, s, -jnp.inf)
    m = jnp.max(s, axis=-1, keepdims=True)
    e = jnp.exp(s - m)
    p = e / jnp.sum(e, axis=-1, keepdims=True)
    out = jnp.dot(p.astype(BF16), v2, preferred_element_type=F32)
    for hd in range(MEM_HEADS):
        o_ref[0, :, hd * MEM_HD:(hd + 1) * MEM_HD] = out[hd * t:(hd + 1) * t, :]


def _mem_attn(q, mk, mv, layer=None):
    b, t, d = q.shape
    tq = min(t, 512)
    if layer is None:
        kv = pl.BlockSpec((1, MEM_LEN, d), lambda i, j: (i, 0, 0))
        body = _mem_attn_kernel
    else:
        kv = pl.BlockSpec((None, None, MEM_LEN, MEM_HEADS, MEM_HD), lambda i, j: (layer, i, 0, 0, 0))
        body = _mem_attn_cache_kernel
    return pl.pallas_call(
        body,
        grid=(b, t // tq),
        in_specs=[pl.BlockSpec((1, tq, d), lambda i, j: (i, j, 0)), kv, kv],
        out_specs=pl.BlockSpec((1, tq, d), lambda i, j: (i, j, 0)),
        out_shape=jax.ShapeDtypeStruct((b, t, d), F32),
        compiler_params=_cparams(("parallel", "parallel")),
        name="mem_attn",
    )(q, mk, mv)


def _top_rows(s, n_top, rank=None, payload=None):
    if rank is None:
        rank = lax.broadcasted_iota(I32, s.shape, 0)
    big = jnp.int32(2 ** 30)
    vals, ids, pays = [], [], []
    for _ in range(n_top):
        m = jnp.max(s, axis=0, keepdims=True)
        i = jnp.min(jnp.where(s == m, rank, big), axis=0, keepdims=True)
        hit = rank == i
        vals.append(m)
        ids.append(i)
        if payload is not None:
            pays.append(jnp.max(jnp.where(hit, payload, -1), axis=0, keepdims=True))
        s = jnp.where(hit, -jnp.inf, s)
    out = [jnp.concatenate(vals, axis=0), jnp.concatenate(ids, axis=0)]
    if payload is not None:
        out.append(jnp.concatenate(pays, axis=0))
    return out


_PAIR_GROUPS = ((0, 0), (0, 8), (1, 0), (2, 0), (3, 0), (4, 0), (5, 0), (6, 0), (7, 0))


def _pair_candidates(sv0, si0, sv1, si1):
    sub = lax.broadcasted_iota(I32, (SUBLANES,) + sv0.shape[1:], 0)
    cand, flat, eid = [], [], []
    for a, b0 in _PAIR_GROUPS:
        cand.append(sv0[a:a + 1, :] + sv1[b0:b0 + SUBLANES, :])
        flat.append(a * PEER_TOPK + b0 + sub)
        eid.append(si0[a:a + 1, :] * PEER_NKEYS + si1[b0:b0 + SUBLANES, :])
    cand.append(sv0[SUBLANES:, :] + sv1[0:1, :])
    flat.append((SUBLANES + sub) * PEER_TOPK)
    eid.append(si0[SUBLANES:, :] * PEER_NKEYS + si1[0:1, :])
    return jnp.concatenate(cand, axis=0), jnp.concatenate(flat, axis=0), jnp.concatenate(eid, axis=0)


def _peer_route_kernel(x_ref, g_ref, wq_ref, keys_ref,
                       xn_ref, roff_ref, sh_ref, gate_ref, e_scr, g_scr):
    n = _rms(x_ref[...], g_ref[...])
    for s in range(ROW_TILES):
        xn_ref[:, s, :] = n[:, s * LANES:(s + 1) * LANES]
    q = jnp.dot(n.astype(BF16), wq_ref[...], preferred_element_type=F32)
    for hd in range(PEER_HEADS):
        qh = q[:, hd * PEER_DKEY:(hd + 1) * PEER_DKEY].astype(BF16)
        tops = []
        for p in range(2):
            s_t = lax.dot_general(keys_ref[hd, p], qh, (((1,), (1,)), ((), ())),
                                  preferred_element_type=F32)
            tops.append(_top_rows(s_t, PEER_TOPK))
        (sv0, si0), (sv1, si1) = tops
        cand, flat, ecand = _pair_candidates(sv0, si0, sv1, si1)
        cv, _, ce = _top_rows(cand, PEER_TOPK, rank=flat, payload=ecand)
        ex = jnp.exp(cv - cv[0:1, :])
        gate = ex / jnp.sum(ex, axis=0, keepdims=True)
        e_scr[hd * PEER_TOPK:(hd + 1) * PEER_TOPK, :] = ce
        g_scr[hd * PEER_TOPK:(hd + 1) * PEER_TOPK, :] = gate
    e_t = pltpu.bitcast(pltpu.bitcast(e_scr[...], F32).T, I32)
    hi = e_t >= HALF_EXPERTS
    roff_ref[...] = jnp.where(hi, e_t - HALF_EXPERTS, e_t) * ROW_TILES
    sh_ref[...] = jnp.where(hi, 0, 16)
    gate_ref[...] = g_scr[...].T


def _peer_route(h, g, wq, keys_pad):
    n_tok, d = h.shape
    tb = LANES
    row = pl.BlockSpec((tb, d), lambda i: (i, 0))
    slot = pl.BlockSpec((tb, PEER_SLOTS), lambda i: (i, 0))
    return pl.pallas_call(
        _peer_route_kernel,
        grid=(n_tok // tb,),
        in_specs=[
            row,
            pl.BlockSpec((1, d), lambda i: (0, 0)),
            pl.BlockSpec(wq.shape, lambda i: (0, 0)),
            pl.BlockSpec(keys_pad.shape, lambda i: (0, 0, 0, 0)),
        ],
        out_specs=[pl.BlockSpec((tb, ROW_TILES, LANES), lambda i: (i, 0, 0)), slot, slot, slot],
        out_shape=[
            jax.ShapeDtypeStruct((n_tok, ROW_TILES, LANES), F32),
            jax.ShapeDtypeStruct((n_tok, PEER_SLOTS), I32),
            jax.ShapeDtypeStruct((n_tok, PEER_SLOTS), I32),
            jax.ShapeDtypeStruct((n_tok, PEER_SLOTS), F32),
        ],
        scratch_shapes=[pltpu.VMEM((PEER_SLOTS, tb), I32), pltpu.VMEM((PEER_SLOTS, tb), F32)],
        compiler_params=_cparams(("parallel",), 48),
        name="peer_route",
    )(h, g.reshape(1, d), wq, keys_pad)


def _expert_tile(tab_ref, roff, shv):
    w = tab_ref[pl.ds(pl.multiple_of(roff, SUBLANES), SUBLANES), :]
    return pltpu.bitcast(w << shv, F32)


def _fold_rows(ps):
    sub = lax.broadcasted_iota(I32, (SUBLANES, LANES), 0)
    dist = SUBLANES // 2
    while len(ps) > 1:
        lo = (sub % (2 * dist)) < dist
        half = len(ps) // 2
        nxt = []
        for a in range(half):
            x, y = ps[a], ps[a + half]
            nxt.append(jnp.where(lo, x, pltpu.roll(y, dist, 0))
                       + jnp.where(lo, pltpu.roll(x, SUBLANES - dist, 0), y))
        ps = nxt
        dist //= 2
    return ps[0]


def _slot_columns(rows8, dst_ref):
    for u in range(SUBLANES):
        dst_ref[u] = jnp.broadcast_to(rows8[u:u + 1, :], (PEER_SLOTS, LANES)).T


def _bcast_row(ref, u, k):
    return jnp.broadcast_to(ref[u, k:k + 1, :], (SUBLANES, LANES))


def _peer_token_loop(tb, prep, token_group):
    n_pairs = tb // (2 * SUBLANES)
    prep(0, 0)

    def pair(i, carry):
        g0 = 2 * i
        prep(g0 + 1, 1)
        token_group(g0, 0)
        prep(jnp.minimum(g0 + 2, 2 * n_pairs - 1), 0)
        token_group(g0 + 1, 1)
        return carry

    lax.fori_loop(0, n_pairs, pair, 0)


def _peer_up_kernel(roff_ref, sh_ref, gate_ref, x_ref, tab_ref, c_ref, shs_a, shs_b):
    tb = sh_ref.shape[0]
    shs = (shs_a, shs_b)

    def prep(grp, par):
        r0 = pl.multiple_of(grp * SUBLANES, SUBLANES)
        _slot_columns(pltpu.bitcast(sh_ref[pl.ds(r0, SUBLANES), :], F32), shs[par])

    def token_group(grp, par):
        rows = []
        for u in range(SUBLANES):
            t = grp * SUBLANES + u
            x_t = x_ref[t]
            tiles = []
            for g in range(PEER_SLOTS // SUBLANES):
                ps = []
                for jj in range(SUBLANES):
                    k = g * SUBLANES + jj
                    shv = pltpu.bitcast(_bcast_row(shs[par], u, k), U32)
                    ps.append(_expert_tile(tab_ref, roff_ref[t, k], shv) * x_t)
                tiles.append(_fold_rows(ps))
            r = jnp.concatenate(tiles, axis=0)
            rows.append(jnp.sum(r.T, axis=0, keepdims=True))
        out = pl.ds(pl.multiple_of(grp * SUBLANES, SUBLANES), SUBLANES)
        c_ref[out, :] = gate_ref[out, :] * _gelu(jnp.concatenate(rows, axis=0))

    _peer_token_loop(tb, prep, token_group)


def _peer_down_kernel(roff_hbm, sh_ref, sh_next, c_ref, c_next, tab_ref, o_ref,
                      idx_a, idx_b, sh_a, sh_b, c_a, c_b, sem):
    i = pl.program_id(0)
    n = pl.num_programs(0)
    tb = sh_ref.shape[0]
    th = tb // 2
    idx = (idx_a, idx_b)

    def idx_copy(step, half):
        row0 = pl.multiple_of(step * tb + half * th, th)
        return pltpu.make_async_copy(roff_hbm.at[pl.ds(row0, th)], idx[half], sem.at[half])

    def build(src_sh, src_c, row, sh_t, c_t, u, after=None):
        sh_row = src_sh[row:row + 1, :]
        c_row = src_c[row:row + 1, :]
        if after is not None:
            zero = lax.shift_right_logical(pltpu.bitcast(after[0:1, :], U32), jnp.uint32(32))
            sh_row = sh_row + pltpu.bitcast(zero, I32)
            c_row = c_row + pltpu.bitcast(zero, F32)
        sh_t[u] = jnp.broadcast_to(pltpu.bitcast(sh_row, F32), (PEER_SLOTS, LANES)).T
        c_t[u] = jnp.broadcast_to(c_row, (PEER_SLOTS, LANES)).T

    def consume(half, sh_t, c_t, u):
        accs = [jnp.zeros((SUBLANES, LANES), F32) for _ in range(2)]
        for k in range(PEER_SLOTS):
            shv = pltpu.bitcast(_bcast_row(sh_t, u, k), U32)
            accs[k % 2] = accs[k % 2] + _expert_tile(tab_ref, idx[half][u, k], shv) * _bcast_row(c_t, u, k)
        acc = accs[0] + accs[1]
        o_ref[half * th + u] = acc
        return acc

    def half_step(half, use, fill, src_sh, src_c, row0):
        @pl.when(i >= 0)
        def _():
            acc = None
            for u in range(th):
                build(src_sh, src_c, row0 + u, fill[0], fill[1], u, after=acc)
                acc = consume(half, use[0], use[1], u)

    tiles = ((sh_a, c_a), (sh_b, c_b))

    @pl.when(i == 0)
    def _():
        idx_copy(0, 0).start()
        idx_copy(0, 1).start()
        for u in range(th):
            build(sh_ref, c_ref, u, sh_a, c_a, u)

    idx_copy(i, 0).wait()
    half_step(0, tiles[0], tiles[1], sh_ref, c_ref, th)

    @pl.when(i + 1 < n)
    def _():
        idx_copy(i + 1, 0).start()

    idx_copy(i, 1).wait()
    half_step(1, tiles[1], tiles[0], sh_next, c_next, 0)

    @pl.when(i + 1 < n)
    def _():
        idx_copy(i + 1, 1).start()


_PEER_TB = 64
_PEER_DOWN_TB = 64
_PEER_VMEM_MIB = 52
_COL_TILES = pltpu.VMEM((SUBLANES, PEER_SLOTS, LANES), F32)


def _table_spec(tab):
    return pl.BlockSpec(tab.shape, lambda i: (0, 0), pipeline_mode=pl.Buffered(1))


def _peer_up(roff, sh, gate, x_tiles, tab):
    n_tok = roff.shape[0]
    tb = _PEER_TB
    slot = pl.BlockSpec((tb, PEER_SLOTS), lambda i: (i, 0))
    return pl.pallas_call(
        _peer_up_kernel,
        grid=(n_tok // tb,),
        in_specs=[
            pl.BlockSpec((tb, PEER_SLOTS), lambda i: (i, 0), memory_space=pltpu.SMEM),
            slot, slot,
            pl.BlockSpec((tb, ROW_TILES, LANES), lambda i: (i, 0, 0)),
            _table_spec(tab),
        ],
        out_specs=slot,
        out_shape=jax.ShapeDtypeStruct((n_tok, PEER_SLOTS), F32),
        scratch_shapes=[_COL_TILES] * 2,
        compiler_params=_cparams(("arbitrary",), _PEER_VMEM_MIB),
        name="peer_up",
    )(roff, sh, gate, x_tiles, tab)


def _peer_down(roff, sh, coef, tab):
    n_tok = roff.shape[0]
    tb = _PEER_DOWN_TB
    th = tb // 2
    n_steps = n_tok // tb
    cur = pl.BlockSpec((tb, PEER_SLOTS), lambda i: (i, 0))
    nxt = pl.BlockSpec((tb, PEER_SLOTS), lambda i: (jnp.minimum(i + 1, n_steps - 1), 0))
    half_tiles = pltpu.VMEM((th, PEER_SLOTS, LANES), F32)
    return pl.pallas_call(
        _peer_down_kernel,
        grid=(n_steps,),
        in_specs=[pl.BlockSpec(memory_space=pl.ANY), cur, nxt, cur, nxt, _table_spec(tab)],
        out_specs=pl.BlockSpec((tb, ROW_TILES, LANES), lambda i: (i, 0, 0)),
        out_shape=jax.ShapeDtypeStruct((n_tok, ROW_TILES, LANES), F32),
        scratch_shapes=[pltpu.SMEM((th, PEER_SLOTS), I32)] * 2 + [half_tiles] * 4
        + [pltpu.SemaphoreType.DMA((2,))],
        compiler_params=_cparams(("arbitrary",), _PEER_VMEM_MIB),
        name="peer_down",
    )(roff, sh, sh, coef, coef, tab)


def _pack_kernel(lo_ref, hi_ref, o_ref):
    low = pltpu.bitcast(pltpu.bitcast(lo_ref[...].astype(BF16).astype(F32), U32) >> 16, I32)
    bits = pltpu.bitcast(hi_ref[...], I32)
    sign = bits & jnp.int32(-2 ** 31)
    mag = jnp.minimum(bits & jnp.int32(0x7FFFFFFF), jnp.int32(0x7F7F0000))
    top = jnp.maximum(mag + jnp.int32(0x8000) - low, 0) & jnp.int32(-65536)
    word = pltpu.bitcast(sign | top | low, U32)
    for s in range(ROW_TILES):
        o_ref[:, s, :] = word[:, s * LANES:(s + 1) * LANES]


def _pack_table(tabs, layer, rows=256):
    d = tabs.shape[2]
    n_blk = HALF_EXPERTS // rows
    packed = pl.pallas_call(
        _pack_kernel,
        grid=(n_blk,),
        in_specs=[pl.BlockSpec((None, rows, d), lambda i: (layer, i, 0)),
                  pl.BlockSpec((None, rows, d), lambda i: (layer, i + n_blk, 0))],
        out_specs=pl.BlockSpec((rows, ROW_TILES, LANES), lambda i: (i, 0, 0)),
        out_shape=jax.ShapeDtypeStruct((HALF_EXPERTS, ROW_TILES, LANES), U32),
        compiler_params=_cparams(("parallel",)),
        name="pack_table",
    )(tabs, tabs)
    return packed.reshape(HALF_EXPERTS * ROW_TILES, LANES)


def _peer(h, g, wq, keys_pad, u_tab, v_tab):
    xn, roff, sh, gate = _peer_route(h, g, wq, keys_pad)
    coef = _peer_up(roff, sh, gate, xn, u_tab)
    return _peer_down(roff, sh, coef, v_tab)


def _prep_layer(l, w_in, conv_w, a_log, dt_bias, o_norm_g, sg_ln_g, sg_ln_b, sg_w, sg_b, w_out,
                norm_mix_g, norm_mem_g, mem_norm_g, w_mq, w_mk, w_mv, w_mo, norm_ffn_g,
                peer_wq, peer_keys, peer_u, peer_v):
    o1 = QKV_DIM
    o2 = o1 + DN_V
    o4 = o2 + 2 * DN_HEADS
    o5 = o4 + SG_WIDTH
    wi = w_in[l]
    ba_cols = jnp.pad(wi[:, o2:o4], ((0, 0), (0, LANES - 2 * DN_HEADS)))
    w_in_r = jnp.concatenate([wi[:, :o2], wi[:, o4:], ba_cols], axis=1).astype(BF16)
    kz = jnp.zeros((PEER_HEADS, PEER_NKEYS, PEER_DKEY // 2), F32)
    keys_pad = jnp.stack([jnp.concatenate([peer_keys[l][:, 0], kz], axis=-1),
                          jnp.concatenate([kz, peer_keys[l][:, 1]], axis=-1)], axis=1).astype(BF16)
    return dict(
        w_in=w_in_r, conv_w=conv_w[l], a_log=a_log[l], dt_bias=dt_bias[l], o_norm_g=o_norm_g[l],
        sg_ln_g=sg_ln_g[l].reshape(-1), sg_ln_b=sg_ln_b[l].reshape(-1), sg_w=sg_w[l], sg_b=sg_b[l],
        w_out_a=w_out[l][:DN_V].astype(BF16), w_out_b=w_out[l][DN_V:].astype(BF16),
        norm_mix_g=norm_mix_g[l], norm_mem_g=norm_mem_g[l], mem_norm_g=mem_norm_g[l],
        w_mq=w_mq[l].astype(BF16), w_mkv=jnp.concatenate([w_mk[l], w_mv[l]], axis=1).astype(BF16),
        w_mo=w_mo[l].astype(BF16), norm_ffn_g=norm_ffn_g[l], peer_wq=peer_wq[l].astype(BF16),
        keys_pad=keys_pad, u_tab=_pack_table(peer_u, l), v_tab=_pack_table(peer_v, l),
    )


_IN_WIDTHS = (QKV_DIM, DN_V, SG_WIDTH, SG_WIDTH, LANES)


def _layer(h, delta, b, t, mk, mv, s0, c0, p, emit_vv, cache_layer=None):
    n_tok = b * t
    outs = _norm_proj(h, p["norm_mix_g"], p["w_in"], _IN_WIDTHS, delta=delta)
    if delta is not None:
        h, outs = outs[0], outs[1:]
    qkv, z, gu, gv, ba = outs
    o_a, s_new = _delta_mixer(qkv, z, ba, c0, s0, b, t, p["conv_w"], p["a_log"], p["dt_bias"], p["o_norm_g"])
    conv_new = qkv.reshape(b, t, QKV_DIM)[:, t - (DN_CONV - 1):, :]
    period = min(t, SG_CHUNK)
    reps = SG_CHUNK // period
    sg_w = jnp.tile(p["sg_w"][:, :period, :period], (1, reps, reps))
    sg_bias = jnp.broadcast_to(jnp.tile(p["sg_b"][:, :period], (1, reps))[:, :, None],
                               (SG_GROUPS, SG_CHUNK, SG_CH))
    sg_out = _sgu(gu, gv, sg_w, sg_bias, p["sg_ln_g"], p["sg_ln_b"], period, emit_vv)
    o_b = sg_out[0]
    vv = sg_out[1] if emit_vv else None
    h = _proj_residual(h, [o_a.reshape(n_tok, DN_V), o_b], [p["w_out_a"], p["w_out_b"]])
    (q,) = _norm_proj(h, p["norm_mem_g"], p["w_mq"], (D_MODEL,))
    att = _mem_attn(q.reshape(b, t, D_MODEL), mk, mv, layer=cache_layer)
    h = _proj_residual(h, [att.reshape(n_tok, D_MODEL)], [p["w_mo"]])
    return h, s_new, conv_new, vv


def kernel(x_prompt, x_sample, state_delta, state_conv, cache_mem_k, cache_mem_v, mem_prompt, w_in, conv_w, a_log, dt_bias, o_norm_g, sg_ln_g, sg_ln_b, sg_w, sg_b, w_out, norm_mix_g, norm_mem_g, mem_norm_g, w_mq, w_mk, w_mv, w_mo, norm_ffn_g, peer_wq, peer_keys, peer_u, peer_v, final_norm_g):
    depth = w_in.shape[0]
    bp, tp, d = x_prompt.shape
    bs, ts, _ = x_sample.shape
    hp = x_prompt.reshape(bp * tp, d)
    hs = x_sample.reshape(bs * ts, d)
    dp = ds = None
    mem_flat = mem_prompt.reshape(bp * MEM_LEN, d)
    sd_p, sc_p, mk_p, mv_p, sd_s, sc_s, vr_s = [], [], [], [], [], [], []
    for l in range(depth):
        p = _prep_layer(l, w_in, conv_w, a_log, dt_bias, o_norm_g, sg_ln_g, sg_ln_b, sg_w, sg_b, w_out,
                        norm_mix_g, norm_mem_g, mem_norm_g, w_mq, w_mk, w_mv, w_mo, norm_ffn_g,
                        peer_wq, peer_keys, peer_u, peer_v)
        mk, mv = _norm_proj(mem_flat, p["mem_norm_g"], p["w_mkv"], (D_MODEL, D_MODEL))
        mk = mk.reshape(bp, MEM_LEN, d)
        mv = mv.reshape(bp, MEM_LEN, d)
        s0 = jnp.zeros((bp, DN_HEADS, DN_DK, DN_DV), F32)
        c0 = jnp.zeros((bp, DN_CONV - 1, QKV_DIM), F32)
        hp, s_p, c_p, _ = _layer(hp, dp, bp, tp, mk, mv, s0, c0, p, False)
        hs, s_s, c_s, vv = _layer(hs, ds, bs, ts, cache_mem_k, cache_mem_v, state_delta[l], state_conv[l], p, True,
                                  cache_layer=l)
        dp = _peer(hp, p["norm_ffn_g"], p["peer_wq"], p["keys_pad"], p["u_tab"], p["v_tab"])
        ds = _peer(hs, p["norm_ffn_g"], p["peer_wq"], p["keys_pad"], p["u_tab"], p["v_tab"])
        sd_p.append(s_p)
        sc_p.append(c_p)
        mk_p.append(mk.reshape(bp, MEM_LEN, MEM_HEADS, MEM_HD))
        mv_p.append(mv.reshape(bp, MEM_LEN, MEM_HEADS, MEM_HD))
        sd_s.append(s_s)
        sc_s.append(c_s)
        vr_s.append(vv.reshape(bs, ts, SG_WIDTH))
    y_prompt = _add_norm(hp, dp, final_norm_g).reshape(bp, tp, d)
    y_sample = _add_norm(hs, ds, final_norm_g).reshape(bs, ts, d)
    return (y_prompt, y_sample, jnp.stack(sd_p), jnp.stack(sc_p), jnp.stack(mk_p), jnp.stack(mv_p),
            jnp.stack(sd_s), jnp.stack(sc_s), jnp.stack(vr_s))
```

```python
import functools
import math

import jax
import jax.numpy as jnp
from jax import lax
from jax.experimental import pallas as pl
from jax.experimental.pallas import tpu as pltpu

F32 = jnp.float32
BF16 = jnp.bfloat16
I32 = jnp.int32
U32 = jnp.uint32

D_MODEL = 1024
DN_HEADS = 4
DN_DK = 128
DN_DV = 128
DN_QK = DN_HEADS * DN_DK
DN_V = DN_HEADS * DN_DV
DN_CONV = 4
DN_CHUNK = 64
QKV_DIM = 2 * DN_QK + DN_V
SG_GROUPS = 4
SG_CH = 128
SG_WIDTH = SG_GROUPS * SG_CH
SG_CHUNK = 128
MEM_LEN = 256
MEM_HEADS = 4
MEM_HD = D_MODEL // MEM_HEADS
PEER_HEADS = 8
PEER_NKEYS = 128
PEER_NEXP = PEER_NKEYS * PEER_NKEYS
PEER_DKEY = 128
PEER_TOPK = 16
PEER_SLOTS = PEER_HEADS * PEER_TOPK
EPS = 1e-6

SUBLANES = 8
LANES = 128
ROW_TILES = D_MODEL // LANES
VMEM_MIB_V7X = 64
HALF_EXPERTS = PEER_NEXP // 2

_HI = lax.Precision.HIGHEST


def _cparams(sem, vmem_mib=None):
    kw = dict(dimension_semantics=sem)
    if vmem_mib is not None:
        kw["vmem_limit_bytes"] = vmem_mib * 1024 * 1024
    return pltpu.CompilerParams(**kw)


def _rms(x, g):
    return x * lax.rsqrt(jnp.mean(x * x, axis=-1, keepdims=True) + EPS) * g


def _gelu(x):
    return jax.nn.gelu(x, approximate=True)


def _sigmoid(x):
    return 1.0 / (1.0 + jnp.exp(-x))


def _silu(x):
    return x * _sigmoid(x)


def _softplus(x):
    return jnp.maximum(x, 0.0) + jnp.log(1.0 + jnp.exp(-jnp.abs(x)))


def _rows_from_tiles(t_ref):
    return jnp.concatenate([t_ref[:, s, :] for s in range(ROW_TILES)], axis=-1)


def _norm_proj_kernel(*refs, widths, has_delta, emit_h):
    it = iter(refs)
    x_ref = next(it)
    d_ref = next(it) if has_delta else None
    g_ref = next(it)
    w_ref = next(it)
    outs = list(it)
    x = x_ref[...]
    if has_delta:
        x = x + _rows_from_tiles(d_ref)
    if emit_h:
        outs[0][...] = x
        outs = outs[1:]
    n = _rms(x, g_ref[...]).astype(BF16)
    y = jnp.dot(n, w_ref[...], preferred_element_type=F32)
    c = 0
    for o, wd in zip(outs, widths):
        o[...] = y[:, c:c + wd]
        c += wd


def _norm_proj(x, g, w, widths, delta=None, tm=512):
    n_tok, d = x.shape
    tm = min(tm, n_tok)
    has_delta = delta is not None
    row = pl.BlockSpec((tm, d), lambda i: (i, 0))
    tiles = pl.BlockSpec((tm, ROW_TILES, LANES), lambda i: (i, 0, 0))
    in_specs = [row] + ([tiles] if has_delta else []) + [
        pl.BlockSpec((1, d), lambda i: (0, 0)),
        pl.BlockSpec(w.shape, lambda i: (0, 0)),
    ]
    out_shape, out_specs = [], []
    if has_delta:
        out_shape.append(jax.ShapeDtypeStruct((n_tok, d), F32))
        out_specs.append(row)
    for wd in widths:
        out_shape.append(jax.ShapeDtypeStruct((n_tok, wd), F32))
        out_specs.append(pl.BlockSpec((tm, wd), lambda i: (i, 0)))
    args = [x] + ([delta] if has_delta else []) + [g.reshape(1, d), w]
    return pl.pallas_call(
        functools.partial(_norm_proj_kernel, widths=tuple(widths), has_delta=has_delta, emit_h=has_delta),
        grid=(n_tok // tm,),
        in_specs=in_specs,
        out_specs=out_specs,
        out_shape=out_shape,
        compiler_params=_cparams(("parallel",), 48),
        name="norm_proj",
    )(*args)


def _proj_res_kernel(*refs, n_in):
    h_ref = refs[0]
    a_refs = refs[1:1 + n_in]
    w_refs = refs[1 + n_in:1 + 2 * n_in]
    o_ref = refs[1 + 2 * n_in]
    acc = h_ref[...]
    for a, w in zip(a_refs, w_refs):
        acc = acc + jnp.dot(a[...].astype(BF16), w[...], preferred_element_type=F32)
    o_ref[...] = acc


def _proj_residual(h, acts, ws, tm=512):
    n_tok, d = h.shape
    tm = min(tm, n_tok)
    in_specs = [pl.BlockSpec((tm, d), lambda i: (i, 0))]
    in_specs += [pl.BlockSpec((tm, a.shape[1]), lambda i: (i, 0)) for a in acts]
    in_specs += [pl.BlockSpec(w.shape, lambda i: (0, 0)) for w in ws]
    return pl.pallas_call(
        functools.partial(_proj_res_kernel, n_in=len(acts)),
        grid=(n_tok // tm,),
        in_specs=in_specs,
        out_specs=pl.BlockSpec((tm, d), lambda i: (i, 0)),
        out_shape=jax.ShapeDtypeStruct((n_tok, d), F32),
        compiler_params=_cparams(("parallel",), 48),
        name="proj_residual",
    )(h, *acts, *ws)


def _add_norm_kernel(x_ref, d_ref, g_ref, o_ref):
    o_ref[...] = _rms(x_ref[...] + _rows_from_tiles(d_ref), g_ref[...])


def _add_norm(x, delta, g, tm=512):
    n_tok, d = x.shape
    tm = min(tm, n_tok)
    row = pl.BlockSpec((tm, d), lambda i: (i, 0))
    tiles = pl.BlockSpec((tm, ROW_TILES, LANES), lambda i: (i, 0, 0))
    return pl.pallas_call(
        _add_norm_kernel,
        grid=(n_tok // tm,),
        in_specs=[row, tiles, pl.BlockSpec((1, d), lambda i: (0, 0))],
        out_specs=row,
        out_shape=jax.ShapeDtypeStruct((n_tok, d), F32),
        compiler_params=_cparams(("parallel",)),
        name="add_norm",
    )(x, delta, g.reshape(1, d))


def _bdot(a, b):
    return jnp.dot(a.astype(BF16), b.astype(BF16), preferred_element_type=F32)


def _bdot_nt(a, b):
    return lax.dot_general(a.astype(BF16), b.astype(BF16), (((1,), (1,)), ((), ())), preferred_element_type=F32)


def _bdot_tn(a, b):
    return lax.dot_general(a.astype(BF16), b.astype(BF16), (((0,), (0,)), ((), ())), preferred_element_type=F32)


def _delta_kernel(qkv_ref, z_ref, ba_ref, prev_ref, s0_ref, cw_ref, al_ref, dt_ref, og_ref,
                  o_ref, s_ref, *scratch, period, carry):
    rows = qkv_ref.shape[0]
    n_sub = rows // period
    x = qkv_ref[...]
    if carry:
        tail_ref = scratch[0]
        j = pl.program_id(1)

        @pl.when(j == 0)
        def _():
            s_ref[0] = s0_ref[0]
            tail_ref[...] = jnp.zeros_like(tail_ref)
            tail_ref[SUBLANES - (DN_CONV - 1):, :] = prev_ref[0]

        xx = jnp.concatenate([tail_ref[...], x], axis=0)
        acc = jnp.zeros((rows, QKV_DIM), F32)
        for jj in range(DN_CONV):
            off = SUBLANES - (DN_CONV - 1) + jj
            acc = acc + xx[off:off + rows, :] * cw_ref[jj:jj + 1, :]
        tail_ref[...] = x[rows - SUBLANES:, :]
    else:
        pos = lax.broadcasted_iota(I32, (rows, QKV_DIM), 0) % period
        hist = prev_ref[...]
        acc = x * cw_ref[DN_CONV - 1:DN_CONV, :]
        for d in range(1, DN_CONV):
            shifted = jnp.where(pos >= d, pltpu.roll(x, d, 0), pltpu.roll(hist, d, 0))
            acc = acc + shifted * cw_ref[DN_CONV - 1 - d:DN_CONV - d, :]
    y = _silu(acc)
    ba = ba_ref[...]
    lane = lax.broadcasted_iota(I32, ba.shape, 1)
    bg = jnp.where(lane < DN_HEADS, _sigmoid(ba), -jnp.exp(al_ref[...]) * _softplus(ba + dt_ref[...]))

    row = lax.broadcasted_iota(I32, (rows, rows), 0)
    col = lax.broadcasted_iota(I32, (rows, rows), 1)
    same = (row // period) == (col // period)
    causal = same & (row >= col)
    strict = same & (row > col)
    eye = (row == col).astype(F32)
    gcs = jnp.dot(causal.astype(F32), bg, precision=_HI, preferred_element_type=F32)
    n_fac = int(math.log2(period)) - 1

    heads = range(DN_HEADS)
    qs, ks, vs, betas, gcl, decays, kbs = [], [], [], [], [], [], []
    for hd in heads:
        q = y[:, hd * DN_DK:(hd + 1) * DN_DK]
        k = y[:, DN_QK + hd * DN_DK:DN_QK + (hd + 1) * DN_DK]
        qs.append(q * lax.rsqrt(jnp.sum(q * q, axis=-1, keepdims=True) + EPS) * (DN_DK ** -0.5))
        ks.append(k * lax.rsqrt(jnp.sum(k * k, axis=-1, keepdims=True) + EPS))
        vs.append(y[:, 2 * DN_QK + hd * DN_DV:2 * DN_QK + (hd + 1) * DN_DV])
        betas.append(bg[:, hd:hd + 1])
        gc = gcs[:, DN_HEADS + hd:DN_HEADS + hd + 1]
        gc_cols = jnp.broadcast_to(gc, (rows, rows))
        decays.append(jnp.exp(jnp.where(causal, gc_cols - gc_cols.T, -jnp.inf)))
        gcl.append(gc)
        kbs.append(ks[hd] * betas[hd])
    a_s = [jnp.where(strict, _bdot_nt(kbs[hd], ks[hd]) * decays[hd], 0.0) for hd in heads]
    tinv = [eye - a for a in a_s]
    pw = [_bdot(a, a) for a in a_s]
    for f in range(n_fac):
        tinv = [t + _bdot(t, p) for t, p in zip(tinv, pw)]
        if f + 1 < n_fac:
            pw = [_bdot(p, p) for p in pw]
    us = [_bdot(tinv[hd], vs[hd] * betas[hd]) for hd in heads]
    ws = [_bdot(tinv[hd], kbs[hd] * jnp.exp(gcl[hd])) for hd in heads]
    qks = [_bdot_nt(qs[hd], ks[hd]) * decays[hd] for hd in heads]
    qgs = [qs[hd] * jnp.exp(gcl[hd]) for hd in heads]
    v_new = [[] for _ in heads]
    o_state = [[] for _ in heads]
    for c in range(n_sub):
        sl = slice(c * period, (c + 1) * period)
        for hd in heads:
            s_old = s_ref[0, hd] if carry else s0_ref[c, hd]
            vn = us[hd][sl] - _bdot(ws[hd][sl], s_old)
            o_state[hd].append(_bdot(qgs[hd][sl], s_old))
            glast = gcl[hd][(c + 1) * period - 1:(c + 1) * period, :]
            s_new = s_old * jnp.exp(glast) + _bdot_tn(ks[hd][sl] * jnp.exp(glast - gcl[hd][sl]), vn)
            if carry:
                s_ref[0, hd] = s_new
            else:
                s_ref[c, hd] = s_new
            v_new[hd].append(vn)
    for hd in heads:
        o = jnp.concatenate(o_state[hd], axis=0) + _bdot(qks[hd], jnp.concatenate(v_new[hd], axis=0))
        o = _rms(o, og_ref[...])
        o_ref[:, hd * DN_DV:(hd + 1) * DN_DV] = o * _silu(z_ref[:, hd * DN_DV:(hd + 1) * DN_DV])


_DELTA_ROWS = 256
_DELTA_SEQS = 16


def _delta_mixer(qkv, z, ba, c0, s0, b, t, conv_w, a_log, dt_bias, o_norm_g):
    al = jnp.zeros((1, LANES), F32).at[0, DN_HEADS:2 * DN_HEADS].set(a_log)
    dt = jnp.zeros((1, LANES), F32).at[0, DN_HEADS:2 * DN_HEADS].set(dt_bias)
    carry = t >= DN_CHUNK
    if carry:
        period, rows, spb = DN_CHUNK, _DELTA_ROWS, 1
        nj = t // rows
        grid = (b, nj)
        tok = lambda i, j: (i * nj + j, 0)
        st = lambda i, j: (i, 0, 0, 0)
        const = lambda i, j: (0, 0)
        prev, prev_spec = c0, pl.BlockSpec((1, DN_CONV - 1, QKV_DIM), lambda i, j: (i, 0, 0))
        scratch = [pltpu.VMEM((SUBLANES, QKV_DIM), F32)]
        sem = ("parallel", "arbitrary")
    else:
        period, spb = t, _DELTA_SEQS
        rows = spb * period
        grid = (b // spb,)
        tok = lambda i: (i, 0)
        st = lambda i: (i, 0, 0, 0)
        const = lambda i: (0, 0)
        hist = jnp.zeros((b, period, QKV_DIM), F32).at[:, period - (DN_CONV - 1):].set(c0)
        prev = jnp.roll(hist.reshape(b // spb, rows, QKV_DIM), -period, axis=1).reshape(b * period, QKV_DIM)
        prev_spec = pl.BlockSpec((rows, QKV_DIM), tok)
        scratch = []
        sem = ("parallel",)
    state = pl.BlockSpec((spb, DN_HEADS, DN_DK, DN_DV), st)
    return pl.pallas_call(
        functools.partial(_delta_kernel, period=period, carry=carry),
        grid=grid,
        in_specs=[
            pl.BlockSpec((rows, QKV_DIM), tok),
            pl.BlockSpec((rows, DN_V), tok),
            pl.BlockSpec((rows, LANES), tok),
            prev_spec,
            state,
            pl.BlockSpec((DN_CONV, QKV_DIM), const),
            pl.BlockSpec((1, LANES), const),
            pl.BlockSpec((1, LANES), const),
            pl.BlockSpec((1, DN_DV), const),
        ],
        out_specs=[pl.BlockSpec((rows, DN_V), tok), state],
        out_shape=[
            jax.ShapeDtypeStruct((b * t, DN_V), F32),
            jax.ShapeDtypeStruct((b, DN_HEADS, DN_DK, DN_DV), F32),
        ],
        scratch_shapes=scratch,
        compiler_params=_cparams(sem, 48),
        name="delta_mixer",
    )(qkv, z, ba, prev, s0, conv_w, al, dt, o_norm_g.reshape(1, DN_DV))


def _sgu_kernel(gu_ref, gv_ref, w_ref, b_ref, lg_ref, lb_ref, o_ref, *rest, period):
    vv_ref = rest[0] if rest else None
    n = gu_ref.shape[0]
    row = lax.broadcasted_iota(I32, (n, n), 0)
    col = lax.broadcasted_iota(I32, (n, n), 1)
    keep = row >= col
    if period < n:
        keep = keep & ((row // period) == (col // period))
    for g in range(SG_GROUPS):
        sl = slice(g * SG_CH, (g + 1) * SG_CH)
        u = _gelu(gu_ref[:, sl])
        x = _gelu(gv_ref[:, sl])
        mu = jnp.mean(x, axis=-1, keepdims=True)
        xc = x - mu
        var = jnp.mean(xc * xc, axis=-1, keepdims=True)
        vv = xc * lax.rsqrt(var + EPS) * lg_ref[:, sl] + lb_ref[:, sl]
        if vv_ref is not None:
            vv_ref[:, sl] = vv
        wm = jnp.where(keep, w_ref[g], 0.0).astype(BF16)
        mix = jnp.dot(wm, vv.astype(BF16), preferred_element_type=F32) + b_ref[g]
        o_ref[:, sl] = u * mix


def _sgu(gu, gv, w, bias, ln_g, ln_b, period, emit_vv):
    n_tok = gu.shape[0]
    blk = pl.BlockSpec((SG_CHUNK, SG_WIDTH), lambda i: (i, 0))
    out_shape = [jax.ShapeDtypeStruct((n_tok, SG_WIDTH), F32)]
    out_specs = [blk]
    if emit_vv:
        out_shape.append(jax.ShapeDtypeStruct((n_tok, SG_WIDTH), F32))
        out_specs.append(blk)
    return pl.pallas_call(
        functools.partial(_sgu_kernel, period=period),
        grid=(n_tok // SG_CHUNK,),
        in_specs=[
            blk, blk,
            pl.BlockSpec((SG_GROUPS, SG_CHUNK, SG_CHUNK), lambda i: (0, 0, 0)),
            pl.BlockSpec((SG_GROUPS, SG_CHUNK, SG_CH), lambda i: (0, 0, 0)),
            pl.BlockSpec((1, SG_WIDTH), lambda i: (0, 0)),
            pl.BlockSpec((1, SG_WIDTH), lambda i: (0, 0)),
        ],
        out_specs=out_specs,
        out_shape=out_shape,
        compiler_params=_cparams(("parallel",)),
        name="sgu",
    )(gu, gv, w, bias, ln_g.reshape(1, SG_WIDTH), ln_b.reshape(1, SG_WIDTH))


def _mem_attn_kernel(q_ref, k_ref, v_ref, o_ref):
    for hd in range(MEM_HEADS):
        sl = slice(hd * MEM_HD, (hd + 1) * MEM_HD)
        q = q_ref[0, :, sl].astype(BF16)
        k = k_ref[0, :, sl].astype(BF16)
        v = v_ref[0, :, sl].astype(BF16)
        s = lax.dot_general(q, k, (((1,), (1,)), ((), ())), preferred_element_type=F32) * (MEM_HD ** -0.5)
        m = jnp.max(s, axis=-1, keepdims=True)
        e = jnp.exp(s - m)
        p = e / jnp.sum(e, axis=-1, keepdims=True)
        o_ref[0, :, sl] = jnp.dot(p.astype(BF16), v, preferred_element_type=F32)


def _mem_attn_cache_kernel(q_ref, k_ref, v_ref, o_ref):
    t = q_ref.shape[1]
    q_all = jnp.concatenate([q_ref[0, :, hd * MEM_HD:(hd + 1) * MEM_HD] for hd in range(MEM_HEADS)], axis=0)
    k2 = k_ref[...].reshape(MEM_LEN * MEM_HEADS, MEM_HD).astype(BF16)
    v2 = v_ref[...].reshape(MEM_LEN * MEM_HEADS, MEM_HD).astype(BF16)
    s = lax.dot_general(q_all.astype(BF16), k2, (((1,), (1,)), ((), ())),
                        preferred_element_type=F32) * (MEM_HD ** -0.5)
    q_head = lax.broadcasted_iota(I32, s.shape, 0) // t
    k_head = lax.broadcasted_iota(I32, s.shape, 1) % MEM_HEADS
    s = jnp.where(q_head == k_head, s, -jnp.inf)
    m = jnp.max(s, axis=-1, keepdims=True)
    e = jnp.exp(s - m)
    p = e / jnp.sum(e, axis=-1, keepdims=True)
    out = jnp.dot(p.astype(BF16), v2, preferred_element_type=F32)
    for hd in range(MEM_HEADS):
        o_ref[0, :, hd * MEM_HD:(hd + 1) * MEM_HD] = out[hd * t:(hd + 1) * t, :]


def _mem_attn(q, mk, mv, layer=None):
    b, t, d = q.shape
    tq = min(t, 512)
    if layer is None:
        kv = pl.BlockSpec((1, MEM_LEN, d), lambda i, j: (i, 0, 0))
        body = _mem_attn_kernel
    else:
        kv = pl.BlockSpec((None, None, MEM_LEN, MEM_HEADS, MEM_HD), lambda i, j: (layer, i, 0, 0, 0))
        body = _mem_attn_cache_kernel
    return pl.pallas_call(
        body,
        grid=(b, t // tq),
        in_specs=[pl.BlockSpec((1, tq, d), lambda i, j: (i, j, 0)), kv, kv],
        out_specs=pl.BlockSpec((1, tq, d), lambda i, j: (i, j, 0)),
        out_shape=jax.ShapeDtypeStruct((b, t, d), F32),
        compiler_params=_cparams(("parallel", "parallel")),
        name="mem_attn",
    )(q, mk, mv)


def _top_rows(s, n_top, rank=None, payload=None):
    if rank is None:
        rank = lax.broadcasted_iota(I32, s.shape, 0)
    big = jnp.int32(2 ** 30)
    vals, ids, pays = [], [], []
    for _ in range(n_top):
        m = jnp.max(s, axis=0, keepdims=True)
        i = jnp.min(jnp.where(s == m, rank, big), axis=0, keepdims=True)
        hit = rank == i
        vals.append(m)
        ids.append(i)
        if payload is not None:
            pays.append(jnp.max(jnp.where(hit, payload, -1), axis=0, keepdims=True))
        s = jnp.where(hit, -jnp.inf, s)
    out = [jnp.concatenate(vals, axis=0), jnp.concatenate(ids, axis=0)]
    if payload is not None:
        out.append(jnp.concatenate(pays, axis=0))
    return out


_PAIR_GROUPS = ((0, 0), (0, 8), (1, 0), (2, 0), (3, 0), (4, 0), (5, 0), (6, 0), (7, 0))


def _pair_candidates(sv0, si0, sv1, si1):
    sub = lax.broadcasted_iota(I32, (SUBLANES,) + sv0.shape[1:], 0)
    cand, flat, eid = [], [], []
    for a, b0 in _PAIR_GROUPS:
        cand.append(sv0[a:a + 1, :] + sv1[b0:b0 + SUBLANES, :])
        flat.append(a * PEER_TOPK + b0 + sub)
        eid.append(si0[a:a + 1, :] * PEER_NKEYS + si1[b0:b0 + SUBLANES, :])
    cand.append(sv0[SUBLANES:, :] + sv1[0:1, :])
    flat.append((SUBLANES + sub) * PEER_TOPK)
    eid.append(si0[SUBLANES:, :] * PEER_NKEYS + si1[0:1, :])
    return jnp.concatenate(cand, axis=0), jnp.concatenate(flat, axis=0), jnp.concatenate(eid, axis=0)


def _peer_route_kernel(x_ref, g_ref, wq_ref, keys_ref,
                       xn_ref, roff_ref, sh_ref, gate_ref, e_scr, g_scr):
    n = _rms(x_ref[...], g_ref[...])
    for s in range(ROW_TILES):
        xn_ref[:, s, :] = n[:, s * LANES:(s + 1) * LANES]
    q = jnp.dot(n.astype(BF16), wq_ref[...], preferred_element_type=F32)
    for hd in range(PEER_HEADS):
        qh = q[:, hd * PEER_DKEY:(hd + 1) * PEER_DKEY].astype(BF16)
        tops = []
        for p in range(2):
            s_t = lax.dot_general(keys_ref[hd, p], qh, (((1,), (1,)), ((), ())),
                                  preferred_element_type=F32)
            tops.append(_top_rows(s_t, PEER_TOPK))
        (sv0, si0), (sv1, si1) = tops
        cand, flat, ecand = _pair_candidates(sv0, si0, sv1, si1)
        cv, _, ce = _top_rows(cand, PEER_TOPK, rank=flat, payload=ecand)
        ex = jnp.exp(cv - cv[0:1, :])
        gate = ex / jnp.sum(ex, axis=0, keepdims=True)
        e_scr[hd * PEER_TOPK:(hd + 1) * PEER_TOPK, :] = ce
        g_scr[hd * PEER_TOPK:(hd + 1) * PEER_TOPK, :] = gate
    e_t = pltpu.bitcast(pltpu.bitcast(e_scr[...], F32).T, I32)
    hi = e_t >= HALF_EXPERTS
    roff_ref[...] = jnp.where(hi, e_t - HALF_EXPERTS, e_t) * ROW_TILES
    sh_ref[...] = jnp.where(hi, 0, 16)
    gate_ref[...] = g_scr[...].T


def _peer_route(h, g, wq, keys_pad):
    n_tok, d = h.shape
    tb = LANES
    row = pl.BlockSpec((tb, d), lambda i: (i, 0))
    slot = pl.BlockSpec((tb, PEER_SLOTS), lambda i: (i, 0))
    return pl.pallas_call(
        _peer_route_kernel,
        grid=(n_tok // tb,),
        in_specs=[
            row,
            pl.BlockSpec((1, d), lambda i: (0, 0)),
            pl.BlockSpec(wq.shape, lambda i: (0, 0)),
            pl.BlockSpec(keys_pad.shape, lambda i: (0, 0, 0, 0)),
        ],
        out_specs=[pl.BlockSpec((tb, ROW_TILES, LANES), lambda i: (i, 0, 0)), slot, slot, slot],
        out_shape=[
            jax.ShapeDtypeStruct((n_tok, ROW_TILES, LANES), F32),
            jax.ShapeDtypeStruct((n_tok, PEER_SLOTS), I32),
            jax.ShapeDtypeStruct((n_tok, PEER_SLOTS), I32),
            jax.ShapeDtypeStruct((n_tok, PEER_SLOTS), F32),
        ],
        scratch_shapes=[pltpu.VMEM((PEER_SLOTS, tb), I32), pltpu.VMEM((PEER_SLOTS, tb), F32)],
        compiler_params=_cparams(("parallel",), 48),
        name="peer_route",
    )(h, g.reshape(1, d), wq, keys_pad)


def _expert_tile(tab_ref, roff, shv):
    w = tab_ref[pl.ds(pl.multiple_of(roff, SUBLANES), SUBLANES), :]
    return pltpu.bitcast(w << shv, F32)


def _fold_rows(ps):
    sub = lax.broadcasted_iota(I32, (SUBLANES, LANES), 0)
    dist = SUBLANES // 2
    while len(ps) > 1:
        lo = (sub % (2 * dist)) < dist
        half = len(ps) // 2
        nxt = []
        for a in range(half):
            x, y = ps[a], ps[a + half]
            nxt.append(jnp.where(lo, x, pltpu.roll(y, dist, 0))
                       + jnp.where(lo, pltpu.roll(x, SUBLANES - dist, 0), y))
        ps = nxt
        dist //= 2
    return ps[0]


def _slot_columns(rows8, dst_ref):
    for u in range(SUBLANES):
        dst_ref[u] = jnp.broadcast_to(rows8[u:u + 1, :], (PEER_SLOTS, LANES)).T


def _bcast_row(ref, u, k):
    return jnp.broadcast_to(ref[u, k:k + 1, :], (SUBLANES, LANES))


def _peer_token_loop(tb, prep, token_group):
    n_pairs = tb // (2 * SUBLANES)
    prep(0, 0)

    def pair(i, carry):
        g0 = 2 * i
        prep(g0 + 1, 1)
        token_group(g0, 0)
        prep(jnp.minimum(g0 + 2, 2 * n_pairs - 1), 0)
        token_group(g0 + 1, 1)
        return carry

    lax.fori_loop(0, n_pairs, pair, 0)


def _peer_up_kernel(roff_ref, sh_ref, gate_ref, x_ref, tab_ref, c_ref, shs_a, shs_b):
    tb = sh_ref.shape[0]
    shs = (shs_a, shs_b)

    def prep(grp, par):
        r0 = pl.multiple_of(grp * SUBLANES, SUBLANES)
        _slot_columns(pltpu.bitcast(sh_ref[pl.ds(r0, SUBLANES), :], F32), shs[par])

    def token_group(grp, par):
        rows = []
        for u in range(SUBLANES):
            t = grp * SUBLANES + u
            x_t = x_ref[t]
            tiles = []
            for g in range(PEER_SLOTS // SUBLANES):
                ps = []
                for jj in range(SUBLANES):
                    k = g * SUBLANES + jj
                    shv = pltpu.bitcast(_bcast_row(shs[par], u, k), U32)
                    ps.append(_expert_tile(tab_ref, roff_ref[t, k], shv) * x_t)
                tiles.append(_fold_rows(ps))
            r = jnp.concatenate(tiles, axis=0)
            rows.append(jnp.sum(r.T, axis=0, keepdims=True))
        out = pl.ds(pl.multiple_of(grp * SUBLANES, SUBLANES), SUBLANES)
        c_ref[out, :] = gate_ref[out, :] * _gelu(jnp.concatenate(rows, axis=0))

    _peer_token_loop(tb, prep, token_group)


def _peer_down_kernel(roff_hbm, sh_ref, sh_next, c_ref, c_next, tab_ref, o_ref,
                      idx_a, idx_b, sh_a, sh_b, c_a, c_b, sem):
    i = pl.program_id(0)
    n = pl.num_programs(0)
    tb = sh_ref.shape[0]
    th = tb // 2
    idx = (idx_a, idx_b)

    def idx_copy(step, half):
        row0 = pl.multiple_of(step * tb + half * th, th)
        return pltpu.make_async_copy(roff_hbm.at[pl.ds(row0, th)], idx[half], sem.at[half])

    def build(src_sh, src_c, row, sh_t, c_t, u, after=None):
        sh_row = src_sh[row:row + 1, :]
        c_row = src_c[row:row + 1, :]
        if after is not None:
            zero = lax.shift_right_logical(pltpu.bitcast(after[0:1, :], U32), jnp.uint32(32))
            sh_row = sh_row + pltpu.bitcast(zero, I32)
            c_row = c_row + pltpu.bitcast(zero, F32)
        sh_t[u] = jnp.broadcast_to(pltpu.bitcast(sh_row, F32), (PEER_SLOTS, LANES)).T
        c_t[u] = jnp.broadcast_to(c_row, (PEER_SLOTS, LANES)).T

    def consume(half, sh_t, c_t, u):
        accs = [jnp.zeros((SUBLANES, LANES), F32) for _ in range(2)]
        for k in range(PEER_SLOTS):
            shv = pltpu.bitcast(_bcast_row(sh_t, u, k), U32)
            accs[k % 2] = accs[k % 2] + _expert_tile(tab_ref, idx[half][u, k], shv) * _bcast_row(c_t, u, k)
        acc = accs[0] + accs[1]
        o_ref[half * th + u] = acc
        return acc

    def half_step(half, use, fill, src_sh, src_c, row0):
        @pl.when(i >= 0)
        def _():
            acc = None
            for u in range(th):
                build(src_sh, src_c, row0 + u, fill[0], fill[1], u, after=acc)
                acc = consume(half, use[0], use[1], u)

    tiles = ((sh_a, c_a), (sh_b, c_b))

    @pl.when(i == 0)
    def _():
        idx_copy(0, 0).start()
        idx_copy(0, 1).start()
        for u in range(th):
            build(sh_ref, c_ref, u, sh_a, c_a, u)

    idx_copy(i, 0).wait()
    half_step(0, tiles[0], tiles[1], sh_ref, c_ref, th)

    @pl.when(i + 1 < n)
    def _():
        idx_copy(i + 1, 0).start()

    idx_copy(i, 1).wait()
    half_step(1, tiles[1], tiles[0], sh_next, c_next, 0)

    @pl.when(i + 1 < n)
    def _():
        idx_copy(i + 1, 1).start()


_PEER_TB = 64
_PEER_DOWN_TB = 16
_PEER_TABLE_MIB = HALF_EXPERTS * D_MODEL * 4 // 2 ** 20
_PEER_VMEM_MIB = min(_PEER_TABLE_MIB + 20, VMEM_MIB_V7X - 8)
_COL_TILES = pltpu.VMEM((SUBLANES, PEER_SLOTS, LANES), F32)


def _table_spec(tab):
    return pl.BlockSpec(tab.shape, lambda i: (0, 0), pipeline_mode=pl.Buffered(1))


def _peer_up(roff, sh, gate, x_tiles, tab):
    n_tok = roff.shape[0]
    tb = _PEER_TB
    slot = pl.BlockSpec((tb, PEER_SLOTS), lambda i: (i, 0))
    return pl.pallas_call(
        _peer_up_kernel,
        grid=(n_tok // tb,),
        in_specs=[
            pl.BlockSpec((tb, PEER_SLOTS), lambda i: (i, 0), memory_space=pltpu.SMEM),
            slot, slot,
            pl.BlockSpec((tb, ROW_TILES, LANES), lambda i: (i, 0, 0)),
            _table_spec(tab),
        ],
        out_specs=slot,
        out_shape=jax.ShapeDtypeStruct((n_tok, PEER_SLOTS), F32),
        scratch_shapes=[_COL_TILES] * 2,
        compiler_params=_cparams(("arbitrary",), _PEER_VMEM_MIB),
        name="peer_up",
    )(roff, sh, gate, x_tiles, tab)


def _peer_down(roff, sh, coef, tab):
    n_tok = roff.shape[0]
    tb = _PEER_DOWN_TB
    th = tb // 2
    n_steps = n_tok // tb
    cur = pl.BlockSpec((tb, PEER_SLOTS), lambda i: (i, 0))
    nxt = pl.BlockSpec((tb, PEER_SLOTS), lambda i: (jnp.minimum(i + 1, n_steps - 1), 0))
    half_tiles = pltpu.VMEM((th, PEER_SLOTS, LANES), F32)
    return pl.pallas_call(
        _peer_down_kernel,
        grid=(n_steps,),
        in_specs=[pl.BlockSpec(memory_space=pl.ANY), cur, nxt, cur, nxt, _table_spec(tab)],
        out_specs=pl.BlockSpec((tb, ROW_TILES, LANES), lambda i: (i, 0, 0)),
        out_shape=jax.ShapeDtypeStruct((n_tok, ROW_TILES, LANES), F32),
        scratch_shapes=[pltpu.SMEM((th, PEER_SLOTS), I32)] * 2 + [half_tiles] * 4
        + [pltpu.SemaphoreType.DMA((2,))],
        compiler_params=_cparams(("arbitrary",), _PEER_VMEM_MIB),
        name="peer_down",
    )(roff, sh, sh, coef, coef, tab)


def _pack_kernel(lo_ref, hi_ref, o_ref):
    low = pltpu.bitcast(pltpu.bitcast(lo_ref[...].astype(BF16).astype(F32), U32) >> 16, I32)
    bits = pltpu.bitcast(hi_ref[...], I32)
    sign = bits & jnp.int32(-2 ** 31)
    mag = jnp.minimum(bits & jnp.int32(0x7FFFFFFF), jnp.int32(0x7F7F0000))
    top = jnp.maximum(mag + jnp.int32(0x8000) - low, 0) & jnp.int32(-65536)
    word = pltpu.bitcast(sign | top | low, U32)
    for s in range(ROW_TILES):
        o_ref[:, s, :] = word[:, s * LANES:(s + 1) * LANES]


def _pack_table(tabs, layer, rows=256):
    d = tabs.shape[2]
    n_blk = HALF_EXPERTS // rows
    packed = pl.pallas_call(
        _pack_kernel,
        grid=(n_blk,),
        in_specs=[pl.BlockSpec((None, rows, d), lambda i: (layer, i, 0)),
                  pl.BlockSpec((None, rows, d), lambda i: (layer, i + n_blk, 0))],
        out_specs=pl.BlockSpec((rows, ROW_TILES, LANES), lambda i: (i, 0, 0)),
        out_shape=jax.ShapeDtypeStruct((HALF_EXPERTS, ROW_TILES, LANES), U32),
        compiler_params=_cparams(("parallel",)),
        name="pack_table",
    )(tabs, tabs)
    return packed.reshape(HALF_EXPERTS * ROW_TILES, LANES)


def _peer(h, g, wq, keys_pad, u_tab, v_tab):
    xn, roff, sh, gate = _peer_route(h, g, wq, keys_pad)
    coef = _peer_up(roff, sh, gate, xn, u_tab)
    return _peer_down(roff, sh, coef, v_tab)


def _prep_layer(l, w_in, conv_w, a_log, dt_bias, o_norm_g, sg_ln_g, sg_ln_b, sg_w, sg_b, w_out,
                norm_mix_g, norm_mem_g, mem_norm_g, w_mq, w_mk, w_mv, w_mo, norm_ffn_g,
                peer_wq, peer_keys, peer_u, peer_v):
    o1 = QKV_DIM
    o2 = o1 + DN_V
    o4 = o2 + 2 * DN_HEADS
    o5 = o4 + SG_WIDTH
    wi = w_in[l]
    ba_cols = jnp.pad(wi[:, o2:o4], ((0, 0), (0, LANES - 2 * DN_HEADS)))
    w_in_r = jnp.concatenate([wi[:, :o2], wi[:, o4:], ba_cols], axis=1).astype(BF16)
    kz = jnp.zeros((PEER_HEADS, PEER_NKEYS, PEER_DKEY // 2), F32)
    keys_pad = jnp.stack([jnp.concatenate([peer_keys[l][:, 0], kz], axis=-1),
                          jnp.concatenate([kz, peer_keys[l][:, 1]], axis=-1)], axis=1).astype(BF16)
    return dict(
        w_in=w_in_r, conv_w=conv_w[l], a_log=a_log[l], dt_bias=dt_bias[l], o_norm_g=o_norm_g[l],
        sg_ln_g=sg_ln_g[l].reshape(-1), sg_ln_b=sg_ln_b[l].reshape(-1), sg_w=sg_w[l], sg_b=sg_b[l],
        w_out_a=w_out[l][:DN_V].astype(BF16), w_out_b=w_out[l][DN_V:].astype(BF16),
        norm_mix_g=norm_mix_g[l], norm_mem_g=norm_mem_g[l], mem_norm_g=mem_norm_g[l],
        w_mq=w_mq[l].astype(BF16), w_mkv=jnp.concatenate([w_mk[l], w_mv[l]], axis=1).astype(BF16),
        w_mo=w_mo[l].astype(BF16), norm_ffn_g=norm_ffn_g[l], peer_wq=peer_wq[l].astype(BF16),
        keys_pad=keys_pad, u_tab=_pack_table(peer_u, l), v_tab=_pack_table(peer_v, l),
    )


_IN_WIDTHS = (QKV_DIM, DN_V, SG_WIDTH, SG_WIDTH, LANES)


def _layer(h, delta, b, t, mk, mv, s0, c0, p, emit_vv, cache_layer=None):
    n_tok = b * t
    outs = _norm_proj(h, p["norm_mix_g"], p["w_in"], _IN_WIDTHS, delta=delta)
    if delta is not None:
        h, outs = outs[0], outs[1:]
    qkv, z, gu, gv, ba = outs
    o_a, s_new = _delta_mixer(qkv, z, ba, c0, s0, b, t, p["conv_w"], p["a_log"], p["dt_bias"], p["o_norm_g"])
    conv_new = qkv.reshape(b, t, QKV_DIM)[:, t - (DN_CONV - 1):, :]
    period = min(t, SG_CHUNK)
    reps = SG_CHUNK // period
    sg_w = jnp.tile(p["sg_w"][:, :period, :period], (1, reps, reps))
    sg_bias = jnp.broadcast_to(jnp.tile(p["sg_b"][:, :period], (1, reps))[:, :, None],
                               (SG_GROUPS, SG_CHUNK, SG_CH))
    sg_out = _sgu(gu, gv, sg_w, sg_bias, p["sg_ln_g"], p["sg_ln_b"], period, emit_vv)
    o_b = sg_out[0]
    vv = sg_out[1] if emit_vv else None
    h = _proj_residual(h, [o_a.reshape(n_tok, DN_V), o_b], [p["w_out_a"], p["w_out_b"]])
    (q,) = _norm_proj(h, p["norm_mem_g"], p["w_mq"], (D_MODEL,))
    att = _mem_attn(q.reshape(b, t, D_MODEL), mk, mv, layer=cache_layer)
    h = _proj_residual(h, [att.reshape(n_tok, D_MODEL)], [p["w_mo"]])
    return h, s_new, conv_new, vv


def kernel(x_prompt, x_sample, state_delta, state_conv, cache_mem_k, cache_mem_v, mem_prompt, w_in, conv_w, a_log, dt_bias, o_norm_g, sg_ln_g, sg_ln_b, sg_w, sg_b, w_out, norm_mix_g, norm_mem_g, mem_norm_g, w_mq, w_mk, w_mv, w_mo, norm_ffn_g, peer_wq, peer_keys, peer_u, peer_v, final_norm_g):
    depth = w_in.shape[0]
    bp, tp, d = x_prompt.shape
    bs, ts, _ = x_sample.shape
    hp = x_prompt.reshape(bp * tp, d)
    hs = x_sample.reshape(bs * ts, d)
    dp = ds = None
    mem_flat = mem_prompt.reshape(bp * MEM_LEN, d)
    sd_p, sc_p, mk_p, mv_p, sd_s, sc_s, vr_s = [], [], [], [], [], [], []
    for l in range(depth):
        p = _prep_layer(l, w_in, conv_w, a_log, dt_bias, o_norm_g, sg_ln_g, sg_ln_b, sg_w, sg_b, w_out,
                        norm_mix_g, norm_mem_g, mem_norm_g, w_mq, w_mk, w_mv, w_mo, norm_ffn_g,
                        peer_wq, peer_keys, peer_u, peer_v)
        mk, mv = _norm_proj(mem_flat, p["mem_norm_g"], p["w_mkv"], (D_MODEL, D_MODEL))
        mk = mk.reshape(bp, MEM_LEN, d)
        mv = mv.reshape(bp, MEM_LEN, d)
        s0 = jnp.zeros((bp, DN_HEADS, DN_DK, DN_DV), F32)
        c0 = jnp.zeros((bp, DN_CONV - 1, QKV_DIM), F32)
        hp, s_p, c_p, _ = _layer(hp, dp, bp, tp, mk, mv, s0, c0, p, False)
        hs, s_s, c_s, vv = _layer(hs, ds, bs, ts, cache_mem_k, cache_mem_v, state_delta[l], state_conv[l], p, True,
                                  cache_layer=l)
        dp = _peer(hp, p["norm_ffn_g"], p["peer_wq"], p["keys_pad"], p["u_tab"], p["v_tab"])
        ds = _peer(hs, p["norm_ffn_g"], p["peer_wq"], p["keys_pad"], p["u_tab"], p["v_tab"])
        sd_p.append(s_p)
        sc_p.append(c_p)
        mk_p.append(mk.reshape(bp, MEM_LEN, MEM_HEADS, MEM_HD))
        mv_p.append(mv.reshape(bp, MEM_LEN, MEM_HEADS, MEM_HD))
        sd_s.append(s_s)
        sc_s.append(c_s)
        vr_s.append(vv.reshape(bs, ts, SG_WIDTH))
    y_prompt = _add_norm(hp, dp, final_norm_g).reshape(bp, tp, d)
    y_sample = _add_norm(hs, ds, final_norm_g).reshape(bs, ts, d)
    return (y_prompt, y_sample, jnp.stack(sd_p), jnp.stack(sc_p), jnp.stack(mk_p), jnp.stack(mv_p),
            jnp.stack(sd_s), jnp.stack(sc_s), jnp.stack(vr_s))
```

```python
import functools
import math

import jax
import jax.numpy as jnp
from jax import lax
from jax.experimental import pallas as pl
from jax.experimental.pallas import tpu as pltpu

F32 = jnp.float32
BF16 = jnp.bfloat16
I32 = jnp.int32
U32 = jnp.uint32

D_MODEL = 1024
DN_HEADS = 4
DN_DK = 128
DN_DV = 128
DN_QK = DN_HEADS * DN_DK
DN_V = DN_HEADS * DN_DV
DN_CONV = 4
DN_CHUNK = 64
QKV_DIM = 2 * DN_QK + DN_V
SG_GROUPS = 4
SG_CH = 128
SG_WIDTH = SG_GROUPS * SG_CH
SG_CHUNK = 128
MEM_LEN = 256
MEM_HEADS = 4
MEM_HD = D_MODEL // MEM_HEADS
PEER_HEADS = 8
PEER_NKEYS = 128
PEER_NEXP = PEER_NKEYS * PEER_NKEYS
PEER_DKEY = 128
PEER_TOPK = 16
PEER_SLOTS = PEER_HEADS * PEER_TOPK
EPS = 1e-6

SUBLANES = 8
LANES = 128
ROW_TILES = D_MODEL // LANES
VMEM_MIB_V7X = 64
HALF_EXPERTS = PEER_NEXP // 2

_HI = lax.Precision.HIGHEST


def _cparams(sem, vmem_mib=None):
    kw = dict(dimension_semantics=sem)
    if vmem_mib is not None:
        kw["vmem_limit_bytes"] = vmem_mib * 1024 * 1024
    return pltpu.CompilerParams(**kw)


def _rms(x, g):
    return x * lax.rsqrt(jnp.mean(x * x, axis=-1, keepdims=True) + EPS) * g


def _gelu(x):
    return jax.nn.gelu(x, approximate=True)


def _sigmoid(x):
    return 1.0 / (1.0 + jnp.exp(-x))


def _silu(x):
    return x * _sigmoid(x)


def _softplus(x):
    return jnp.maximum(x, 0.0) + jnp.log(1.0 + jnp.exp(-jnp.abs(x)))


def _rows_from_tiles(t_ref):
    return jnp.concatenate([t_ref[:, s, :] for s in range(ROW_TILES)], axis=-1)


def _norm_proj_kernel(*refs, widths, has_delta, emit_h):
    it = iter(refs)
    x_ref = next(it)
    d_ref = next(it) if has_delta else None
    g_ref = next(it)
    w_ref = next(it)
    outs = list(it)
    x = x_ref[...]
    if has_delta:
        x = x + _rows_from_tiles(d_ref)
    if emit_h:
        outs[0][...] = x
        outs = outs[1:]
    n = _rms(x, g_ref[...]).astype(BF16)
    y = jnp.dot(n, w_ref[...], preferred_element_type=F32)
    c = 0
    for o, wd in zip(outs, widths):
        o[...] = y[:, c:c + wd]
        c += wd


def _norm_proj(x, g, w, widths, delta=None, tm=512):
    n_tok, d = x.shape
    tm = min(tm, n_tok)
    has_delta = delta is not None
    row = pl.BlockSpec((tm, d), lambda i: (i, 0))
    tiles = pl.BlockSpec((tm, ROW_TILES, LANES), lambda i: (i, 0, 0))
    in_specs = [row] + ([tiles] if has_delta else []) + [
        pl.BlockSpec((1, d), lambda i: (0, 0)),
        pl.BlockSpec(w.shape, lambda i: (0, 0)),
    ]
    out_shape, out_specs = [], []
    if has_delta:
        out_shape.append(jax.ShapeDtypeStruct((n_tok, d), F32))
        out_specs.append(row)
    for wd in widths:
        out_shape.append(jax.ShapeDtypeStruct((n_tok, wd), F32))
        out_specs.append(pl.BlockSpec((tm, wd), lambda i: (i, 0)))
    args = [x] + ([delta] if has_delta else []) + [g.reshape(1, d), w]
    return pl.pallas_call(
        functools.partial(_norm_proj_kernel, widths=tuple(widths), has_delta=has_delta, emit_h=has_delta),
        grid=(n_tok // tm,),
        in_specs=in_specs,
        out_specs=out_specs,
        out_shape=out_shape,
        compiler_params=_cparams(("parallel",), 48),
        name="norm_proj",
    )(*args)


def _proj_res_kernel(*refs, n_in):
    h_ref = refs[0]
    a_refs = refs[1:1 + n_in]
    w_refs = refs[1 + n_in:1 + 2 * n_in]
    o_ref = refs[1 + 2 * n_in]
    acc = h_ref[...]
    for a, w in zip(a_refs, w_refs):
        acc = acc + jnp.dot(a[...].astype(BF16), w[...], preferred_element_type=F32)
    o_ref[...] = acc


def _proj_residual(h, acts, ws, tm=512):
    n_tok, d = h.shape
    tm = min(tm, n_tok)
    in_specs = [pl.BlockSpec((tm, d), lambda i: (i, 0))]
    in_specs += [pl.BlockSpec((tm, a.shape[1]), lambda i: (i, 0)) for a in acts]
    in_specs += [pl.BlockSpec(w.shape, lambda i: (0, 0)) for w in ws]
    return pl.pallas_call(
        functools.partial(_proj_res_kernel, n_in=len(acts)),
        grid=(n_tok // tm,),
        in_specs=in_specs,
        out_specs=pl.BlockSpec((tm, d), lambda i: (i, 0)),
        out_shape=jax.ShapeDtypeStruct((n_tok, d), F32),
        compiler_params=_cparams(("parallel",), 48),
        name="proj_residual",
    )(h, *acts, *ws)


def _add_norm_kernel(x_ref, d_ref, g_ref, o_ref):
    o_ref[...] = _rms(x_ref[...] + _rows_from_tiles(d_ref), g_ref[...])


def _add_norm(x, delta, g, tm=512):
    n_tok, d = x.shape
    tm = min(tm, n_tok)
    row = pl.BlockSpec((tm, d), lambda i: (i, 0))
    tiles = pl.BlockSpec((tm, ROW_TILES, LANES), lambda i: (i, 0, 0))
    return pl.pallas_call(
        _add_norm_kernel,
        grid=(n_tok // tm,),
        in_specs=[row, tiles, pl.BlockSpec((1, d), lambda i: (0, 0))],
        out_specs=row,
        out_shape=jax.ShapeDtypeStruct((n_tok, d), F32),
        compiler_params=_cparams(("parallel",)),
        name="add_norm",
    )(x, delta, g.reshape(1, d))


def _bdot(a, b):
    return jnp.dot(a.astype(BF16), b.astype(BF16), preferred_element_type=F32)


def _bdot_nt(a, b):
    return lax.dot_general(a.astype(BF16), b.astype(BF16), (((1,), (1,)), ((), ())), preferred_element_type=F32)


def _bdot_tn(a, b):
    return lax.dot_general(a.astype(BF16), b.astype(BF16), (((0,), (0,)), ((), ())), preferred_element_type=F32)


def _delta_kernel(qkv_ref, z_ref, ba_ref, prev_ref, s0_ref, cw_ref, al_ref, dt_ref, og_ref,
                  o_ref, s_ref, *scratch, period, carry):
    rows = qkv_ref.shape[0]
    n_sub = rows // period
    x = qkv_ref[...]
    if carry:
        tail_ref = scratch[0]
        j = pl.program_id(1)

        @pl.when(j == 0)
        def _():
            s_ref[0] = s0_ref[0]
            tail_ref[...] = jnp.zeros_like(tail_ref)
            tail_ref[SUBLANES - (DN_CONV - 1):, :] = prev_ref[0]

        xx = jnp.concatenate([tail_ref[...], x], axis=0)
        acc = jnp.zeros((rows, QKV_DIM), F32)
        for jj in range(DN_CONV):
            off = SUBLANES - (DN_CONV - 1) + jj
            acc = acc + xx[off:off + rows, :] * cw_ref[jj:jj + 1, :]
        tail_ref[...] = x[rows - SUBLANES:, :]
    else:
        pos = lax.broadcasted_iota(I32, (rows, QKV_DIM), 0) % period
        hist = prev_ref[...]
        acc = x * cw_ref[DN_CONV - 1:DN_CONV, :]
        for d in range(1, DN_CONV):
            shifted = jnp.where(pos >= d, pltpu.roll(x, d, 0), pltpu.roll(hist, d, 0))
            acc = acc + shifted * cw_ref[DN_CONV - 1 - d:DN_CONV - d, :]
    y = _silu(acc)
    ba = ba_ref[...]
    lane = lax.broadcasted_iota(I32, ba.shape, 1)
    bg = jnp.where(lane < DN_HEADS, _sigmoid(ba), -jnp.exp(al_ref[...]) * _softplus(ba + dt_ref[...]))

    row = lax.broadcasted_iota(I32, (rows, rows), 0)
    col = lax.broadcasted_iota(I32, (rows, rows), 1)
    same = (row // period) == (col // period)
    causal = same & (row >= col)
    strict = same & (row > col)
    eye = (row == col).astype(F32)
    gcs = jnp.dot(causal.astype(F32), bg, precision=_HI, preferred_element_type=F32)
    n_fac = int(math.log2(period)) - 1

    heads = range(DN_HEADS)
    qs, ks, vs, betas, gcl, decays, kbs = [], [], [], [], [], [], []
    for hd in heads:
        q = y[:, hd * DN_DK:(hd + 1) * DN_DK]
        k = y[:, DN_QK + hd * DN_DK:DN_QK + (hd + 1) * DN_DK]
        qs.append(q * lax.rsqrt(jnp.sum(q * q, axis=-1, keepdims=True) + EPS) * (DN_DK ** -0.5))
        ks.append(k * lax.rsqrt(jnp.sum(k * k, axis=-1, keepdims=True) + EPS))
        vs.append(y[:, 2 * DN_QK + hd * DN_DV:2 * DN_QK + (hd + 1) * DN_DV])
        betas.append(bg[:, hd:hd + 1])
        gc = gcs[:, DN_HEADS + hd:DN_HEADS + hd + 1]
        gc_cols = jnp.broadcast_to(gc, (rows, rows))
        decays.append(jnp.exp(jnp.where(causal, gc_cols - gc_cols.T, -jnp.inf)))
        gcl.append(gc)
        kbs.append(ks[hd] * betas[hd])
    a_s = [jnp.where(strict, _bdot_nt(kbs[hd], ks[hd]) * decays[hd], 0.0) for hd in heads]
    tinv = [eye - a for a in a_s]
    pw = [_bdot(a, a) for a in a_s]
    for f in range(n_fac):
        tinv = [t + _bdot(t, p) for t, p in zip(tinv, pw)]
        if f + 1 < n_fac:
            pw = [_bdot(p, p) for p in pw]
    us = [_bdot(tinv[hd], vs[hd] * betas[hd]) for hd in heads]
    ws = [_bdot(tinv[hd], kbs[hd] * jnp.exp(gcl[hd])) for hd in heads]
    qks = [_bdot_nt(qs[hd], ks[hd]) * decays[hd] for hd in heads]
    qgs = [qs[hd] * jnp.exp(gcl[hd]) for hd in heads]
    v_new = [[] for _ in heads]
    o_state = [[] for _ in heads]
    for c in range(n_sub):
        sl = slice(c * period, (c + 1) * period)
        for hd in heads:
            s_old = s_ref[0, hd] if carry else s0_ref[c, hd]
            vn = us[hd][sl] - _bdot(ws[hd][sl], s_old)
            o_state[hd].append(_bdot(qgs[hd][sl], s_old))
            glast = gcl[hd][(c + 1) * period - 1:(c + 1) * period, :]
            s_new = s_old * jnp.exp(glast) + _bdot_tn(ks[hd][sl] * jnp.exp(glast - gcl[hd][sl]), vn)
            if carry:
                s_ref[0, hd] = s_new
            else:
                s_ref[c, hd] = s_new
            v_new[hd].append(vn)
    for hd in heads:
        o = jnp.concatenate(o_state[hd], axis=0) + _bdot(qks[hd], jnp.concatenate(v_new[hd], axis=0))
        o = _rms(o, og_ref[...])
        o_ref[:, hd * DN_DV:(hd + 1) * DN_DV] = o * _silu(z_ref[:, hd * DN_DV:(hd + 1) * DN_DV])


_DELTA_ROWS = 256
_DELTA_SEQS = 16


def _delta_mixer(qkv, z, ba, c0, s0, b, t, conv_w, a_log, dt_bias, o_norm_g):
    al = jnp.zeros((1, LANES), F32).at[0, DN_HEADS:2 * DN_HEADS].set(a_log)
    dt = jnp.zeros((1, LANES), F32).at[0, DN_HEADS:2 * DN_HEADS].set(dt_bias)
    carry = t >= DN_CHUNK
    if carry:
        period, rows, spb = DN_CHUNK, _DELTA_ROWS, 1
        nj = t // rows
        grid = (b, nj)
        tok = lambda i, j: (i * nj + j, 0)
        st = lambda i, j: (i, 0, 0, 0)
        const = lambda i, j: (0, 0)
        prev, prev_spec = c0, pl.BlockSpec((1, DN_CONV - 1, QKV_DIM), lambda i, j: (i, 0, 0))
        scratch = [pltpu.VMEM((SUBLANES, QKV_DIM), F32)]
        sem = ("parallel", "arbitrary")
    else:
        period, spb = t, _DELTA_SEQS
        rows = spb * period
        grid = (b // spb,)
        tok = lambda i: (i, 0)
        st = lambda i: (i, 0, 0, 0)
        const = lambda i: (0, 0)
        hist = jnp.zeros((b, period, QKV_DIM), F32).at[:, period - (DN_CONV - 1):].set(c0)
        prev = jnp.roll(hist.reshape(b // spb, rows, QKV_DIM), -period, axis=1).reshape(b * period, QKV_DIM)
        prev_spec = pl.BlockSpec((rows, QKV_DIM), tok)
        scratch = []
        sem = ("parallel",)
    state = pl.BlockSpec((spb, DN_HEADS, DN_DK, DN_DV), st)
    return pl.pallas_call(
        functools.partial(_delta_kernel, period=period, carry=carry),
        grid=grid,
        in_specs=[
            pl.BlockSpec((rows, QKV_DIM), tok),
            pl.BlockSpec((rows, DN_V), tok),
            pl.BlockSpec((rows, LANES), tok),
            prev_spec,
            state,
            pl.BlockSpec((DN_CONV, QKV_DIM), const),
            pl.BlockSpec((1, LANES), const),
            pl.BlockSpec((1, LANES), const),
            pl.BlockSpec((1, DN_DV), const),
        ],
        out_specs=[pl.BlockSpec((rows, DN_V), tok), state],
        out_shape=[
            jax.ShapeDtypeStruct((b * t, DN_V), F32),
            jax.ShapeDtypeStruct((b, DN_HEADS, DN_DK, DN_DV), F32),
        ],
        scratch_shapes=scratch,
        compiler_params=_cparams(sem, 48),
        name="delta_mixer",
    )(qkv, z, ba, prev, s0, conv_w, al, dt, o_norm_g.reshape(1, DN_DV))


def _sgu_kernel(gu_ref, gv_ref, w_ref, b_ref, lg_ref, lb_ref, o_ref, *rest, period):
    vv_ref = rest[0] if rest else None
    n = gu_ref.shape[0]
    row = lax.broadcasted_iota(I32, (n, n), 0)
    col = lax.broadcasted_iota(I32, (n, n), 1)
    keep = row >= col
    if period < n:
        keep = keep & ((row // period) == (col // period))
    for g in range(SG_GROUPS):
        sl = slice(g * SG_CH, (g + 1) * SG_CH)
        u = _gelu(gu_ref[:, sl])
        x = _gelu(gv_ref[:, sl])
        mu = jnp.mean(x, axis=-1, keepdims=True)
        xc = x - mu
        var = jnp.mean(xc * xc, axis=-1, keepdims=True)
        vv = xc * lax.rsqrt(var + EPS) * lg_ref[:, sl] + lb_ref[:, sl]
        if vv_ref is not None:
            vv_ref[:, sl] = vv
        wm = jnp.where(keep, w_ref[g], 0.0).astype(BF16)
        mix = jnp.dot(wm, vv.astype(BF16), preferred_element_type=F32) + b_ref[g]
        o_ref[:, sl] = u * mix


def _sgu(gu, gv, w, bias, ln_g, ln_b, period, emit_vv):
    n_tok = gu.shape[0]
    blk = pl.BlockSpec((SG_CHUNK, SG_WIDTH), lambda i: (i, 0))
    out_shape = [jax.ShapeDtypeStruct((n_tok, SG_WIDTH), F32)]
    out_specs = [blk]
    if emit_vv:
        out_shape.append(jax.ShapeDtypeStruct((n_tok, SG_WIDTH), F32))
        out_specs.append(blk)
    return pl.pallas_call(
        functools.partial(_sgu_kernel, period=period),
        grid=(n_tok // SG_CHUNK,),
        in_specs=[
            blk, blk,
            pl.BlockSpec((SG_GROUPS, SG_CHUNK, SG_CHUNK), lambda i: (0, 0, 0)),
            pl.BlockSpec((SG_GROUPS, SG_CHUNK, SG_CH), lambda i: (0, 0, 0)),
            pl.BlockSpec((1, SG_WIDTH), lambda i: (0, 0)),
            pl.BlockSpec((1, SG_WIDTH), lambda i: (0, 0)),
        ],
        out_specs=out_specs,
        out_shape=out_shape,
        compiler_params=_cparams(("parallel",)),
        name="sgu",
    )(gu, gv, w, bias, ln_g.reshape(1, SG_WIDTH), ln_b.reshape(1, SG_WIDTH))


def _mem_attn_kernel(q_ref, k_ref, v_ref, o_ref):
    for hd in range(MEM_HEADS):
        sl = slice(hd * MEM_HD, (hd + 1) * MEM_HD)
        q = q_ref[0, :, sl].astype(BF16)
        k = k_ref[0, :, sl].astype(BF16)
        v = v_ref[0, :, sl].astype(BF16)
        s = lax.dot_general(q, k, (((1,), (1,)), ((), ())), preferred_element_type=F32) * (MEM_HD ** -0.5)
        m = jnp.max(s, axis=-1, keepdims=True)
        e = jnp.exp(s - m)
        p = e / jnp.sum(e, axis=-1, keepdims=True)
        o_ref[0, :, sl] = jnp.dot(p.astype(BF16), v, preferred_element_type=F32)


def _mem_attn_cache_kernel(q_ref, k_ref, v_ref, o_ref):
    t = q_ref.shape[1]
    q_all = jnp.concatenate([q_ref[0, :, hd * MEM_HD:(hd + 1) * MEM_HD] for hd in range(MEM_HEADS)], axis=0)
    k2 = k_ref[...].reshape(MEM_LEN * MEM_HEADS, MEM_HD).astype(BF16)
    v2 = v_ref[...].reshape(MEM_LEN * MEM_HEADS, MEM_HD).astype(BF16)
    s = lax.dot_general(q_all.astype(BF16), k2, (((1,), (1,)), ((), ())),
                        preferred_element_type=F32) * (MEM_HD ** -0.5)
    q_head = lax.broadcasted_iota(I32, s.shape, 0) // t
    k_head = lax.broadcasted_iota(I32, s.shape, 1) % MEM_HEADS
    s = jnp.where(q_head == k_head, s, -jnp.inf)
    m = jnp.max(s, axis=-1, keepdims=True)
    e = jnp.exp(s - m)
    p = e / jnp.sum(e, axis=-1, keepdims=True)
    out = jnp.dot(p.astype(BF16), v2, preferred_element_type=F32)
    for hd in range(MEM_HEADS):
        o_ref[0, :, hd * MEM_HD:(hd + 1) * MEM_HD] = out[hd * t:(hd + 1) * t, :]


def _mem_attn(q, mk, mv, layer=None):
    b, t, d = q.shape
    tq = min(t, 512)
    if layer is None:
        kv = pl.BlockSpec((1, MEM_LEN, d), lambda i, j: (i, 0, 0))
        body = _mem_attn_kernel
    else:
        kv = pl.BlockSpec((None, None, MEM_LEN, MEM_HEADS, MEM_HD), lambda i, j: (layer, i, 0, 0, 0))
        body = _mem_attn_cache_kernel
    return pl.pallas_call(
        body,
        grid=(b, t // tq),
        in_specs=[pl.BlockSpec((1, tq, d), lambda i, j: (i, j, 0)), kv, kv],
        out_specs=pl.BlockSpec((1, tq, d), lambda i, j: (i, j, 0)),
        out_shape=jax.ShapeDtypeStruct((b, t, d), F32),
        compiler_params=_cparams(("parallel", "parallel")),
        name="mem_attn",
    )(q, mk, mv)


def _top_rows(s, n_top, rank=None, payload=None):
    if rank is None:
        rank = lax.broadcasted_iota(I32, s.shape, 0)
    big = jnp.int32(2 ** 30)
    vals, ids, pays = [], [], []
    for _ in range(n_top):
        m = jnp.max(s, axis=0, keepdims=True)
        i = jnp.min(jnp.where(s == m, rank, big), axis=0, keepdims=True)
        hit = rank == i
        vals.append(m)
        ids.append(i)
        if payload is not None:
            pays.append(jnp.max(jnp.where(hit, payload, -1), axis=0, keepdims=True))
        s = jnp.where(hit, -jnp.inf, s)
    out = [jnp.concatenate(vals, axis=0), jnp.concatenate(ids, axis=0)]
    if payload is not None:
        out.append(jnp.concatenate(pays, axis=0))
    return out


_PAIR_GROUPS = ((0, 0), (0, 8), (1, 0), (2, 0), (3, 0), (4, 0), (5, 0), (6, 0), (7, 0))


def _pair_candidates(sv0, si0, sv1, si1):
    sub = lax.broadcasted_iota(I32, (SUBLANES,) + sv0.shape[1:], 0)
    cand, flat, eid = [], [], []
    for a, b0 in _PAIR_GROUPS:
        cand.append(sv0[a:a + 1, :] + sv1[b0:b0 + SUBLANES, :])
        flat.append(a * PEER_TOPK + b0 + sub)
        eid.append(si0[a:a + 1, :] * PEER_NKEYS + si1[b0:b0 + SUBLANES, :])
    cand.append(sv0[SUBLANES:, :] + sv1[0:1, :])
    flat.append((SUBLANES + sub) * PEER_TOPK)
    eid.append(si0[SUBLANES:, :] * PEER_NKEYS + si1[0:1, :])
    return jnp.concatenate(cand, axis=0), jnp.concatenate(flat, axis=0), jnp.concatenate(eid, axis=0)


def _peer_route_kernel(x_ref, g_ref, wq_ref, keys_ref,
                       xn_ref, roff_ref, sh_ref, gate_ref, e_scr, g_scr):
    n = _rms(x_ref[...], g_ref[...])
    for s in range(ROW_TILES):
        xn_ref[:, s, :] = n[:, s * LANES:(s + 1) * LANES]
    q = jnp.dot(n.astype(BF16), wq_ref[...], preferred_element_type=F32)
    for hd in range(PEER_HEADS):
        qh = q[:, hd * PEER_DKEY:(hd + 1) * PEER_DKEY].astype(BF16)
        tops = []
        for p in range(2):
            s_t = lax.dot_general(keys_ref[hd, p], qh, (((1,), (1,)), ((), ())),
                                  preferred_element_type=F32)
            tops.append(_top_rows(s_t, PEER_TOPK))
        (sv0, si0), (sv1, si1) = tops
        cand, flat, ecand = _pair_candidates(sv0, si0, sv1, si1)
        cv, _, ce = _top_rows(cand, PEER_TOPK, rank=flat, payload=ecand)
        ex = jnp.exp(cv - cv[0:1, :])
        gate = ex / jnp.sum(ex, axis=0, keepdims=True)
        e_scr[hd * PEER_TOPK:(hd + 1) * PEER_TOPK, :] = ce
        g_scr[hd * PEER_TOPK:(hd + 1) * PEER_TOPK, :] = gate
    e_t = pltpu.bitcast(pltpu.bitcast(e_scr[...], F32).T, I32)
    hi = e_t >= HALF_EXPERTS
    roff_ref[...] = jnp.where(hi, e_t - HALF_EXPERTS, e_t) * ROW_TILES
    sh_ref[...] = jnp.where(hi, 0, 16)
    gate_ref[...] = g_scr[...].T


def _peer_route(h, g, wq, keys_pad):
    n_tok, d = h.shape
    tb = LANES
    row = pl.BlockSpec((tb, d), lambda i: (i, 0))
    slot = pl.BlockSpec((tb, PEER_SLOTS), lambda i: (i, 0))
    return pl.pallas_call(
        _peer_route_kernel,
        grid=(n_tok // tb,),
        in_specs=[
            row,
            pl.BlockSpec((1, d), lambda i: (0, 0)),
            pl.BlockSpec(wq.shape, lambda i: (0, 0)),
            pl.BlockSpec(keys_pad.shape, lambda i: (0, 0, 0, 0)),
        ],
        out_specs=[pl.BlockSpec((tb, ROW_TILES, LANES), lambda i: (i, 0, 0)), slot, slot, slot],
        out_shape=[
            jax.ShapeDtypeStruct((n_tok, ROW_TILES, LANES), F32),
            jax.ShapeDtypeStruct((n_tok, PEER_SLOTS), I32),
            jax.ShapeDtypeStruct((n_tok, PEER_SLOTS), I32),
            jax.ShapeDtypeStruct((n_tok, PEER_SLOTS), F32),
        ],
        scratch_shapes=[pltpu.VMEM((PEER_SLOTS, tb), I32), pltpu.VMEM((PEER_SLOTS, tb), F32)],
        compiler_params=_cparams(("parallel",), 48),
        name="peer_route",
    )(h, g.reshape(1, d), wq, keys_pad)


def _expert_tile(tab_ref, roff, shv):
    w = tab_ref[pl.ds(pl.multiple_of(roff, SUBLANES), SUBLANES), :]
    return pltpu.bitcast(w << shv, F32)


def _fold_rows(ps):
    sub = lax.broadcasted_iota(I32, (SUBLANES, LANES), 0)
    dist = SUBLANES // 2
    while len(ps) > 1:
        lo = (sub % (2 * dist)) < dist
        half = len(ps) // 2
        nxt = []
        for a in range(half):
            x, y = ps[a], ps[a + half]
            nxt.append(jnp.where(lo, x, pltpu.roll(y, dist, 0))
                       + jnp.where(lo, pltpu.roll(x, SUBLANES - dist, 0), y))
        ps = nxt
        dist //= 2
    return ps[0]


def _slot_columns(rows8, dst_ref):
    for u in range(SUBLANES):
        dst_ref[u] = jnp.broadcast_to(rows8[u:u + 1, :], (PEER_SLOTS, LANES)).T


def _bcast_row(ref, u, k):
    return jnp.broadcast_to(ref[u, k:k + 1, :], (SUBLANES, LANES))


def _peer_token_loop(tb, prep, token_group):
    n_pairs = tb // (2 * SUBLANES)
    prep(0, 0)

    def pair(i, carry):
        g0 = 2 * i
        prep(g0 + 1, 1)
        token_group(g0, 0)
        prep(jnp.minimum(g0 + 2, 2 * n_pairs - 1), 0)
        token_group(g0 + 1, 1)
        return carry

    lax.fori_loop(0, n_pairs, pair, 0)


def _peer_up_kernel(roff_ref, sh_ref, gate_ref, x_ref, tab_ref, c_ref, shs_a, shs_b):
    tb = sh_ref.shape[0]
    shs = (shs_a, shs_b)

    def prep(grp, par):
        r0 = pl.multiple_of(grp * SUBLANES, SUBLANES)
        _slot_columns(pltpu.bitcast(sh_ref[pl.ds(r0, SUBLANES), :], F32), shs[par])

    def token_group(grp, par):
        rows = []
        for u in range(SUBLANES):
            t = grp * SUBLANES + u
            x_t = x_ref[t]
            tiles = []
            for g in range(PEER_SLOTS // SUBLANES):
                ps = []
                for jj in range(SUBLANES):
                    k = g * SUBLANES + jj
                    shv = pltpu.bitcast(_bcast_row(shs[par], u, k), U32)
                    ps.append(_expert_tile(tab_ref, roff_ref[t, k], shv) * x_t)
                tiles.append(_fold_rows(ps))
            r = jnp.concatenate(tiles, axis=0)
            rows.append(jnp.sum(r.T, axis=0, keepdims=True))
        out = pl.ds(pl.multiple_of(grp * SUBLANES, SUBLANES), SUBLANES)
        c_ref[out, :] = gate_ref[out, :] * _gelu(jnp.concatenate(rows, axis=0))

    _peer_token_loop(tb, prep, token_group)


def _peer_down_kernel(roff_hbm, sh_ref, sh_next, c_ref, c_next, tab_ref, o_ref,
                      idx_a, idx_b, sh_a, sh_b, c_a, c_b, sem):
    i = pl.program_id(0)
    n = pl.num_programs(0)
    tb = sh_ref.shape[0]
    th = tb // 2
    idx = (idx_a, idx_b)

    def idx_copy(step, half):
        row0 = pl.multiple_of(step * tb + half * th, th)
        return pltpu.make_async_copy(roff_hbm.at[pl.ds(row0, th)], idx[half], sem.at[half])

    def build(src_sh, src_c, row, sh_t, c_t, u, after=None):
        sh_row = src_sh[row:row + 1, :]
        c_row = src_c[row:row + 1, :]
        if after is not None:
            zero = lax.shift_right_logical(pltpu.bitcast(after[0:1, :], U32), jnp.uint32(32))
            sh_row = sh_row + pltpu.bitcast(zero, I32)
            c_row = c_row + pltpu.bitcast(zero, F32)
        sh_t[u] = jnp.broadcast_to(pltpu.bitcast(sh_row, F32), (PEER_SLOTS, LANES)).T
        c_t[u] = jnp.broadcast_to(c_row, (PEER_SLOTS, LANES)).T

    def consume(half, sh_t, c_t, u):
        accs = [jnp.zeros((SUBLANES, LANES), F32) for _ in range(2)]
        for k in range(PEER_SLOTS):
            shv = pltpu.bitcast(_bcast_row(sh_t, u, k), U32)
            accs[k % 2] = accs[k % 2] + _expert_tile(tab_ref, idx[half][u, k], shv) * _bcast_row(c_t, u, k)
        acc = accs[0] + accs[1]
        o_ref[half * th + u] = acc
        return acc

    def half_step(half, use, fill, src_sh, src_c, row0):
        @pl.when(i >= 0)
        def _():
            acc = None
            for u in range(th):
                build(src_sh, src_c, row0 + u, fill[0], fill[1], u, after=acc)
                acc = consume(half, use[0], use[1], u)

    tiles = ((sh_a, c_a), (sh_b, c_b))

    @pl.when(i == 0)
    def _():
        idx_copy(0, 0).start()
        idx_copy(0, 1).start()
        for u in range(th):
            build(sh_ref, c_ref, u, sh_a, c_a, u)

    idx_copy(i, 0).wait()
    half_step(0, tiles[0], tiles[1], sh_ref, c_ref, th)

    @pl.when(i + 1 < n)
    def _():
        idx_copy(i + 1, 0).start()

    idx_copy(i, 1).wait()
    half_step(1, tiles[1], tiles[0], sh_next, c_next, 0)

    @pl.when(i + 1 < n)
    def _():
        idx_copy(i + 1, 1).start()


_PEER_TB = 64
_PEER_DOWN_TB = 64
_PEER_TABLE_MIB = HALF_EXPERTS * D_MODEL * 4 // 2 ** 20
_PEER_VMEM_MIB = min(_PEER_TABLE_MIB + 20, VMEM_MIB_V7X - 8)
_COL_TILES = pltpu.VMEM((SUBLANES, PEER_SLOTS, LANES), F32)


def _table_spec(tab):
    return pl.BlockSpec(tab.shape, lambda i: (0, 0), pipeline_mode=pl.Buffered(1))


def _peer_up(roff, sh, gate, x_tiles, tab):
    n_tok = roff.shape[0]
    tb = _PEER_TB
    slot = pl.BlockSpec((tb, PEER_SLOTS), lambda i: (i, 0))
    return pl.pallas_call(
        _peer_up_kernel,
        grid=(n_tok // tb,),
        in_specs=[
            pl.BlockSpec((tb, PEER_SLOTS), lambda i: (i, 0), memory_space=pltpu.SMEM),
            slot, slot,
            pl.BlockSpec((tb, ROW_TILES, LANES), lambda i: (i, 0, 0)),
            _table_spec(tab),
        ],
        out_specs=slot,
        out_shape=jax.ShapeDtypeStruct((n_tok, PEER_SLOTS), F32),
        scratch_shapes=[_COL_TILES] * 2,
        compiler_params=_cparams(("arbitrary",), _PEER_VMEM_MIB),
        name="peer_up",
    )(roff, sh, gate, x_tiles, tab)


def _peer_down(roff, sh, coef, tab):
    n_tok = roff.shape[0]
    tb = _PEER_DOWN_TB
    th = tb // 2
    n_steps = n_tok // tb
    cur = pl.BlockSpec((tb, PEER_SLOTS), lambda i: (i, 0))
    nxt = pl.BlockSpec((tb, PEER_SLOTS), lambda i: (jnp.minimum(i + 1, n_steps - 1), 0))
    half_tiles = pltpu.VMEM((th, PEER_SLOTS, LANES), F32)
    return pl.pallas_call(
        _peer_down_kernel,
        grid=(n_steps,),
        in_specs=[pl.BlockSpec(memory_space=pl.ANY), cur, nxt, cur, nxt, _table_spec(tab)],
        out_specs=pl.BlockSpec((tb, ROW_TILES, LANES), lambda i: (i, 0, 0)),
        out_shape=jax.ShapeDtypeStruct((n_tok, ROW_TILES, LANES), F32),
        scratch_shapes=[pltpu.SMEM((th, PEER_SLOTS), I32)] * 2 + [half_tiles] * 4
        + [pltpu.SemaphoreType.DMA((2,))],
        compiler_params=_cparams(("arbitrary",), _PEER_VMEM_MIB),
        name="peer_down",
    )(roff, sh, sh, coef, coef, tab)


def _pack_kernel(lo_ref, hi_ref, o_ref):
    low = pltpu.bitcast(pltpu.bitcast(lo_ref[...].astype(BF16).astype(F32), U32) >> 16, I32)
    bits = pltpu.bitcast(hi_ref[...], I32)
    sign = bits & jnp.int32(-2 ** 31)
    mag = jnp.minimum(bits & jnp.int32(0x7FFFFFFF), jnp.int32(0x7F7F0000))
    top = jnp.maximum(mag + jnp.int32(0x8000) - low, 0) & jnp.int32(-65536)
    word = pltpu.bitcast(sign | top | low, U32)
    for s in range(ROW_TILES):
        o_ref[:, s, :] = word[:, s * LANES:(s + 1) * LANES]


def _pack_table(tabs, layer, rows=512):
    d = tabs.shape[2]
    n_blk = HALF_EXPERTS // rows
    packed = pl.pallas_call(
        _pack_kernel,
        grid=(n_blk,),
        in_specs=[pl.BlockSpec((None, rows, d), lambda i: (layer, i, 0)),
                  pl.BlockSpec((None, rows, d), lambda i: (layer, i + n_blk, 0))],
        out_specs=pl.BlockSpec((rows, ROW_TILES, LANES), lambda i: (i, 0, 0)),
        out_shape=jax.ShapeDtypeStruct((HALF_EXPERTS, ROW_TILES, LANES), U32),
        compiler_params=_cparams(("parallel",), 48),
        name="pack_table",
    )(tabs, tabs)
    return packed.reshape(HALF_EXPERTS * ROW_TILES, LANES)


def _peer(h, g, wq, keys_pad, u_tab, v_tab):
    xn, roff, sh, gate = _peer_route(h, g, wq, keys_pad)
    coef = _peer_up(roff, sh, gate, xn, u_tab)
    return _peer_down(roff, sh, coef, v_tab)


def _prep_layer(l, w_in, conv_w, a_log, dt_bias, o_norm_g, sg_ln_g, sg_ln_b, sg_w, sg_b, w_out,
                norm_mix_g, norm_mem_g, mem_norm_g, w_mq, w_mk, w_mv, w_mo, norm_ffn_g,
                peer_wq, peer_keys, peer_u, peer_v):
    o1 = QKV_DIM
    o2 = o1 + DN_V
    o4 = o2 + 2 * DN_HEADS
    o5 = o4 + SG_WIDTH
    wi = w_in[l]
    ba_cols = jnp.pad(wi[:, o2:o4], ((0, 0), (0, LANES - 2 * DN_HEADS)))
    w_in_r = jnp.concatenate([wi[:, :o2], wi[:, o4:], ba_cols], axis=1).astype(BF16)
    kz = jnp.zeros((PEER_HEADS, PEER_NKEYS, PEER_DKEY // 2), F32)
    keys_pad = jnp.stack([jnp.concatenate([peer_keys[l][:, 0], kz], axis=-1),
                          jnp.concatenate([kz, peer_keys[l][:, 1]], axis=-1)], axis=1).astype(BF16)
    return dict(
        w_in=w_in_r, conv_w=conv_w[l], a_log=a_log[l], dt_bias=dt_bias[l], o_norm_g=o_norm_g[l],
        sg_ln_g=sg_ln_g[l].reshape(-1), sg_ln_b=sg_ln_b[l].reshape(-1), sg_w=sg_w[l], sg_b=sg_b[l],
        w_out_a=w_out[l][:DN_V].astype(BF16), w_out_b=w_out[l][DN_V:].astype(BF16),
        norm_mix_g=norm_mix_g[l], norm_mem_g=norm_mem_g[l], mem_norm_g=mem_norm_g[l],
        w_mq=w_mq[l].astype(BF16), w_mkv=jnp.concatenate([w_mk[l], w_mv[l]], axis=1).astype(BF16),
        w_mo=w_mo[l].astype(BF16), norm_ffn_g=norm_ffn_g[l], peer_wq=peer_wq[l].astype(BF16),
        keys_pad=keys_pad, u_tab=_pack_table(peer_u, l), v_tab=_pack_table(peer_v, l),
    )


_IN_WIDTHS = (QKV_DIM, DN_V, SG_WIDTH, SG_WIDTH, LANES)


def _layer(h, delta, b, t, mk, mv, s0, c0, p, emit_vv, cache_layer=None):
    n_tok = b * t
    outs = _norm_proj(h, p["norm_mix_g"], p["w_in"], _IN_WIDTHS, delta=delta)
    if delta is not None:
        h, outs = outs[0], outs[1:]
    qkv, z, gu, gv, ba = outs
    o_a, s_new = _delta_mixer(qkv, z, ba, c0, s0, b, t, p["conv_w"], p["a_log"], p["dt_bias"], p["o_norm_g"])
    conv_new = qkv.reshape(b, t, QKV_DIM)[:, t - (DN_CONV - 1):, :]
    period = min(t, SG_CHUNK)
    reps = SG_CHUNK // period
    sg_w = jnp.tile(p["sg_w"][:, :period, :period], (1, reps, reps))
    sg_bias = jnp.broadcast_to(jnp.tile(p["sg_b"][:, :period], (1, reps))[:, :, None],
                               (SG_GROUPS, SG_CHUNK, SG_CH))
    sg_out = _sgu(gu, gv, sg_w, sg_bias, p["sg_ln_g"], p["sg_ln_b"], period, emit_vv)
    o_b = sg_out[0]
    vv = sg_out[1] if emit_vv else None
    h = _proj_residual(h, [o_a.reshape(n_tok, DN_V), o_b], [p["w_out_a"], p["w_out_b"]])
    (q,) = _norm_proj(h, p["norm_mem_g"], p["w_mq"], (D_MODEL,))
    att = _mem_attn(q.reshape(b, t, D_MODEL), mk, mv, layer=cache_layer)
    h = _proj_residual(h, [att.reshape(n_tok, D_MODEL)], [p["w_mo"]])
    return h, s_new, conv_new, vv


def kernel(x_prompt, x_sample, state_delta, state_conv, cache_mem_k, cache_mem_v, mem_prompt, w_in, conv_w, a_log, dt_bias, o_norm_g, sg_ln_g, sg_ln_b, sg_w, sg_b, w_out, norm_mix_g, norm_mem_g, mem_norm_g, w_mq, w_mk, w_mv, w_mo, norm_ffn_g, peer_wq, peer_keys, peer_u, peer_v, final_norm_g):
    depth = w_in.shape[0]
    bp, tp, d = x_prompt.shape
    bs, ts, _ = x_sample.shape
    hp = x_prompt.reshape(bp * tp, d)
    hs = x_sample.reshape(bs * ts, d)
    dp = ds = None
    mem_flat = mem_prompt.reshape(bp * MEM_LEN, d)
    sd_p, sc_p, mk_p, mv_p, sd_s, sc_s, vr_s = [], [], [], [], [], [], []
    for l in range(depth):
        p = _prep_layer(l, w_in, conv_w, a_log, dt_bias, o_norm_g, sg_ln_g, sg_ln_b, sg_w, sg_b, w_out,
                        norm_mix_g, norm_mem_g, mem_norm_g, w_mq, w_mk, w_mv, w_mo, norm_ffn_g,
                        peer_wq, peer_keys, peer_u, peer_v)
        mk, mv = _norm_proj(mem_flat, p["mem_norm_g"], p["w_mkv"], (D_MODEL, D_MODEL))
        mk = mk.reshape(bp, MEM_LEN, d)
        mv = mv.reshape(bp, MEM_LEN, d)
        s0 = jnp.zeros((bp, DN_HEADS, DN_DK, DN_DV), F32)
        c0 = jnp.zeros((bp, DN_CONV - 1, QKV_DIM), F32)
        hp, s_p, c_p, _ = _layer(hp, dp, bp, tp, mk, mv, s0, c0, p, False)
        hs, s_s, c_s, vv = _layer(hs, ds, bs, ts, cache_mem_k, cache_mem_v, state_delta[l], state_conv[l], p, True,
                                  cache_layer=l)
        dp = _peer(hp, p["norm_ffn_g"], p["peer_wq"], p["keys_pad"], p["u_tab"], p["v_tab"])
        ds = _peer(hs, p["norm_ffn_g"], p["peer_wq"], p["keys_pad"], p["u_tab"], p["v_tab"])
        sd_p.append(s_p)
        sc_p.append(c_p)
        mk_p.append(mk.reshape(bp, MEM_LEN, MEM_HEADS, MEM_HD))
        mv_p.append(mv.reshape(bp, MEM_LEN, MEM_HEADS, MEM_HD))
        sd_s.append(s_s)
        sc_s.append(c_s)
        vr_s.append(vv.reshape(bs, ts, SG_WIDTH))
    y_prompt = _add_norm(hp, dp, final_norm_g).reshape(bp, tp, d)
    y_sample = _add_norm(hs, ds, final_norm_g).reshape(bs, ts, d)
    return (y_prompt, y_sample, jnp.stack(sd_p), jnp.stack(sc_p), jnp.stack(mk_p), jnp.stack(mv_p),
            jnp.stack(sd_s), jnp.stack(sc_s), jnp.stack(vr_s))
```

```python
import functools
import math

import jax
import jax.numpy as jnp
from jax import lax
from jax.experimental import pallas as pl
from jax.experimental.pallas import tpu as pltpu

F32 = jnp.float32
BF16 = jnp.bfloat16
I32 = jnp.int32
U32 = jnp.uint32

D_MODEL = 1024
DN_HEADS = 4
DN_DK = 128
DN_DV = 128
DN_QK = DN_HEADS * DN_DK
DN_V = DN_HEADS * DN_DV
DN_CONV = 4
DN_CHUNK = 64
QKV_DIM = 2 * DN_QK + DN_V
SG_GROUPS = 4
SG_CH = 128
SG_WIDTH = SG_GROUPS * SG_CH
SG_CHUNK = 128
MEM_LEN = 256
MEM_HEADS = 4
MEM_HD = D_MODEL // MEM_HEADS
PEER_HEADS = 8
PEER_NKEYS = 128
PEER_NEXP = PEER_NKEYS * PEER_NKEYS
PEER_DKEY = 128
PEER_TOPK = 16
PEER_SLOTS = PEER_HEADS * PEER_TOPK
EPS = 1e-6

SUBLANES = 8
LANES = 128
ROW_TILES = D_MODEL // LANES
VMEM_MIB_V7X = 64
HALF_EXPERTS = PEER_NEXP // 2

_HI = lax.Precision.HIGHEST


def _cparams(sem, vmem_mib=None):
    kw = dict(dimension_semantics=sem)
    if vmem_mib is not None:
        kw["vmem_limit_bytes"] = vmem_mib * 1024 * 1024
    return pltpu.CompilerParams(**kw)


def _rms(x, g):
    return x * lax.rsqrt(jnp.mean(x * x, axis=-1, keepdims=True) + EPS) * g


def _gelu(x):
    return jax.nn.gelu(x, approximate=True)


def _sigmoid(x):
    return 1.0 / (1.0 + jnp.exp(-x))


def _silu(x):
    return x * _sigmoid(x)


def _softplus(x):
    return jnp.maximum(x, 0.0) + jnp.log(1.0 + jnp.exp(-jnp.abs(x)))


def _rows_from_tiles(t_ref):
    return jnp.concatenate([t_ref[:, s, :] for s in range(ROW_TILES)], axis=-1)


def _norm_proj_kernel(*refs, widths, has_delta, emit_h):
    it = iter(refs)
    x_ref = next(it)
    d_ref = next(it) if has_delta else None
    g_ref = next(it)
    w_ref = next(it)
    outs = list(it)
    x = x_ref[...]
    if has_delta:
        x = x + _rows_from_tiles(d_ref)
    if emit_h:
        outs[0][...] = x
        outs = outs[1:]
    n = _rms(x, g_ref[...]).astype(BF16)
    y = jnp.dot(n, w_ref[...], preferred_element_type=F32)
    c = 0
    for o, wd in zip(outs, widths):
        o[...] = y[:, c:c + wd]
        c += wd


def _norm_proj(x, g, w, widths, delta=None, tm=512):
    n_tok, d = x.shape
    tm = min(tm, n_tok)
    has_delta = delta is not None
    row = pl.BlockSpec((tm, d), lambda i: (i, 0))
    tiles = pl.BlockSpec((tm, ROW_TILES, LANES), lambda i: (i, 0, 0))
    in_specs = [row] + ([tiles] if has_delta else []) + [
        pl.BlockSpec((1, d), lambda i: (0, 0)),
        pl.BlockSpec(w.shape, lambda i: (0, 0)),
    ]
    out_shape, out_specs = [], []
    if has_delta:
        out_shape.append(jax.ShapeDtypeStruct((n_tok, d), F32))
        out_specs.append(row)
    for wd in widths:
        out_shape.append(jax.ShapeDtypeStruct((n_tok, wd), F32))
        out_specs.append(pl.BlockSpec((tm, wd), lambda i: (i, 0)))
    args = [x] + ([delta] if has_delta else []) + [g.reshape(1, d), w]
    return pl.pallas_call(
        functools.partial(_norm_proj_kernel, widths=tuple(widths), has_delta=has_delta, emit_h=has_delta),
        grid=(n_tok // tm,),
        in_specs=in_specs,
        out_specs=out_specs,
        out_shape=out_shape,
        compiler_params=_cparams(("parallel",), 48),
        name="norm_proj",
    )(*args)


def _proj_res_kernel(*refs, n_in):
    h_ref = refs[0]
    a_refs = refs[1:1 + n_in]
    w_refs = refs[1 + n_in:1 + 2 * n_in]
    o_ref = refs[1 + 2 * n_in]
    acc = h_ref[...]
    for a, w in zip(a_refs, w_refs):
        acc = acc + jnp.dot(a[...].astype(BF16), w[...], preferred_element_type=F32)
    o_ref[...] = acc


def _proj_residual(h, acts, ws, tm=512):
    n_tok, d = h.shape
    tm = min(tm, n_tok)
    in_specs = [pl.BlockSpec((tm, d), lambda i: (i, 0))]
    in_specs += [pl.BlockSpec((tm, a.shape[1]), lambda i: (i, 0)) for a in acts]
    in_specs += [pl.BlockSpec(w.shape, lambda i: (0, 0)) for w in ws]
    return pl.pallas_call(
        functools.partial(_proj_res_kernel, n_in=len(acts)),
        grid=(n_tok // tm,),
        in_specs=in_specs,
        out_specs=pl.BlockSpec((tm, d), lambda i: (i, 0)),
        out_shape=jax.ShapeDtypeStruct((n_tok, d), F32),
        compiler_params=_cparams(("parallel",), 48),
        name="proj_residual",
    )(h, *acts, *ws)


def _add_norm_kernel(x_ref, d_ref, g_ref, o_ref):
    o_ref[...] = _rms(x_ref[...] + _rows_from_tiles(d_ref), g_ref[...])


def _add_norm(x, delta, g, tm=512):
    n_tok, d = x.shape
    tm = min(tm, n_tok)
    row = pl.BlockSpec((tm, d), lambda i: (i, 0))
    tiles = pl.BlockSpec((tm, ROW_TILES, LANES), lambda i: (i, 0, 0))
    return pl.pallas_call(
        _add_norm_kernel,
        grid=(n_tok // tm,),
        in_specs=[row, tiles, pl.BlockSpec((1, d), lambda i: (0, 0))],
        out_specs=row,
        out_shape=jax.ShapeDtypeStruct((n_tok, d), F32),
        compiler_params=_cparams(("parallel",)),
        name="add_norm",
    )(x, delta, g.reshape(1, d))


def _bdot(a, b):
    return jnp.dot(a.astype(BF16), b.astype(BF16), preferred_element_type=F32)


def _bdot_nt(a, b):
    return lax.dot_general(a.astype(BF16), b.astype(BF16), (((1,), (1,)), ((), ())), preferred_element_type=F32)


def _bdot_tn(a, b):
    return lax.dot_general(a.astype(BF16), b.astype(BF16), (((0,), (0,)), ((), ())), preferred_element_type=F32)


def _delta_kernel(qkv_ref, z_ref, ba_ref, prev_ref, s0_ref, cw_ref, al_ref, dt_ref, og_ref,
                  o_ref, s_ref, *scratch, period, carry):
    rows = qkv_ref.shape[0]
    n_sub = rows // period
    x = qkv_ref[...]
    if carry:
        tail_ref = scratch[0]
        j = pl.program_id(1)

        @pl.when(j == 0)
        def _():
            s_ref[0] = s0_ref[0]
            tail_ref[...] = jnp.zeros_like(tail_ref)
            tail_ref[SUBLANES - (DN_CONV - 1):, :] = prev_ref[0]

        xx = jnp.concatenate([tail_ref[...], x], axis=0)
        acc = jnp.zeros((rows, QKV_DIM), F32)
        for jj in range(DN_CONV):
            off = SUBLANES - (DN_CONV - 1) + jj
            acc = acc + xx[off:off + rows, :] * cw_ref[jj:jj + 1, :]
        tail_ref[...] = x[rows - SUBLANES:, :]
    else:
        pos = lax.broadcasted_iota(I32, (rows, QKV_DIM), 0) % period
        hist = prev_ref[...]
        acc = x * cw_ref[DN_CONV - 1:DN_CONV, :]
        for d in range(1, DN_CONV):
            shifted = jnp.where(pos >= d, pltpu.roll(x, d, 0), pltpu.roll(hist, d, 0))
            acc = acc + shifted * cw_ref[DN_CONV - 1 - d:DN_CONV - d, :]
    y = _silu(acc)
    ba = ba_ref[...]
    lane = lax.broadcasted_iota(I32, ba.shape, 1)
    bg = jnp.where(lane < DN_HEADS, _sigmoid(ba), -jnp.exp(al_ref[...]) * _softplus(ba + dt_ref[...]))

    row = lax.broadcasted_iota(I32, (rows, rows), 0)
    col = lax.broadcasted_iota(I32, (rows, rows), 1)
    same = (row // period) == (col // period)
    causal = same & (row >= col)
    strict = same & (row > col)
    eye = (row == col).astype(F32)
    gcs = jnp.dot(causal.astype(F32), bg, precision=_HI, preferred_element_type=F32)
    n_fac = int(math.log2(period)) - 1

    heads = range(DN_HEADS)
    qs, ks, vs, betas, gcl, decays, kbs = [], [], [], [], [], [], []
    for hd in heads:
        q = y[:, hd * DN_DK:(hd + 1) * DN_DK]
        k = y[:, DN_QK + hd * DN_DK:DN_QK + (hd + 1) * DN_DK]
        qs.append(q * lax.rsqrt(jnp.sum(q * q, axis=-1, keepdims=True) + EPS) * (DN_DK ** -0.5))
        ks.append(k * lax.rsqrt(jnp.sum(k * k, axis=-1, keepdims=True) + EPS))
        vs.append(y[:, 2 * DN_QK + hd * DN_DV:2 * DN_QK + (hd + 1) * DN_DV])
        betas.append(bg[:, hd:hd + 1])
        gc = gcs[:, DN_HEADS + hd:DN_HEADS + hd + 1]
        gc_cols = jnp.broadcast_to(gc, (rows, rows))
        decays.append(jnp.exp(jnp.where(causal, gc_cols - gc_cols.T, -jnp.inf)))
        gcl.append(gc)
        kbs.append(ks[hd] * betas[hd])
    a_s = [jnp.where(strict, _bdot_nt(kbs[hd], ks[hd]) * decays[hd], 0.0) for hd in heads]
    tinv = [eye - a for a in a_s]
    pw = [_bdot(a, a) for a in a_s]
    for f in range(n_fac):
        tinv = [t + _bdot(t, p) for t, p in zip(tinv, pw)]
        if f + 1 < n_fac:
            pw = [_bdot(p, p) for p in pw]
    us = [_bdot(tinv[hd], vs[hd] * betas[hd]) for hd in heads]
    ws = [_bdot(tinv[hd], kbs[hd] * jnp.exp(gcl[hd])) for hd in heads]
    qks = [_bdot_nt(qs[hd], ks[hd]) * decays[hd] for hd in heads]
    qgs = [qs[hd] * jnp.exp(gcl[hd]) for hd in heads]
    v_new = [[] for _ in heads]
    o_state = [[] for _ in heads]
    for c in range(n_sub):
        sl = slice(c * period, (c + 1) * period)
        for hd in heads:
            s_old = s_ref[0, hd] if carry else s0_ref[c, hd]
            vn = us[hd][sl] - _bdot(ws[hd][sl], s_old)
            o_state[hd].append(_bdot(qgs[hd][sl], s_old))
            glast = gcl[hd][(c + 1) * period - 1:(c + 1) * period, :]
            s_new = s_old * jnp.exp(glast) + _bdot_tn(ks[hd][sl] * jnp.exp(glast - gcl[hd][sl]), vn)
            if carry:
                s_ref[0, hd] = s_new
            else:
                s_ref[c, hd] = s_new
            v_new[hd].append(vn)
    for hd in heads:
        o = jnp.concatenate(o_state[hd], axis=0) + _bdot(qks[hd], jnp.concatenate(v_new[hd], axis=0))
        o = _rms(o, og_ref[...])
        o_ref[:, hd * DN_DV:(hd + 1) * DN_DV] = o * _silu(z_ref[:, hd * DN_DV:(hd + 1) * DN_DV])


_DELTA_ROWS = 256
_DELTA_SEQS = 16


def _delta_mixer(qkv, z, ba, c0, s0, b, t, conv_w, a_log, dt_bias, o_norm_g):
    al = jnp.zeros((1, LANES), F32).at[0, DN_HEADS:2 * DN_HEADS].set(a_log)
    dt = jnp.zeros((1, LANES), F32).at[0, DN_HEADS:2 * DN_HEADS].set(dt_bias)
    carry = t >= DN_CHUNK
    if carry:
        period, rows, spb = DN_CHUNK, _DELTA_ROWS, 1
        nj = t // rows
        grid = (b, nj)
        tok = lambda i, j: (i * nj + j, 0)
        st = lambda i, j: (i, 0, 0, 0)
        const = lambda i, j: (0, 0)
        prev, prev_spec = c0, pl.BlockSpec((1, DN_CONV - 1, QKV_DIM), lambda i, j: (i, 0, 0))
        scratch = [pltpu.VMEM((SUBLANES, QKV_DIM), F32)]
        sem = ("parallel", "arbitrary")
    else:
        period, spb = t, _DELTA_SEQS
        rows = spb * period
        grid = (b // spb,)
        tok = lambda i: (i, 0)
        st = lambda i: (i, 0, 0, 0)
        const = lambda i: (0, 0)
        hist = jnp.zeros((b, period, QKV_DIM), F32).at[:, period - (DN_CONV - 1):].set(c0)
        prev = jnp.roll(hist.reshape(b // spb, rows, QKV_DIM), -period, axis=1).reshape(b * period, QKV_DIM)
        prev_spec = pl.BlockSpec((rows, QKV_DIM), tok)
        scratch = []
        sem = ("parallel",)
    state = pl.BlockSpec((spb, DN_HEADS, DN_DK, DN_DV), st)
    return pl.pallas_call(
        functools.partial(_delta_kernel, period=period, carry=carry),
        grid=grid,
        in_specs=[
            pl.BlockSpec((rows, QKV_DIM), tok),
            pl.BlockSpec((rows, DN_V), tok),
            pl.BlockSpec((rows, LANES), tok),
            prev_spec,
            state,
            pl.BlockSpec((DN_CONV, QKV_DIM), const),
            pl.BlockSpec((1, LANES), const),
            pl.BlockSpec((1, LANES), const),
            pl.BlockSpec((1, DN_DV), const),
        ],
        out_specs=[pl.BlockSpec((rows, DN_V), tok), state],
        out_shape=[
            jax.ShapeDtypeStruct((b * t, DN_V), F32),
            jax.ShapeDtypeStruct((b, DN_HEADS, DN_DK, DN_DV), F32),
        ],
        scratch_shapes=scratch,
        compiler_params=_cparams(sem, 48),
        name="delta_mixer",
    )(qkv, z, ba, prev, s0, conv_w, al, dt, o_norm_g.reshape(1, DN_DV))


def _sgu_kernel(gu_ref, gv_ref, w_ref, b_ref, lg_ref, lb_ref, o_ref, *rest, period):
    vv_ref = rest[0] if rest else None
    n = gu_ref.shape[0]
    row = lax.broadcasted_iota(I32, (n, n), 0)
    col = lax.broadcasted_iota(I32, (n, n), 1)
    keep = row >= col
    if period < n:
        keep = keep & ((row // period) == (col // period))
    for g in range(SG_GROUPS):
        sl = slice(g * SG_CH, (g + 1) * SG_CH)
        u = _gelu(gu_ref[:, sl])
        x = _gelu(gv_ref[:, sl])
        mu = jnp.mean(x, axis=-1, keepdims=True)
        xc = x - mu
        var = jnp.mean(xc * xc, axis=-1, keepdims=True)
        vv = xc * lax.rsqrt(var + EPS) * lg_ref[:, sl] + lb_ref[:, sl]
        if vv_ref is not None:
            vv_ref[:, sl] = vv
        wm = jnp.where(keep, w_ref[g], 0.0).astype(BF16)
        mix = jnp.dot(wm, vv.astype(BF16), preferred_element_type=F32) + b_ref[g]
        o_ref[:, sl] = u * mix


def _sgu(gu, gv, w, bias, ln_g, ln_b, period, emit_vv):
    n_tok = gu.shape[0]
    blk = pl.BlockSpec((SG_CHUNK, SG_WIDTH), lambda i: (i, 0))
    out_shape = [jax.ShapeDtypeStruct((n_tok, SG_WIDTH), F32)]
    out_specs = [blk]
    if emit_vv:
        out_shape.append(jax.ShapeDtypeStruct((n_tok, SG_WIDTH), F32))
        out_specs.append(blk)
    return pl.pallas_call(
        functools.partial(_sgu_kernel, period=period),
        grid=(n_tok // SG_CHUNK,),
        in_specs=[
            blk, blk,
            pl.BlockSpec((SG_GROUPS, SG_CHUNK, SG_CHUNK), lambda i: (0, 0, 0)),
            pl.BlockSpec((SG_GROUPS, SG_CHUNK, SG_CH), lambda i: (0, 0, 0)),
            pl.BlockSpec((1, SG_WIDTH), lambda i: (0, 0)),
            pl.BlockSpec((1, SG_WIDTH), lambda i: (0, 0)),
        ],
        out_specs=out_specs,
        out_shape=out_shape,
        compiler_params=_cparams(("parallel",)),
        name="sgu",
    )(gu, gv, w, bias, ln_g.reshape(1, SG_WIDTH), ln_b.reshape(1, SG_WIDTH))


def _mem_attn_kernel(q_ref, k_ref, v_ref, o_ref):
    for hd in range(MEM_HEADS):
        sl = slice(hd * MEM_HD, (hd + 1) * MEM_HD)
        q = q_ref[0, :, sl].astype(BF16)
        k = k_ref[0, :, sl].astype(BF16)
        v = v_ref[0, :, sl].astype(BF16)
        s = lax.dot_general(q, k, (((1,), (1,)), ((), ())), preferred_element_type=F32) * (MEM_HD ** -0.5)
        m = jnp.max(s, axis=-1, keepdims=True)
        e = jnp.exp(s - m)
        p = e / jnp.sum(e, axis=-1, keepdims=True)
        o_ref[0, :, sl] = jnp.dot(p.astype(BF16), v, preferred_element_type=F32)


def _mem_attn_cache_kernel(q_ref, k_ref, v_ref, o_ref):
    t = q_ref.shape[1]
    q_all = jnp.concatenate([q_ref[0, :, hd * MEM_HD:(hd + 1) * MEM_HD] for hd in range(MEM_HEADS)], axis=0)
    k2 = k_ref[...].reshape(MEM_LEN * MEM_HEADS, MEM_HD).astype(BF16)
    v2 = v_ref[...].reshape(MEM_LEN * MEM_HEADS, MEM_HD).astype(BF16)
    s = lax.dot_general(q_all.astype(BF16), k2, (((1,), (1,)), ((), ())),
                        preferred_element_type=F32) * (MEM_HD ** -0.5)
    q_head = lax.broadcasted_iota(I32, s.shape, 0) // t
    k_head = lax.broadcasted_iota(I32, s.shape, 1) % MEM_HEADS
    s = jnp.where(q_head == k_head, s, -jnp.inf)
    m = jnp.max(s, axis=-1, keepdims=True)
    e = jnp.exp(s - m)
    p = e / jnp.sum(e, axis=-1, keepdims=True)
    out = jnp.dot(p.astype(BF16), v2, preferred_element_type=F32)
    for hd in range(MEM_HEADS):
        o_ref[0, :, hd * MEM_HD:(hd + 1) * MEM_HD] = out[hd * t:(hd + 1) * t, :]


def _mem_attn(q, mk, mv, layer=None):
    b, t, d = q.shape
    tq = min(t, 512)
    if layer is None:
        kv = pl.BlockSpec((1, MEM_LEN, d), lambda i, j: (i, 0, 0))
        body = _mem_attn_kernel
    else:
        kv = pl.BlockSpec((None, None, MEM_LEN, MEM_HEADS, MEM_HD), lambda i, j: (layer, i, 0, 0, 0))
        body = _mem_attn_cache_kernel
    return pl.pallas_call(
        body,
        grid=(b, t // tq),
        in_specs=[pl.BlockSpec((1, tq, d), lambda i, j: (i, j, 0)), kv, kv],
        out_specs=pl.BlockSpec((1, tq, d), lambda i, j: (i, j, 0)),
        out_shape=jax.ShapeDtypeStruct((b, t, d), F32),
        compiler_params=_cparams(("parallel", "parallel")),
        name="mem_attn",
    )(q, mk, mv)


def _attn_layer_kernel(h_ref, g_ref, wq_ref, k_ref, v_ref, wo_ref, o_ref):
    x = h_ref[...]
    q = jnp.dot(_rms(x, g_ref[...]).astype(BF16), wq_ref[...], preferred_element_type=F32)
    heads = []
    for hd in range(MEM_HEADS):
        sl = slice(hd * MEM_HD, (hd + 1) * MEM_HD)
        k = k_ref[0, :, sl].astype(BF16)
        v = v_ref[0, :, sl].astype(BF16)
        s = lax.dot_general(q[:, sl].astype(BF16), k, (((1,), (1,)), ((), ())),
                            preferred_element_type=F32) * (MEM_HD ** -0.5)
        m = jnp.max(s, axis=-1, keepdims=True)
        e = jnp.exp(s - m)
        p = e / jnp.sum(e, axis=-1, keepdims=True)
        heads.append(jnp.dot(p.astype(BF16), v, preferred_element_type=F32))
    att = jnp.concatenate(heads, axis=-1)
    o_ref[...] = x + jnp.dot(att.astype(BF16), wo_ref[...], preferred_element_type=F32)


def _attn_layer(h, g, wq, mk, mv, wo, b, t):
    d = h.shape[1]
    tq = 512
    nj = t // tq
    row = pl.BlockSpec((tq, d), lambda i, j: (i * nj + j, 0))
    kv = pl.BlockSpec((1, MEM_LEN, d), lambda i, j: (i, 0, 0))
    full = pl.BlockSpec((d, d), lambda i, j: (0, 0))
    return pl.pallas_call(
        _attn_layer_kernel,
        grid=(b, nj),
        in_specs=[row, pl.BlockSpec((1, d), lambda i, j: (0, 0)), full, kv, kv, full],
        out_specs=row,
        out_shape=jax.ShapeDtypeStruct(h.shape, F32),
        compiler_params=_cparams(("parallel", "parallel"), 48),
        name="attn_layer",
    )(h, g.reshape(1, d), wq, mk, mv, wo)


def _top_rows(s, n_top, rank=None, payload=None):
    if rank is None:
        rank = lax.broadcasted_iota(I32, s.shape, 0)
    big = jnp.int32(2 ** 30)
    vals, ids, pays = [], [], []
    for _ in range(n_top):
        m = jnp.max(s, axis=0, keepdims=True)
        i = jnp.min(jnp.where(s == m, rank, big), axis=0, keepdims=True)
        hit = rank == i
        vals.append(m)
        ids.append(i)
        if payload is not None:
            pays.append(jnp.max(jnp.where(hit, payload, -1), axis=0, keepdims=True))
        s = jnp.where(hit, -jnp.inf, s)
    out = [jnp.concatenate(vals, axis=0), jnp.concatenate(ids, axis=0)]
    if payload is not None:
        out.append(jnp.concatenate(pays, axis=0))
    return out


_PAIR_GROUPS = ((0, 0), (0, 8), (1, 0), (2, 0), (3, 0), (4, 0), (5, 0), (6, 0), (7, 0))


def _pair_candidates(sv0, si0, sv1, si1):
    sub = lax.broadcasted_iota(I32, (SUBLANES,) + sv0.shape[1:], 0)
    cand, flat, eid = [], [], []
    for a, b0 in _PAIR_GROUPS:
        cand.append(sv0[a:a + 1, :] + sv1[b0:b0 + SUBLANES, :])
        flat.append(a * PEER_TOPK + b0 + sub)
        eid.append(si0[a:a + 1, :] * PEER_NKEYS + si1[b0:b0 + SUBLANES, :])
    cand.append(sv0[SUBLANES:, :] + sv1[0:1, :])
    flat.append((SUBLANES + sub) * PEER_TOPK)
    eid.append(si0[SUBLANES:, :] * PEER_NKEYS + si1[0:1, :])
    return jnp.concatenate(cand, axis=0), jnp.concatenate(flat, axis=0), jnp.concatenate(eid, axis=0)


def _peer_route_kernel(x_ref, g_ref, wq_ref, keys_ref,
                       xn_ref, roff_ref, sh_ref, gate_ref, e_scr, g_scr):
    n = _rms(x_ref[...], g_ref[...])
    for s in range(ROW_TILES):
        xn_ref[:, s, :] = n[:, s * LANES:(s + 1) * LANES]
    q = jnp.dot(n.astype(BF16), wq_ref[...], preferred_element_type=F32)
    for hd in range(PEER_HEADS):
        qh = q[:, hd * PEER_DKEY:(hd + 1) * PEER_DKEY].astype(BF16)
        tops = []
        for p in range(2):
            s_t = lax.dot_general(keys_ref[hd, p], qh, (((1,), (1,)), ((), ())),
                                  preferred_element_type=F32)
            tops.append(_top_rows(s_t, PEER_TOPK))
        (sv0, si0), (sv1, si1) = tops
        cand, flat, ecand = _pair_candidates(sv0, si0, sv1, si1)
        cv, _, ce = _top_rows(cand, PEER_TOPK, rank=flat, payload=ecand)
        ex = jnp.exp(cv - cv[0:1, :])
        gate = ex / jnp.sum(ex, axis=0, keepdims=True)
        e_scr[hd * PEER_TOPK:(hd + 1) * PEER_TOPK, :] = ce
        g_scr[hd * PEER_TOPK:(hd + 1) * PEER_TOPK, :] = gate
    e_t = pltpu.bitcast(pltpu.bitcast(e_scr[...], F32).T, I32)
    hi = e_t >= HALF_EXPERTS
    roff_ref[...] = jnp.where(hi, e_t - HALF_EXPERTS, e_t) * ROW_TILES
    sh_ref[...] = jnp.where(hi, 0, 16)
    gate_ref[...] = g_scr[...].T


def _peer_route(h, g, wq, keys_pad):
    n_tok, d = h.shape
    tb = LANES
    row = pl.BlockSpec((tb, d), lambda i: (i, 0))
    slot = pl.BlockSpec((tb, PEER_SLOTS), lambda i: (i, 0))
    return pl.pallas_call(
        _peer_route_kernel,
        grid=(n_tok // tb,),
        in_specs=[
            row,
            pl.BlockSpec((1, d), lambda i: (0, 0)),
            pl.BlockSpec(wq.shape, lambda i: (0, 0)),
            pl.BlockSpec(keys_pad.shape, lambda i: (0, 0, 0, 0)),
        ],
        out_specs=[pl.BlockSpec((tb, ROW_TILES, LANES), lambda i: (i, 0, 0)), slot, slot, slot],
        out_shape=[
            jax.ShapeDtypeStruct((n_tok, ROW_TILES, LANES), F32),
            jax.ShapeDtypeStruct((n_tok, PEER_SLOTS), I32),
            jax.ShapeDtypeStruct((n_tok, PEER_SLOTS), I32),
            jax.ShapeDtypeStruct((n_tok, PEER_SLOTS), F32),
        ],
        scratch_shapes=[pltpu.VMEM((PEER_SLOTS, tb), I32), pltpu.VMEM((PEER_SLOTS, tb), F32)],
        compiler_params=_cparams(("parallel",), 48),
        name="peer_route",
    )(h, g.reshape(1, d), wq, keys_pad)


def _expert_tile(tab_ref, roff, shv):
    w = tab_ref[pl.ds(pl.multiple_of(roff, SUBLANES), SUBLANES), :]
    return pltpu.bitcast(w << shv, F32)


def _fold_rows(ps):
    sub = lax.broadcasted_iota(I32, (SUBLANES, LANES), 0)
    dist = SUBLANES // 2
    while len(ps) > 1:
        lo = (sub % (2 * dist)) < dist
        half = len(ps) // 2
        nxt = []
        for a in range(half):
            x, y = ps[a], ps[a + half]
            nxt.append(jnp.where(lo, x, pltpu.roll(y, dist, 0))
                       + jnp.where(lo, pltpu.roll(x, SUBLANES - dist, 0), y))
        ps = nxt
        dist //= 2
    return ps[0]


def _slot_columns(rows8, dst_ref):
    for u in range(SUBLANES):
        dst_ref[u] = jnp.broadcast_to(rows8[u:u + 1, :], (PEER_SLOTS, LANES)).T


def _bcast_row(ref, u, k):
    return jnp.broadcast_to(ref[u, k:k + 1, :], (SUBLANES, LANES))


def _peer_token_loop(tb, prep, token_group):
    n_pairs = tb // (2 * SUBLANES)
    prep(0, 0)

    def pair(i, carry):
        g0 = 2 * i
        prep(g0 + 1, 1)
        token_group(g0, 0)
        prep(jnp.minimum(g0 + 2, 2 * n_pairs - 1), 0)
        token_group(g0 + 1, 1)
        return carry

    lax.fori_loop(0, n_pairs, pair, 0)


def _peer_up_kernel(roff_ref, sh_ref, gate_ref, x_ref, tab_ref, c_ref, shs_a, shs_b):
    tb = sh_ref.shape[0]
    shs = (shs_a, shs_b)

    def prep(grp, par):
        r0 = pl.multiple_of(grp * SUBLANES, SUBLANES)
        _slot_columns(pltpu.bitcast(sh_ref[pl.ds(r0, SUBLANES), :], F32), shs[par])

    def token_group(grp, par):
        rows = []
        for u in range(SUBLANES):
            t = grp * SUBLANES + u
            x_t = x_ref[t]
            tiles = []
            for g in range(PEER_SLOTS // SUBLANES):
                ps = []
                for jj in range(SUBLANES):
                    k = g * SUBLANES + jj
                    shv = pltpu.bitcast(_bcast_row(shs[par], u, k), U32)
                    ps.append(_expert_tile(tab_ref, roff_ref[t, k], shv) * x_t)
                tiles.append(_fold_rows(ps))
            r = jnp.concatenate(tiles, axis=0)
            rows.append(jnp.sum(r.T, axis=0, keepdims=True))
        out = pl.ds(pl.multiple_of(grp * SUBLANES, SUBLANES), SUBLANES)
        c_ref[out, :] = gate_ref[out, :] * _gelu(jnp.concatenate(rows, axis=0))

    _peer_token_loop(tb, prep, token_group)


def _peer_down_kernel(roff_hbm, sh_ref, sh_next, c_ref, c_next, tab_ref, o_ref,
                      idx_a, idx_b, sh_a, sh_b, c_a, c_b, sem):
    i = pl.program_id(0)
    n = pl.num_programs(0)
    tb = sh_ref.shape[0]
    th = tb // 2
    idx = (idx_a, idx_b)

    def idx_copy(step, half):
        row0 = pl.multiple_of(step * tb + half * th, th)
        return pltpu.make_async_copy(roff_hbm.at[pl.ds(row0, th)], idx[half], sem.at[half])

    def build(src_sh, src_c, row, sh_t, c_t, u, after=None):
        sh_row = src_sh[row:row + 1, :]
        c_row = src_c[row:row + 1, :]
        if after is not None:
            zero = lax.shift_right_logical(pltpu.bitcast(after[0:1, :], U32), jnp.uint32(32))
            sh_row = sh_row + pltpu.bitcast(zero, I32)
            c_row = c_row + pltpu.bitcast(zero, F32)
        sh_t[u] = jnp.broadcast_to(pltpu.bitcast(sh_row, F32), (PEER_SLOTS, LANES)).T
        c_t[u] = jnp.broadcast_to(c_row, (PEER_SLOTS, LANES)).T

    def consume(half, sh_t, c_t, u):
        accs = [jnp.zeros((SUBLANES, LANES), F32) for _ in range(2)]
        for k in range(PEER_SLOTS):
            shv = pltpu.bitcast(_bcast_row(sh_t, u, k), U32)
            accs[k % 2] = accs[k % 2] + _expert_tile(tab_ref, idx[half][u, k], shv) * _bcast_row(c_t, u, k)
        acc = accs[0] + accs[1]
        o_ref[half * th + u] = acc
        return acc

    def half_step(half, use, fill, src_sh, src_c, row0):
        @pl.when(i >= 0)
        def _():
            acc = None
            for u in range(th):
                build(src_sh, src_c, row0 + u, fill[0], fill[1], u, after=acc)
                acc = consume(half, use[0], use[1], u)

    tiles = ((sh_a, c_a), (sh_b, c_b))

    @pl.when(i == 0)
    def _():
        idx_copy(0, 0).start()
        idx_copy(0, 1).start()
        for u in range(th):
            build(sh_ref, c_ref, u, sh_a, c_a, u)

    idx_copy(i, 0).wait()
    half_step(0, tiles[0], tiles[1], sh_ref, c_ref, th)

    @pl.when(i + 1 < n)
    def _():
        idx_copy(i + 1, 0).start()

    idx_copy(i, 1).wait()
    half_step(1, tiles[1], tiles[0], sh_next, c_next, 0)

    @pl.when(i + 1 < n)
    def _():
        idx_copy(i + 1, 1).start()


_PEER_TB = 64
_PEER_DOWN_TB = 64
_PEER_TABLE_MIB = HALF_EXPERTS * D_MODEL * 4 // 2 ** 20
_PEER_VMEM_MIB = min(_PEER_TABLE_MIB + 20, VMEM_MIB_V7X - 8)
_COL_TILES = pltpu.VMEM((SUBLANES, PEER_SLOTS, LANES), F32)


def _table_spec(tab):
    return pl.BlockSpec(tab.shape, lambda i: (0, 0), pipeline_mode=pl.Buffered(1))


def _peer_up(roff, sh, gate, x_tiles, tab):
    n_tok = roff.shape[0]
    tb = _PEER_TB
    slot = pl.BlockSpec((tb, PEER_SLOTS), lambda i: (i, 0))
    return pl.pallas_call(
        _peer_up_kernel,
        grid=(n_tok // tb,),
        in_specs=[
            pl.BlockSpec((tb, PEER_SLOTS), lambda i: (i, 0), memory_space=pltpu.SMEM),
            slot, slot,
            pl.BlockSpec((tb, ROW_TILES, LANES), lambda i: (i, 0, 0)),
            _table_spec(tab),
        ],
        out_specs=slot,
        out_shape=jax.ShapeDtypeStruct((n_tok, PEER_SLOTS), F32),
        scratch_shapes=[_COL_TILES] * 2,
        compiler_params=_cparams(("arbitrary",), _PEER_VMEM_MIB),
        name="peer_up",
    )(roff, sh, gate, x_tiles, tab)


def _peer_down(roff, sh, coef, tab):
    n_tok = roff.shape[0]
    tb = _PEER_DOWN_TB
    th = tb // 2
    n_steps = n_tok // tb
    cur = pl.BlockSpec((tb, PEER_SLOTS), lambda i: (i, 0))
    nxt = pl.BlockSpec((tb, PEER_SLOTS), lambda i: (jnp.minimum(i + 1, n_steps - 1), 0))
    half_tiles = pltpu.VMEM((th, PEER_SLOTS, LANES), F32)
    return pl.pallas_call(
        _peer_down_kernel,
        grid=(n_steps,),
        in_specs=[pl.BlockSpec(memory_space=pl.ANY), cur, nxt, cur, nxt, _table_spec(tab)],
        out_specs=pl.BlockSpec((tb, ROW_TILES, LANES), lambda i: (i, 0, 0)),
        out_shape=jax.ShapeDtypeStruct((n_tok, ROW_TILES, LANES), F32),
        scratch_shapes=[pltpu.SMEM((th, PEER_SLOTS), I32)] * 2 + [half_tiles] * 4
        + [pltpu.SemaphoreType.DMA((2,))],
        compiler_params=_cparams(("arbitrary",), _PEER_VMEM_MIB),
        name="peer_down",
    )(roff, sh, sh, coef, coef, tab)


def _pack_kernel(lo_ref, hi_ref, o_ref):
    low = pltpu.bitcast(pltpu.bitcast(lo_ref[...].astype(BF16).astype(F32), U32) >> 16, I32)
    bits = pltpu.bitcast(hi_ref[...], I32)
    sign = bits & jnp.int32(-2 ** 31)
    mag = jnp.minimum(bits & jnp.int32(0x7FFFFFFF), jnp.int32(0x7F7F0000))
    top = jnp.maximum(mag + jnp.int32(0x8000) - low, 0) & jnp.int32(-65536)
    word = pltpu.bitcast(sign | top | low, U32)
    for s in range(ROW_TILES):
        o_ref[:, s, :] = word[:, s * LANES:(s + 1) * LANES]


def _pack_table(tabs, layer, rows=512):
    d = tabs.shape[2]
    n_blk = HALF_EXPERTS // rows
    packed = pl.pallas_call(
        _pack_kernel,
        grid=(n_blk,),
        in_specs=[pl.BlockSpec((None, rows, d), lambda i: (layer, i, 0)),
                  pl.BlockSpec((None, rows, d), lambda i: (layer, i + n_blk, 0))],
        out_specs=pl.BlockSpec((rows, ROW_TILES, LANES), lambda i: (i, 0, 0)),
        out_shape=jax.ShapeDtypeStruct((HALF_EXPERTS, ROW_TILES, LANES), U32),
        compiler_params=_cparams(("parallel",), 48),
        name="pack_table",
    )(tabs, tabs)
    return packed.reshape(HALF_EXPERTS * ROW_TILES, LANES)


def _peer(h, g, wq, keys_pad, u_tab, v_tab):
    xn, roff, sh, gate = _peer_route(h, g, wq, keys_pad)
    coef = _peer_up(roff, sh, gate, xn, u_tab)
    return _peer_down(roff, sh, coef, v_tab)


def _prep_layer(l, w_in, conv_w, a_log, dt_bias, o_norm_g, sg_ln_g, sg_ln_b, sg_w, sg_b, w_out,
                norm_mix_g, norm_mem_g, mem_norm_g, w_mq, w_mk, w_mv, w_mo, norm_ffn_g,
                peer_wq, peer_keys, peer_u, peer_v):
    o1 = QKV_DIM
    o2 = o1 + DN_V
    o4 = o2 + 2 * DN_HEADS
    o5 = o4 + SG_WIDTH
    wi = w_in[l]
    ba_cols = jnp.pad(wi[:, o2:o4], ((0, 0), (0, LANES - 2 * DN_HEADS)))
    w_in_r = jnp.concatenate([wi[:, :o2], wi[:, o4:], ba_cols], axis=1).astype(BF16)
    kz = jnp.zeros((PEER_HEADS, PEER_NKEYS, PEER_DKEY // 2), F32)
    keys_pad = jnp.stack([jnp.concatenate([peer_keys[l][:, 0], kz], axis=-1),
                          jnp.concatenate([kz, peer_keys[l][:, 1]], axis=-1)], axis=1).astype(BF16)
    return dict(
        w_in=w_in_r, conv_w=conv_w[l], a_log=a_log[l], dt_bias=dt_bias[l], o_norm_g=o_norm_g[l],
        sg_ln_g=sg_ln_g[l].reshape(-1), sg_ln_b=sg_ln_b[l].reshape(-1), sg_w=sg_w[l], sg_b=sg_b[l],
        w_out_a=w_out[l][:DN_V].astype(BF16), w_out_b=w_out[l][DN_V:].astype(BF16),
        norm_mix_g=norm_mix_g[l], norm_mem_g=norm_mem_g[l], mem_norm_g=mem_norm_g[l],
        w_mq=w_mq[l].astype(BF16), w_mkv=jnp.concatenate([w_mk[l], w_mv[l]], axis=1).astype(BF16),
        w_mo=w_mo[l].astype(BF16), norm_ffn_g=norm_ffn_g[l], peer_wq=peer_wq[l].astype(BF16),
        keys_pad=keys_pad, u_tab=_pack_table(peer_u, l), v_tab=_pack_table(peer_v, l),
    )


_IN_WIDTHS = (QKV_DIM, DN_V, SG_WIDTH, SG_WIDTH, LANES)


def _layer(h, delta, b, t, mk, mv, s0, c0, p, emit_vv, cache_layer=None):
    n_tok = b * t
    outs = _norm_proj(h, p["norm_mix_g"], p["w_in"], _IN_WIDTHS, delta=delta)
    if delta is not None:
        h, outs = outs[0], outs[1:]
    qkv, z, gu, gv, ba = outs
    o_a, s_new = _delta_mixer(qkv, z, ba, c0, s0, b, t, p["conv_w"], p["a_log"], p["dt_bias"], p["o_norm_g"])
    conv_new = qkv.reshape(b, t, QKV_DIM)[:, t - (DN_CONV - 1):, :]
    period = min(t, SG_CHUNK)
    reps = SG_CHUNK // period
    sg_w = jnp.tile(p["sg_w"][:, :period, :period], (1, reps, reps))
    sg_bias = jnp.broadcast_to(jnp.tile(p["sg_b"][:, :period], (1, reps))[:, :, None],
                               (SG_GROUPS, SG_CHUNK, SG_CH))
    sg_out = _sgu(gu, gv, sg_w, sg_bias, p["sg_ln_g"], p["sg_ln_b"], period, emit_vv)
    o_b = sg_out[0]
    vv = sg_out[1] if emit_vv else None
    h = _proj_residual(h, [o_a.reshape(n_tok, DN_V), o_b], [p["w_out_a"], p["w_out_b"]])
    if cache_layer is None and t % 512 == 0:
        h = _attn_layer(h, p["norm_mem_g"], p["w_mq"], mk, mv, p["w_mo"], b, t)
    else:
        (q,) = _norm_proj(h, p["norm_mem_g"], p["w_mq"], (D_MODEL,))
        att = _mem_attn(q.reshape(b, t, D_MODEL), mk, mv, layer=cache_layer)
        h = _proj_residual(h, [att.reshape(n_tok, D_MODEL)], [p["w_mo"]])
    return h, s_new, conv_new, vv


def kernel(x_prompt, x_sample, state_delta, state_conv, cache_mem_k, cache_mem_v, mem_prompt, w_in, conv_w, a_log, dt_bias, o_norm_g, sg_ln_g, sg_ln_b, sg_w, sg_b, w_out, norm_mix_g, norm_mem_g, mem_norm_g, w_mq, w_mk, w_mv, w_mo, norm_ffn_g, peer_wq, peer_keys, peer_u, peer_v, final_norm_g):
    depth = w_in.shape[0]
    bp, tp, d = x_prompt.shape
    bs, ts, _ = x_sample.shape
    hp = x_prompt.reshape(bp * tp, d)
    hs = x_sample.reshape(bs * ts, d)
    dp = ds = None
    mem_flat = mem_prompt.reshape(bp * MEM_LEN, d)
    sd_p, sc_p, mk_p, mv_p, sd_s, sc_s, vr_s = [], [], [], [], [], [], []
    for l in range(depth):
        p = _prep_layer(l, w_in, conv_w, a_log, dt_bias, o_norm_g, sg_ln_g, sg_ln_b, sg_w, sg_b, w_out,
                        norm_mix_g, norm_mem_g, mem_norm_g, w_mq, w_mk, w_mv, w_mo, norm_ffn_g,
                        peer_wq, peer_keys, peer_u, peer_v)
        mk, mv = _norm_proj(mem_flat, p["mem_norm_g"], p["w_mkv"], (D_MODEL, D_MODEL))
        mk = mk.reshape(bp, MEM_LEN, d)
        mv = mv.reshape(bp, MEM_LEN, d)
        s0 = jnp.zeros((bp, DN_HEADS, DN_DK, DN_DV), F32)
        c0 = jnp.zeros((bp, DN_CONV - 1, QKV_DIM), F32)
        hp, s_p, c_p, _ = _layer(hp, dp, bp, tp, mk, mv, s0, c0, p, False)
        hs, s_s, c_s, vv = _layer(hs, ds, bs, ts, cache_mem_k, cache_mem_v, state_delta[l], state_conv[l], p, True,
                                  cache_layer=l)
        dp = _peer(hp, p["norm_ffn_g"], p["peer_wq"], p["keys_pad"], p["u_tab"], p["v_tab"])
        ds = _peer(hs, p["norm_ffn_g"], p["peer_wq"], p["keys_pad"], p["u_tab"], p["v_tab"])
        sd_p.append(s_p)
        sc_p.append(c_p)
        mk_p.append(mk.reshape(bp, MEM_LEN, MEM_HEADS, MEM_HD))
        mv_p.append(mv.reshape(bp, MEM_LEN, MEM_HEADS, MEM_HD))
        sd_s.append(s_s)
        sc_s.append(c_s)
        vr_s.append(vv.reshape(bs, ts, SG_WIDTH))
    y_prompt = _add_norm(hp, dp, final_norm_g).reshape(bp, tp, d)
    y_sample = _add_norm(hs, ds, final_norm_g).reshape(bs, ts, d)
    return (y_prompt, y_sample, jnp.stack(sd_p), jnp.stack(sc_p), jnp.stack(mk_p), jnp.stack(mv_p),
            jnp.stack(sd_s), jnp.stack(sc_s), jnp.stack(vr_s))
```

```python
import functools
import math

import jax
import jax.numpy as jnp
from jax import lax
from jax.experimental import pallas as pl
from jax.experimental.pallas import tpu as pltpu

F32 = jnp.float32
BF16 = jnp.bfloat16
I32 = jnp.int32
U32 = jnp.uint32

D_MODEL = 1024
DN_HEADS = 4
DN_DK = 128
DN_DV = 128
DN_QK = DN_HEADS * DN_DK
DN_V = DN_HEADS * DN_DV
DN_CONV = 4
DN_CHUNK = 64
QKV_DIM = 2 * DN_QK + DN_V
SG_GROUPS = 4
SG_CH = 128
SG_WIDTH = SG_GROUPS * SG_CH
SG_CHUNK = 128
MEM_LEN = 256
MEM_HEADS = 4
MEM_HD = D_MODEL // MEM_HEADS
PEER_HEADS = 8
PEER_NKEYS = 128
PEER_NEXP = PEER_NKEYS * PEER_NKEYS
PEER_DKEY = 128
PEER_TOPK = 16
PEER_SLOTS = PEER_HEADS * PEER_TOPK
EPS = 1e-6

SUBLANES = 8
LANES = 128
ROW_TILES = D_MODEL // LANES
VMEM_MIB_V7X = 64
HALF_EXPERTS = PEER_NEXP // 2

_HI = lax.Precision.HIGHEST


def _cparams(sem, vmem_mib=None):
    kw = dict(dimension_semantics=sem)
    if vmem_mib is not None:
        kw["vmem_limit_bytes"] = vmem_mib * 1024 * 1024
    return pltpu.CompilerParams(**kw)


def _rms(x, g):
    return x * lax.rsqrt(jnp.mean(x * x, axis=-1, keepdims=True) + EPS) * g


def _gelu(x):
    return jax.nn.gelu(x, approximate=True)


def _sigmoid(x):
    return 1.0 / (1.0 + jnp.exp(-x))


def _silu(x):
    return x * _sigmoid(x)


def _softplus(x):
    return jnp.maximum(x, 0.0) + jnp.log(1.0 + jnp.exp(-jnp.abs(x)))


def _rows_from_tiles(t_ref):
    return jnp.concatenate([t_ref[:, s, :] for s in range(ROW_TILES)], axis=-1)


def _norm_proj_kernel(*refs, widths, has_delta, emit_h):
    it = iter(refs)
    x_ref = next(it)
    d_ref = next(it) if has_delta else None
    g_ref = next(it)
    w_ref = next(it)
    outs = list(it)
    x = x_ref[...]
    if has_delta:
        x = x + _rows_from_tiles(d_ref)
    if emit_h:
        outs[0][...] = x
        outs = outs[1:]
    n = _rms(x, g_ref[...]).astype(BF16)
    y = jnp.dot(n, w_ref[...], preferred_element_type=F32)
    c = 0
    for o, wd in zip(outs, widths):
        o[...] = y[:, c:c + wd]
        c += wd


def _norm_proj(x, g, w, widths, delta=None, tm=512):
    n_tok, d = x.shape
    tm = min(tm, n_tok)
    has_delta = delta is not None
    row = pl.BlockSpec((tm, d), lambda i: (i, 0))
    tiles = pl.BlockSpec((tm, ROW_TILES, LANES), lambda i: (i, 0, 0))
    in_specs = [row] + ([tiles] if has_delta else []) + [
        pl.BlockSpec((1, d), lambda i: (0, 0)),
        pl.BlockSpec(w.shape, lambda i: (0, 0)),
    ]
    out_shape, out_specs = [], []
    if has_delta:
        out_shape.append(jax.ShapeDtypeStruct((n_tok, d), F32))
        out_specs.append(row)
    for wd in widths:
        out_shape.append(jax.ShapeDtypeStruct((n_tok, wd), F32))
        out_specs.append(pl.BlockSpec((tm, wd), lambda i: (i, 0)))
    args = [x] + ([delta] if has_delta else []) + [g.reshape(1, d), w]
    return pl.pallas_call(
        functools.partial(_norm_proj_kernel, widths=tuple(widths), has_delta=has_delta, emit_h=has_delta),
        grid=(n_tok // tm,),
        in_specs=in_specs,
        out_specs=out_specs,
        out_shape=out_shape,
        compiler_params=_cparams(("parallel",), 48),
        name="norm_proj",
    )(*args)


def _proj_res_kernel(*refs, n_in):
    h_ref = refs[0]
    a_refs = refs[1:1 + n_in]
    w_refs = refs[1 + n_in:1 + 2 * n_in]
    o_ref = refs[1 + 2 * n_in]
    acc = h_ref[...]
    for a, w in zip(a_refs, w_refs):
        acc = acc + jnp.dot(a[...].astype(BF16), w[...], preferred_element_type=F32)
    o_ref[...] = acc


def _proj_residual(h, acts, ws, tm=512):
    n_tok, d = h.shape
    tm = min(tm, n_tok)
    in_specs = [pl.BlockSpec((tm, d), lambda i: (i, 0))]
    in_specs += [pl.BlockSpec((tm, a.shape[1]), lambda i: (i, 0)) for a in acts]
    in_specs += [pl.BlockSpec(w.shape, lambda i: (0, 0)) for w in ws]
    return pl.pallas_call(
        functools.partial(_proj_res_kernel, n_in=len(acts)),
        grid=(n_tok // tm,),
        in_specs=in_specs,
        out_specs=pl.BlockSpec((tm, d), lambda i: (i, 0)),
        out_shape=jax.ShapeDtypeStruct((n_tok, d), F32),
        compiler_params=_cparams(("parallel",), 48),
        name="proj_residual",
    )(h, *acts, *ws)


def _add_norm_kernel(x_ref, d_ref, g_ref, o_ref):
    o_ref[...] = _rms(x_ref[...] + _rows_from_tiles(d_ref), g_ref[...])


def _add_norm(x, delta, g, tm=512):
    n_tok, d = x.shape
    tm = min(tm, n_tok)
    row = pl.BlockSpec((tm, d), lambda i: (i, 0))
    tiles = pl.BlockSpec((tm, ROW_TILES, LANES), lambda i: (i, 0, 0))
    return pl.pallas_call(
        _add_norm_kernel,
        grid=(n_tok // tm,),
        in_specs=[row, tiles, pl.BlockSpec((1, d), lambda i: (0, 0))],
        out_specs=row,
        out_shape=jax.ShapeDtypeStruct((n_tok, d), F32),
        compiler_params=_cparams(("parallel",)),
        name="add_norm",
    )(x, delta, g.reshape(1, d))


def _bdot(a, b):
    return jnp.dot(a.astype(BF16), b.astype(BF16), preferred_element_type=F32)


def _bdot_nt(a, b):
    return lax.dot_general(a.astype(BF16), b.astype(BF16), (((1,), (1,)), ((), ())), preferred_element_type=F32)


def _bdot_tn(a, b):
    return lax.dot_general(a.astype(BF16), b.astype(BF16), (((0,), (0,)), ((), ())), preferred_element_type=F32)


def _delta_kernel(qkv_ref, z_ref, ba_ref, prev_ref, s0_ref, cw_ref, al_ref, dt_ref, og_ref,
                  o_ref, s_ref, *scratch, period, carry):
    rows = qkv_ref.shape[0]
    n_sub = rows // period
    x = qkv_ref[...]
    if carry:
        tail_ref = scratch[0]
        j = pl.program_id(1)

        @pl.when(j == 0)
        def _():
            s_ref[0] = s0_ref[0]
            tail_ref[...] = jnp.zeros_like(tail_ref)
            tail_ref[SUBLANES - (DN_CONV - 1):, :] = prev_ref[0]

        xx = jnp.concatenate([tail_ref[...], x], axis=0)
        acc = jnp.zeros((rows, QKV_DIM), F32)
        for jj in range(DN_CONV):
            off = SUBLANES - (DN_CONV - 1) + jj
            acc = acc + xx[off:off + rows, :] * cw_ref[jj:jj + 1, :]
        tail_ref[...] = x[rows - SUBLANES:, :]
    else:
        pos = lax.broadcasted_iota(I32, (rows, QKV_DIM), 0) % period
        hist = prev_ref[...]
        acc = x * cw_ref[DN_CONV - 1:DN_CONV, :]
        for d in range(1, DN_CONV):
            shifted = jnp.where(pos >= d, pltpu.roll(x, d, 0), pltpu.roll(hist, d, 0))
            acc = acc + shifted * cw_ref[DN_CONV - 1 - d:DN_CONV - d, :]
    y = _silu(acc)
    ba = ba_ref[...]
    lane = lax.broadcasted_iota(I32, ba.shape, 1)
    bg = jnp.where(lane < DN_HEADS, _sigmoid(ba), -jnp.exp(al_ref[...]) * _softplus(ba + dt_ref[...]))

    row = lax.broadcasted_iota(I32, (rows, rows), 0)
    col = lax.broadcasted_iota(I32, (rows, rows), 1)
    same = (row // period) == (col // period)
    causal = same & (row >= col)
    strict = same & (row > col)
    eye = (row == col).astype(F32)
    gcs = jnp.dot(causal.astype(F32), bg, precision=_HI, preferred_element_type=F32)
    n_fac = int(math.log2(period)) - 1

    heads = range(DN_HEADS)
    qs, ks, vs, betas, gcl, decays, kbs = [], [], [], [], [], [], []
    for hd in heads:
        q = y[:, hd * DN_DK:(hd + 1) * DN_DK]
        k = y[:, DN_QK + hd * DN_DK:DN_QK + (hd + 1) * DN_DK]
        qs.append(q * lax.rsqrt(jnp.sum(q * q, axis=-1, keepdims=True) + EPS) * (DN_DK ** -0.5))
        ks.append(k * lax.rsqrt(jnp.sum(k * k, axis=-1, keepdims=True) + EPS))
        vs.append(y[:, 2 * DN_QK + hd * DN_DV:2 * DN_QK + (hd + 1) * DN_DV])
        betas.append(bg[:, hd:hd + 1])
        gc = gcs[:, DN_HEADS + hd:DN_HEADS + hd + 1]
        gc_cols = jnp.broadcast_to(gc, (rows, rows))
        decays.append(jnp.exp(jnp.where(causal, gc_cols - gc_cols.T, -jnp.inf)))
        gcl.append(gc)
        kbs.append(ks[hd] * betas[hd])
    a_s = [jnp.where(strict, _bdot_nt(kbs[hd], ks[hd]) * decays[hd], 0.0) for hd in heads]
    tinv = [eye - a for a in a_s]
    pw = [_bdot(a, a) for a in a_s]
    for f in range(n_fac):
        tinv = [t + _bdot(t, p) for t, p in zip(tinv, pw)]
        if f + 1 < n_fac:
            pw = [_bdot(p, p) for p in pw]
    us = [_bdot(tinv[hd], vs[hd] * betas[hd]) for hd in heads]
    ws = [_bdot(tinv[hd], kbs[hd] * jnp.exp(gcl[hd])) for hd in heads]
    qks = [_bdot_nt(qs[hd], ks[hd]) * decays[hd] for hd in heads]
    qgs = [qs[hd] * jnp.exp(gcl[hd]) for hd in heads]
    v_new = [[] for _ in heads]
    o_state = [[] for _ in heads]
    for c in range(n_sub):
        sl = slice(c * period, (c + 1) * period)
        for hd in heads:
            s_old = s_ref[0, hd] if carry else s0_ref[c, hd]
            vn = us[hd][sl] - _bdot(ws[hd][sl], s_old)
            o_state[hd].append(_bdot(qgs[hd][sl], s_old))
            glast = gcl[hd][(c + 1) * period - 1:(c + 1) * period, :]
            s_new = s_old * jnp.exp(glast) + _bdot_tn(ks[hd][sl] * jnp.exp(glast - gcl[hd][sl]), vn)
            if carry:
                s_ref[0, hd] = s_new
            else:
                s_ref[c, hd] = s_new
            v_new[hd].append(vn)
    for hd in heads:
        o = jnp.concatenate(o_state[hd], axis=0) + _bdot(qks[hd], jnp.concatenate(v_new[hd], axis=0))
        o = _rms(o, og_ref[...])
        o_ref[:, hd * DN_DV:(hd + 1) * DN_DV] = o * _silu(z_ref[:, hd * DN_DV:(hd + 1) * DN_DV])


_DELTA_ROWS = 256
_DELTA_SEQS = 16


def _delta_mixer(qkv, z, ba, c0, s0, b, t, conv_w, a_log, dt_bias, o_norm_g):
    al = jnp.zeros((1, LANES), F32).at[0, DN_HEADS:2 * DN_HEADS].set(a_log)
    dt = jnp.zeros((1, LANES), F32).at[0, DN_HEADS:2 * DN_HEADS].set(dt_bias)
    carry = t >= DN_CHUNK
    if carry:
        period, rows, spb = DN_CHUNK, _DELTA_ROWS, 1
        nj = t // rows
        grid = (b, nj)
        tok = lambda i, j: (i * nj + j, 0)
        st = lambda i, j: (i, 0, 0, 0)
        const = lambda i, j: (0, 0)
        prev, prev_spec = c0, pl.BlockSpec((1, DN_CONV - 1, QKV_DIM), lambda i, j: (i, 0, 0))
        scratch = [pltpu.VMEM((SUBLANES, QKV_DIM), F32)]
        sem = ("parallel", "arbitrary")
    else:
        period, spb = t, _DELTA_SEQS
        rows = spb * period
        grid = (b // spb,)
        tok = lambda i: (i, 0)
        st = lambda i: (i, 0, 0, 0)
        const = lambda i: (0, 0)
        hist = jnp.zeros((b, period, QKV_DIM), F32).at[:, period - (DN_CONV - 1):].set(c0)
        prev = jnp.roll(hist.reshape(b // spb, rows, QKV_DIM), -period, axis=1).reshape(b * period, QKV_DIM)
        prev_spec = pl.BlockSpec((rows, QKV_DIM), tok)
        scratch = []
        sem = ("parallel",)
    state = pl.BlockSpec((spb, DN_HEADS, DN_DK, DN_DV), st)
    return pl.pallas_call(
        functools.partial(_delta_kernel, period=period, carry=carry),
        grid=grid,
        in_specs=[
            pl.BlockSpec((rows, QKV_DIM), tok),
            pl.BlockSpec((rows, DN_V), tok),
            pl.BlockSpec((rows, LANES), tok),
            prev_spec,
            state,
            pl.BlockSpec((DN_CONV, QKV_DIM), const),
            pl.BlockSpec((1, LANES), const),
            pl.BlockSpec((1, LANES), const),
            pl.BlockSpec((1, DN_DV), const),
        ],
        out_specs=[pl.BlockSpec((rows, DN_V), tok), state],
        out_shape=[
            jax.ShapeDtypeStruct((b * t, DN_V), F32),
            jax.ShapeDtypeStruct((b, DN_HEADS, DN_DK, DN_DV), F32),
        ],
        scratch_shapes=scratch,
        compiler_params=_cparams(sem, 48),
        name="delta_mixer",
    )(qkv, z, ba, prev, s0, conv_w, al, dt, o_norm_g.reshape(1, DN_DV))


def _sgu_kernel(gu_ref, gv_ref, w_ref, b_ref, lg_ref, lb_ref, h_ref, oa_ref, wa_ref, wb_ref, o_ref, *rest, period):
    vv_ref = rest[0] if rest else None
    n = gu_ref.shape[0]
    gated = []
    row = lax.broadcasted_iota(I32, (n, n), 0)
    col = lax.broadcasted_iota(I32, (n, n), 1)
    keep = row >= col
    if period < n:
        keep = keep & ((row // period) == (col // period))
    for g in range(SG_GROUPS):
        sl = slice(g * SG_CH, (g + 1) * SG_CH)
        u = _gelu(gu_ref[:, sl])
        x = _gelu(gv_ref[:, sl])
        mu = jnp.mean(x, axis=-1, keepdims=True)
        xc = x - mu
        var = jnp.mean(xc * xc, axis=-1, keepdims=True)
        vv = xc * lax.rsqrt(var + EPS) * lg_ref[:, sl] + lb_ref[:, sl]
        if vv_ref is not None:
            vv_ref[:, sl] = vv
        wm = jnp.where(keep, w_ref[g], 0.0).astype(BF16)
        mix = jnp.dot(wm, vv.astype(BF16), preferred_element_type=F32) + b_ref[g]
        gated.append(u * mix)
    o_b = jnp.concatenate(gated, axis=-1)
    o_ref[...] = (h_ref[...]
                  + jnp.dot(oa_ref[...].astype(BF16), wa_ref[...], preferred_element_type=F32)
                  + jnp.dot(o_b.astype(BF16), wb_ref[...], preferred_element_type=F32))


def _sgu_out(gu, gv, w, bias, ln_g, ln_b, period, emit_vv, h, o_a, wa, wb):
    n_tok, d = h.shape
    blk = pl.BlockSpec((SG_CHUNK, SG_WIDTH), lambda i: (i, 0))
    row = pl.BlockSpec((SG_CHUNK, d), lambda i: (i, 0))
    out_shape = [jax.ShapeDtypeStruct((n_tok, d), F32)]
    out_specs = [row]
    if emit_vv:
        out_shape.append(jax.ShapeDtypeStruct((n_tok, SG_WIDTH), F32))
        out_specs.append(blk)
    return pl.pallas_call(
        functools.partial(_sgu_kernel, period=period),
        grid=(n_tok // SG_CHUNK,),
        in_specs=[
            blk, blk,
            pl.BlockSpec((SG_GROUPS, SG_CHUNK, SG_CHUNK), lambda i: (0, 0, 0)),
            pl.BlockSpec((SG_GROUPS, SG_CHUNK, SG_CH), lambda i: (0, 0, 0)),
            pl.BlockSpec((1, SG_WIDTH), lambda i: (0, 0)),
            pl.BlockSpec((1, SG_WIDTH), lambda i: (0, 0)),
            row, blk,
            pl.BlockSpec(wa.shape, lambda i: (0, 0)),
            pl.BlockSpec(wb.shape, lambda i: (0, 0)),
        ],
        out_specs=out_specs,
        out_shape=out_shape,
        compiler_params=_cparams(("parallel",)),
        name="sgu_out",
    )(gu, gv, w, bias, ln_g.reshape(1, SG_WIDTH), ln_b.reshape(1, SG_WIDTH), h, o_a, wa, wb)


def _mem_attn_kernel(q_ref, k_ref, v_ref, o_ref):
    for hd in range(MEM_HEADS):
        sl = slice(hd * MEM_HD, (hd + 1) * MEM_HD)
        q = q_ref[0, :, sl].astype(BF16)
        k = k_ref[0, :, sl].astype(BF16)
        v = v_ref[0, :, sl].astype(BF16)
        s = lax.dot_general(q, k, (((1,), (1,)), ((), ())), preferred_element_type=F32) * (MEM_HD ** -0.5)
        m = jnp.max(s, axis=-1, keepdims=True)
        e = jnp.exp(s - m)
        p = e / jnp.sum(e, axis=-1, keepdims=True)
        o_ref[0, :, sl] = jnp.dot(p.astype(BF16), v, preferred_element_type=F32)


def _mem_attn_cache_kernel(q_ref, k_ref, v_ref, o_ref):
    t = q_ref.shape[1]
    q_all = jnp.concatenate([q_ref[0, :, hd * MEM_HD:(hd + 1) * MEM_HD] for hd in range(MEM_HEADS)], axis=0)
    k2 = k_ref[...].reshape(MEM_LEN * MEM_HEADS, MEM_HD).astype(BF16)
    v2 = v_ref[...].reshape(MEM_LEN * MEM_HEADS, MEM_HD).astype(BF16)
    s = lax.dot_general(q_all.astype(BF16), k2, (((1,), (1,)), ((), ())),
                        preferred_element_type=F32) * (MEM_HD ** -0.5)
    q_head = lax.broadcasted_iota(I32, s.shape, 0) // t
    k_head = lax.broadcasted_iota(I32, s.shape, 1) % MEM_HEADS
    s = jnp.where(q_head == k_head, s, -jnp.inf)
    m = jnp.max(s, axis=-1, keepdims=True)
    e = jnp.exp(s - m)
    p = e / jnp.sum(e, axis=-1, keepdims=True)
    out = jnp.dot(p.astype(BF16), v2, preferred_element_type=F32)
    for hd in range(MEM_HEADS):
        o_ref[0, :, hd * MEM_HD:(hd + 1) * MEM_HD] = out[hd * t:(hd + 1) * t, :]


def _mem_attn(q, mk, mv, layer=None):
    b, t, d = q.shape
    tq = min(t, 512)
    if layer is None:
        kv = pl.BlockSpec((1, MEM_LEN, d), lambda i, j: (i, 0, 0))
        body = _mem_attn_kernel
    else:
        kv = pl.BlockSpec((None, None, MEM_LEN, MEM_HEADS, MEM_HD), lambda i, j: (layer, i, 0, 0, 0))
        body = _mem_attn_cache_kernel
    return pl.pallas_call(
        body,
        grid=(b, t // tq),
        in_specs=[pl.BlockSpec((1, tq, d), lambda i, j: (i, j, 0)), kv, kv],
        out_specs=pl.BlockSpec((1, tq, d), lambda i, j: (i, j, 0)),
        out_shape=jax.ShapeDtypeStruct((b, t, d), F32),
        compiler_params=_cparams(("parallel", "parallel")),
        name="mem_attn",
    )(q, mk, mv)


def _attn_layer_kernel(h_ref, g_ref, wq_ref, k_ref, v_ref, wo_ref, o_ref):
    x = h_ref[...]
    q = jnp.dot(_rms(x, g_ref[...]).astype(BF16), wq_ref[...], preferred_element_type=F32)
    heads = []
    for hd in range(MEM_HEADS):
        sl = slice(hd * MEM_HD, (hd + 1) * MEM_HD)
        k = k_ref[0, :, sl].astype(BF16)
        v = v_ref[0, :, sl].astype(BF16)
        s = lax.dot_general(q[:, sl].astype(BF16), k, (((1,), (1,)), ((), ())),
                            preferred_element_type=F32) * (MEM_HD ** -0.5)
        m = jnp.max(s, axis=-1, keepdims=True)
        e = jnp.exp(s - m)
        p = e / jnp.sum(e, axis=-1, keepdims=True)
        heads.append(jnp.dot(p.astype(BF16), v, preferred_element_type=F32))
    att = jnp.concatenate(heads, axis=-1)
    o_ref[...] = x + jnp.dot(att.astype(BF16), wo_ref[...], preferred_element_type=F32)


def _attn_layer(h, g, wq, mk, mv, wo, b, t):
    d = h.shape[1]
    tq = 512
    nj = t // tq
    row = pl.BlockSpec((tq, d), lambda i, j: (i * nj + j, 0))
    kv = pl.BlockSpec((1, MEM_LEN, d), lambda i, j: (i, 0, 0))
    full = pl.BlockSpec((d, d), lambda i, j: (0, 0))
    return pl.pallas_call(
        _attn_layer_kernel,
        grid=(b, nj),
        in_specs=[row, pl.BlockSpec((1, d), lambda i, j: (0, 0)), full, kv, kv, full],
        out_specs=row,
        out_shape=jax.ShapeDtypeStruct(h.shape, F32),
        compiler_params=_cparams(("parallel", "parallel"), 48),
        name="attn_layer",
    )(h, g.reshape(1, d), wq, mk, mv, wo)


def _top_rows(s, n_top, rank=None, payload=None):
    if rank is None:
        rank = lax.broadcasted_iota(I32, s.shape, 0)
    big = jnp.int32(2 ** 30)
    vals, ids, pays = [], [], []
    for _ in range(n_top):
        m = jnp.max(s, axis=0, keepdims=True)
        i = jnp.min(jnp.where(s == m, rank, big), axis=0, keepdims=True)
        hit = rank == i
        vals.append(m)
        ids.append(i)
        if payload is not None:
            pays.append(jnp.max(jnp.where(hit, payload, -1), axis=0, keepdims=True))
        s = jnp.where(hit, -jnp.inf, s)
    out = [jnp.concatenate(vals, axis=0), jnp.concatenate(ids, axis=0)]
    if payload is not None:
        out.append(jnp.concatenate(pays, axis=0))
    return out


_PAIR_GROUPS = ((0, 0), (0, 8), (1, 0), (2, 0), (3, 0), (4, 0), (5, 0), (6, 0), (7, 0))


def _pair_candidates(sv0, si0, sv1, si1):
    sub = lax.broadcasted_iota(I32, (SUBLANES,) + sv0.shape[1:], 0)
    cand, flat, eid = [], [], []
    for a, b0 in _PAIR_GROUPS:
        cand.append(sv0[a:a + 1, :] + sv1[b0:b0 + SUBLANES, :])
        flat.append(a * PEER_TOPK + b0 + sub)
        eid.append(si0[a:a + 1, :] * PEER_NKEYS + si1[b0:b0 + SUBLANES, :])
    cand.append(sv0[SUBLANES:, :] + sv1[0:1, :])
    flat.append((SUBLANES + sub) * PEER_TOPK)
    eid.append(si0[SUBLANES:, :] * PEER_NKEYS + si1[0:1, :])
    return jnp.concatenate(cand, axis=0), jnp.concatenate(flat, axis=0), jnp.concatenate(eid, axis=0)


def _peer_route_kernel(x_ref, g_ref, wq_ref, keys_ref,
                       xn_ref, roff_ref, sh_ref, gate_ref, e_scr, g_scr):
    n = _rms(x_ref[...], g_ref[...])
    for s in range(ROW_TILES):
        xn_ref[:, s, :] = n[:, s * LANES:(s + 1) * LANES]
    q = jnp.dot(n.astype(BF16), wq_ref[...], preferred_element_type=F32)
    for hd in range(PEER_HEADS):
        qh = q[:, hd * PEER_DKEY:(hd + 1) * PEER_DKEY].astype(BF16)
        tops = []
        for p in range(2):
            s_t = lax.dot_general(keys_ref[hd, p], qh, (((1,), (1,)), ((), ())),
                                  preferred_element_type=F32)
            tops.append(_top_rows(s_t, PEER_TOPK))
        (sv0, si0), (sv1, si1) = tops
        cand, flat, ecand = _pair_candidates(sv0, si0, sv1, si1)
        cv, _, ce = _top_rows(cand, PEER_TOPK, rank=flat, payload=ecand)
        ex = jnp.exp(cv - cv[0:1, :])
        gate = ex / jnp.sum(ex, axis=0, keepdims=True)
        e_scr[hd * PEER_TOPK:(hd + 1) * PEER_TOPK, :] = ce
        g_scr[hd * PEER_TOPK:(hd + 1) * PEER_TOPK, :] = gate
    e_t = pltpu.bitcast(pltpu.bitcast(e_scr[...], F32).T, I32)
    hi = e_t >= HALF_EXPERTS
    roff_ref[...] = jnp.where(hi, e_t - HALF_EXPERTS, e_t) * ROW_TILES
    sh_ref[...] = jnp.where(hi, 0, 16)
    gate_ref[...] = g_scr[...].T


def _peer_route(h, g, wq, keys_pad):
    n_tok, d = h.shape
    tb = LANES
    row = pl.BlockSpec((tb, d), lambda i: (i, 0))
    slot = pl.BlockSpec((tb, PEER_SLOTS), lambda i: (i, 0))
    return pl.pallas_call(
        _peer_route_kernel,
        grid=(n_tok // tb,),
        in_specs=[
            row,
            pl.BlockSpec((1, d), lambda i: (0, 0)),
            pl.BlockSpec(wq.shape, lambda i: (0, 0)),
            pl.BlockSpec(keys_pad.shape, lambda i: (0, 0, 0, 0)),
        ],
        out_specs=[pl.BlockSpec((tb, ROW_TILES, LANES), lambda i: (i, 0, 0)), slot, slot, slot],
        out_shape=[
            jax.ShapeDtypeStruct((n_tok, ROW_TILES, LANES), F32),
            jax.ShapeDtypeStruct((n_tok, PEER_SLOTS), I32),
            jax.ShapeDtypeStruct((n_tok, PEER_SLOTS), I32),
            jax.ShapeDtypeStruct((n_tok, PEER_SLOTS), F32),
        ],
        scratch_shapes=[pltpu.VMEM((PEER_SLOTS, tb), I32), pltpu.VMEM((PEER_SLOTS, tb), F32)],
        compiler_params=_cparams(("parallel",), 48),
        name="peer_route",
    )(h, g.reshape(1, d), wq, keys_pad)


def _expert_tile(tab_ref, roff, shv):
    w = tab_ref[pl.ds(pl.multiple_of(roff, SUBLANES), SUBLANES), :]
    return pltpu.bitcast(w << shv, F32)


def _fold_rows(ps):
    sub = lax.broadcasted_iota(I32, (SUBLANES, LANES), 0)
    dist = SUBLANES // 2
    while len(ps) > 1:
        lo = (sub % (2 * dist)) < dist
        half = len(ps) // 2
        nxt = []
        for a in range(half):
            x, y = ps[a], ps[a + half]
            nxt.append(jnp.where(lo, x, pltpu.roll(y, dist, 0))
                       + jnp.where(lo, pltpu.roll(x, SUBLANES - dist, 0), y))
        ps = nxt
        dist //= 2
    return ps[0]


def _slot_columns(rows8, dst_ref):
    for u in range(SUBLANES):
        dst_ref[u] = jnp.broadcast_to(rows8[u:u + 1, :], (PEER_SLOTS, LANES)).T


def _bcast_row(ref, u, k):
    return jnp.broadcast_to(ref[u, k:k + 1, :], (SUBLANES, LANES))


def _peer_token_loop(tb, prep, token_group):
    n_pairs = tb // (2 * SUBLANES)
    prep(0, 0)

    def pair(i, carry):
        g0 = 2 * i
        prep(g0 + 1, 1)
        token_group(g0, 0)
        prep(jnp.minimum(g0 + 2, 2 * n_pairs - 1), 0)
        token_group(g0 + 1, 1)
        return carry

    lax.fori_loop(0, n_pairs, pair, 0)


def _peer_up_kernel(roff_ref, sh_ref, gate_ref, x_ref, tab_ref, c_ref, shs_a, shs_b):
    tb = sh_ref.shape[0]
    shs = (shs_a, shs_b)

    def prep(grp, par):
        r0 = pl.multiple_of(grp * SUBLANES, SUBLANES)
        _slot_columns(pltpu.bitcast(sh_ref[pl.ds(r0, SUBLANES), :], F32), shs[par])

    def token_group(grp, par):
        rows = []
        for u in range(SUBLANES):
            t = grp * SUBLANES + u
            x_t = x_ref[t]
            tiles = []
            for g in range(PEER_SLOTS // SUBLANES):
                ps = []
                for jj in range(SUBLANES):
                    k = g * SUBLANES + jj
                    shv = pltpu.bitcast(_bcast_row(shs[par], u, k), U32)
                    ps.append(_expert_tile(tab_ref, roff_ref[t, k], shv) * x_t)
                tiles.append(_fold_rows(ps))
            r = jnp.concatenate(tiles, axis=0)
            rows.append(jnp.sum(r.T, axis=0, keepdims=True))
        out = pl.ds(pl.multiple_of(grp * SUBLANES, SUBLANES), SUBLANES)
        c_ref[out, :] = gate_ref[out, :] * _gelu(jnp.concatenate(rows, axis=0))

    _peer_token_loop(tb, prep, token_group)


def _peer_down_kernel(roff_hbm, sh_ref, sh_next, c_ref, c_next, tab_ref, o_ref,
                      idx_a, idx_b, sh_a, sh_b, c_a, c_b, sem):
    i = pl.program_id(0)
    n = pl.num_programs(0)
    tb = sh_ref.shape[0]
    th = tb // 2
    idx = (idx_a, idx_b)

    def idx_copy(step, half):
        row0 = pl.multiple_of(step * tb + half * th, th)
        return pltpu.make_async_copy(roff_hbm.at[pl.ds(row0, th)], idx[half], sem.at[half])

    def build(src_sh, src_c, row, sh_t, c_t, u, after=None):
        sh_row = src_sh[row:row + 1, :]
        c_row = src_c[row:row + 1, :]
        if after is not None:
            zero = lax.shift_right_logical(pltpu.bitcast(after[0:1, :], U32), jnp.uint32(32))
            sh_row = sh_row + pltpu.bitcast(zero, I32)
            c_row = c_row + pltpu.bitcast(zero, F32)
        sh_t[u] = jnp.broadcast_to(pltpu.bitcast(sh_row, F32), (PEER_SLOTS, LANES)).T
        c_t[u] = jnp.broadcast_to(c_row, (PEER_SLOTS, LANES)).T

    def consume(half, sh_t, c_t, u):
        accs = [jnp.zeros((SUBLANES, LANES), F32) for _ in range(2)]
        for k in range(PEER_SLOTS):
            shv = pltpu.bitcast(_bcast_row(sh_t, u, k), U32)
            accs[k % 2] = accs[k % 2] + _expert_tile(tab_ref, idx[half][u, k], shv) * _bcast_row(c_t, u, k)
        acc = accs[0] + accs[1]
        o_ref[half * th + u] = acc
        return acc

    def half_step(half, use, fill, src_sh, src_c, row0):
        @pl.when(i >= 0)
        def _():
            acc = None
            for u in range(th):
                build(src_sh, src_c, row0 + u, fill[0], fill[1], u, after=acc)
                acc = consume(half, use[0], use[1], u)

    tiles = ((sh_a, c_a), (sh_b, c_b))

    @pl.when(i == 0)
    def _():
        idx_copy(0, 0).start()
        idx_copy(0, 1).start()
        for u in range(th):
            build(sh_ref, c_ref, u, sh_a, c_a, u)

    idx_copy(i, 0).wait()
    half_step(0, tiles[0], tiles[1], sh_ref, c_ref, th)

    @pl.when(i + 1 < n)
    def _():
        idx_copy(i + 1, 0).start()

    idx_copy(i, 1).wait()
    half_step(1, tiles[1], tiles[0], sh_next, c_next, 0)

    @pl.when(i + 1 < n)
    def _():
        idx_copy(i + 1, 1).start()


_PEER_TB = 64
_PEER_DOWN_TB = 64
_PEER_TABLE_MIB = HALF_EXPERTS * D_MODEL * 4 // 2 ** 20
_PEER_VMEM_MIB = min(_PEER_TABLE_MIB + 20, VMEM_MIB_V7X - 8)
_COL_TILES = pltpu.VMEM((SUBLANES, PEER_SLOTS, LANES), F32)


def _table_spec(tab):
    return pl.BlockSpec(tab.shape, lambda i: (0, 0), pipeline_mode=pl.Buffered(1))


def _peer_up(roff, sh, gate, x_tiles, tab):
    n_tok = roff.shape[0]
    tb = _PEER_TB
    slot = pl.BlockSpec((tb, PEER_SLOTS), lambda i: (i, 0))
    return pl.pallas_call(
        _peer_up_kernel,
        grid=(n_tok // tb,),
        in_specs=[
            pl.BlockSpec((tb, PEER_SLOTS), lambda i: (i, 0), memory_space=pltpu.SMEM),
            slot, slot,
            pl.BlockSpec((tb, ROW_TILES, LANES), lambda i: (i, 0, 0)),
            _table_spec(tab),
        ],
        out_specs=slot,
        out_shape=jax.ShapeDtypeStruct((n_tok, PEER_SLOTS), F32),
        scratch_shapes=[_COL_TILES] * 2,
        compiler_params=_cparams(("arbitrary",), _PEER_VMEM_MIB),
        name="peer_up",
    )(roff, sh, gate, x_tiles, tab)


def _peer_down(roff, sh, coef, tab):
    n_tok = roff.shape[0]
    tb = _PEER_DOWN_TB
    th = tb // 2
    n_steps = n_tok // tb
    cur = pl.BlockSpec((tb, PEER_SLOTS), lambda i: (i, 0))
    nxt = pl.BlockSpec((tb, PEER_SLOTS), lambda i: (jnp.minimum(i + 1, n_steps - 1), 0))
    half_tiles = pltpu.VMEM((th, PEER_SLOTS, LANES), F32)
    return pl.pallas_call(
        _peer_down_kernel,
        grid=(n_steps,),
        in_specs=[pl.BlockSpec(memory_space=pl.ANY), cur, nxt, cur, nxt, _table_spec(tab)],
        out_specs=pl.BlockSpec((tb, ROW_TILES, LANES), lambda i: (i, 0, 0)),
        out_shape=jax.ShapeDtypeStruct((n_tok, ROW_TILES, LANES), F32),
        scratch_shapes=[pltpu.SMEM((th, PEER_SLOTS), I32)] * 2 + [half_tiles] * 4
        + [pltpu.SemaphoreType.DMA((2,))],
        compiler_params=_cparams(("arbitrary",), _PEER_VMEM_MIB),
        name="peer_down",
    )(roff, sh, sh, coef, coef, tab)


def _pack_kernel(lo_ref, hi_ref, o_ref):
    low = pltpu.bitcast(pltpu.bitcast(lo_ref[...].astype(BF16).astype(F32), U32) >> 16, I32)
    bits = pltpu.bitcast(hi_ref[...], I32)
    sign = bits & jnp.int32(-2 ** 31)
    mag = jnp.minimum(bits & jnp.int32(0x7FFFFFFF), jnp.int32(0x7F7F0000))
    top = jnp.maximum(mag + jnp.int32(0x8000) - low, 0) & jnp.int32(-65536)
    word = pltpu.bitcast(sign | top | low, U32)
    for s in range(ROW_TILES):
        o_ref[:, s, :] = word[:, s * LANES:(s + 1) * LANES]


def _pack_table(tabs, layer, rows=512):
    d = tabs.shape[2]
    n_blk = HALF_EXPERTS // rows
    packed = pl.pallas_call(
        _pack_kernel,
        grid=(n_blk,),
        in_specs=[pl.BlockSpec((None, rows, d), lambda i: (layer, i, 0)),
                  pl.BlockSpec((None, rows, d), lambda i: (layer, i + n_blk, 0))],
        out_specs=pl.BlockSpec((rows, ROW_TILES, LANES), lambda i: (i, 0, 0)),
        out_shape=jax.ShapeDtypeStruct((HALF_EXPERTS, ROW_TILES, LANES), U32),
        compiler_params=_cparams(("parallel",), 48),
        name="pack_table",
    )(tabs, tabs)
    return packed.reshape(HALF_EXPERTS * ROW_TILES, LANES)


def _peer(h, g, wq, keys_pad, u_tab, v_tab):
    xn, roff, sh, gate = _peer_route(h, g, wq, keys_pad)
    coef = _peer_up(roff, sh, gate, xn, u_tab)
    return _peer_down(roff, sh, coef, v_tab)


def _prep_layer(l, w_in, conv_w, a_log, dt_bias, o_norm_g, sg_ln_g, sg_ln_b, sg_w, sg_b, w_out,
                norm_mix_g, norm_mem_g, mem_norm_g, w_mq, w_mk, w_mv, w_mo, norm_ffn_g,
                peer_wq, peer_keys, peer_u, peer_v):
    o1 = QKV_DIM
    o2 = o1 + DN_V
    o4 = o2 + 2 * DN_HEADS
    o5 = o4 + SG_WIDTH
    wi = w_in[l]
    ba_cols = jnp.pad(wi[:, o2:o4], ((0, 0), (0, LANES - 2 * DN_HEADS)))
    w_in_r = jnp.concatenate([wi[:, :o2], wi[:, o4:], ba_cols], axis=1).astype(BF16)
    kz = jnp.zeros((PEER_HEADS, PEER_NKEYS, PEER_DKEY // 2), F32)
    keys_pad = jnp.stack([jnp.concatenate([peer_keys[l][:, 0], kz], axis=-1),
                          jnp.concatenate([kz, peer_keys[l][:, 1]], axis=-1)], axis=1).astype(BF16)
    return dict(
        w_in=w_in_r, conv_w=conv_w[l], a_log=a_log[l], dt_bias=dt_bias[l], o_norm_g=o_norm_g[l],
        sg_ln_g=sg_ln_g[l].reshape(-1), sg_ln_b=sg_ln_b[l].reshape(-1), sg_w=sg_w[l], sg_b=sg_b[l],
        w_out_a=w_out[l][:DN_V].astype(BF16), w_out_b=w_out[l][DN_V:].astype(BF16),
        norm_mix_g=norm_mix_g[l], norm_mem_g=norm_mem_g[l], mem_norm_g=mem_norm_g[l],
        w_mq=w_mq[l].astype(BF16), w_mkv=jnp.concatenate([w_mk[l], w_mv[l]], axis=1).astype(BF16),
        w_mo=w_mo[l].astype(BF16), norm_ffn_g=norm_ffn_g[l], peer_wq=peer_wq[l].astype(BF16),
        keys_pad=keys_pad, u_tab=_pack_table(peer_u, l), v_tab=_pack_table(peer_v, l),
    )


_IN_WIDTHS = (QKV_DIM, DN_V, SG_WIDTH, SG_WIDTH, LANES)


def _layer(h, delta, b, t, mk, mv, s0, c0, p, emit_vv, cache_layer=None):
    n_tok = b * t
    outs = _norm_proj(h, p["norm_mix_g"], p["w_in"], _IN_WIDTHS, delta=delta)
    if delta is not None:
        h, outs = outs[0], outs[1:]
    qkv, z, gu, gv, ba = outs
    o_a, s_new = _delta_mixer(qkv, z, ba, c0, s0, b, t, p["conv_w"], p["a_log"], p["dt_bias"], p["o_norm_g"])
    conv_new = qkv.reshape(b, t, QKV_DIM)[:, t - (DN_CONV - 1):, :]
    period = min(t, SG_CHUNK)
    reps = SG_CHUNK // period
    sg_w = jnp.tile(p["sg_w"][:, :period, :period], (1, reps, reps))
    sg_bias = jnp.broadcast_to(jnp.tile(p["sg_b"][:, :period], (1, reps))[:, :, None],
                               (SG_GROUPS, SG_CHUNK, SG_CH))
    sg_out = _sgu_out(gu, gv, sg_w, sg_bias, p["sg_ln_g"], p["sg_ln_b"], period, emit_vv,
                      h, o_a, p["w_out_a"], p["w_out_b"])
    h = sg_out[0]
    vv = sg_out[1] if emit_vv else None
    if cache_layer is None and t % 512 == 0:
        h = _attn_layer(h, p["norm_mem_g"], p["w_mq"], mk, mv, p["w_mo"], b, t)
    else:
        (q,) = _norm_proj(h, p["norm_mem_g"], p["w_mq"], (D_MODEL,))
        att = _mem_attn(q.reshape(b, t, D_MODEL), mk, mv, layer=cache_layer)
        h = _proj_residual(h, [att.reshape(n_tok, D_MODEL)], [p["w_mo"]])
    return h, s_new, conv_new, vv


def kernel(x_prompt, x_sample, state_delta, state_conv, cache_mem_k, cache_mem_v, mem_prompt, w_in, conv_w, a_log, dt_bias, o_norm_g, sg_ln_g, sg_ln_b, sg_w, sg_b, w_out, norm_mix_g, norm_mem_g, mem_norm_g, w_mq, w_mk, w_mv, w_mo, norm_ffn_g, peer_wq, peer_keys, peer_u, peer_v, final_norm_g):
    depth = w_in.shape[0]
    bp, tp, d = x_prompt.shape
    bs, ts, _ = x_sample.shape
    hp = x_prompt.reshape(bp * tp, d)
    hs = x_sample.reshape(bs * ts, d)
    dp = ds = None
    mem_flat = mem_prompt.reshape(bp * MEM_LEN, d)
    sd_p, sc_p, mk_p, mv_p, sd_s, sc_s, vr_s = [], [], [], [], [], [], []
    for l in range(depth):
        p = _prep_layer(l, w_in, conv_w, a_log, dt_bias, o_norm_g, sg_ln_g, sg_ln_b, sg_w, sg_b, w_out,
                        norm_mix_g, norm_mem_g, mem_norm_g, w_mq, w_mk, w_mv, w_mo, norm_ffn_g,
                        peer_wq, peer_keys, peer_u, peer_v)
        mk, mv = _norm_proj(mem_flat, p["mem_norm_g"], p["w_mkv"], (D_MODEL, D_MODEL))
        mk = mk.reshape(bp, MEM_LEN, d)
        mv = mv.reshape(bp, MEM_LEN, d)
        s0 = jnp.zeros((bp, DN_HEADS, DN_DK, DN_DV), F32)
        c0 = jnp.zeros((bp, DN_CONV - 1, QKV_DIM), F32)
        hp, s_p, c_p, _ = _layer(hp, dp, bp, tp, mk, mv, s0, c0, p, False)
        hs, s_s, c_s, vv = _layer(hs, ds, bs, ts, cache_mem_k, cache_mem_v, state_delta[l], state_conv[l], p, True,
                                  cache_layer=l)
        dp = _peer(hp, p["norm_ffn_g"], p["peer_wq"], p["keys_pad"], p["u_tab"], p["v_tab"])
        ds = _peer(hs, p["norm_ffn_g"], p["peer_wq"], p["keys_pad"], p["u_tab"], p["v_tab"])
        sd_p.append(s_p)
        sc_p.append(c_p)
        mk_p.append(mk.reshape(bp, MEM_LEN, MEM_HEADS, MEM_HD))
        mv_p.append(mv.reshape(bp, MEM_LEN, MEM_HEADS, MEM_HD))
        sd_s.append(s_s)
        sc_s.append(c_s)
        vr_s.append(vv.reshape(bs, ts, SG_WIDTH))
    y_prompt = _add_norm(hp, dp, final_norm_g).reshape(bp, tp, d)
    y_sample = _add_norm(hs, ds, final_norm_g).reshape(bs, ts, d)
    return (y_prompt, y_sample, jnp.stack(sd_p), jnp.stack(sc_p), jnp.stack(mk_p), jnp.stack(mv_p),
            jnp.stack(sd_s), jnp.stack(sc_s), jnp.stack(vr_s))
```

```python
import functools
import math

import jax
import jax.numpy as jnp
from jax import lax
from jax.experimental import pallas as pl
from jax.experimental.pallas import tpu as pltpu

F32 = jnp.float32
BF16 = jnp.bfloat16
I32 = jnp.int32
U32 = jnp.uint32

D_MODEL = 1024
DN_HEADS = 4
DN_DK = 128
DN_DV = 128
DN_QK = DN_HEADS * DN_DK
DN_V = DN_HEADS * DN_DV
DN_CONV = 4
DN_CHUNK = 64
QKV_DIM = 2 * DN_QK + DN_V
SG_GROUPS = 4
SG_CH = 128
SG_WIDTH = SG_GROUPS * SG_CH
SG_CHUNK = 128
MEM_LEN = 256
MEM_HEADS = 4
MEM_HD = D_MODEL // MEM_HEADS
PEER_HEADS = 8
PEER_NKEYS = 128
PEER_NEXP = PEER_NKEYS * PEER_NKEYS
PEER_DKEY = 128
PEER_TOPK = 16
PEER_SLOTS = PEER_HEADS * PEER_TOPK
EPS = 1e-6

SUBLANES = 8
LANES = 128
ROW_TILES = D_MODEL // LANES
VMEM_MIB_V7X = 64
HALF_EXPERTS = PEER_NEXP // 2

_HI = lax.Precision.HIGHEST


def _cparams(sem, vmem_mib=None):
    kw = dict(dimension_semantics=sem)
    if vmem_mib is not None:
        kw["vmem_limit_bytes"] = vmem_mib * 1024 * 1024
    return pltpu.CompilerParams(**kw)


def _rms(x, g):
    return x * lax.rsqrt(jnp.mean(x * x, axis=-1, keepdims=True) + EPS) * g


def _gelu(x):
    return jax.nn.gelu(x, approximate=True)


def _sigmoid(x):
    return 1.0 / (1.0 + jnp.exp(-x))


def _silu(x):
    return x * _sigmoid(x)


def _softplus(x):
    return jnp.maximum(x, 0.0) + jnp.log(1.0 + jnp.exp(-jnp.abs(x)))


def _rows_from_tiles(t_ref):
    return jnp.concatenate([t_ref[:, s, :] for s in range(ROW_TILES)], axis=-1)


def _norm_proj_kernel(*refs, widths, has_delta, emit_h):
    it = iter(refs)
    x_ref = next(it)
    d_ref = next(it) if has_delta else None
    g_ref = next(it)
    w_ref = next(it)
    outs = list(it)
    x = x_ref[...]
    if has_delta:
        x = x + _rows_from_tiles(d_ref)
    if emit_h:
        outs[0][...] = x
        outs = outs[1:]
    n = _rms(x, g_ref[...]).astype(BF16)
    y = jnp.dot(n, w_ref[...], preferred_element_type=F32)
    c = 0
    for o, wd in zip(outs, widths):
        o[...] = y[:, c:c + wd]
        c += wd


def _norm_proj(x, g, w, widths, delta=None, tm=512):
    n_tok, d = x.shape
    tm = min(tm, n_tok)
    has_delta = delta is not None
    row = pl.BlockSpec((tm, d), lambda i: (i, 0))
    tiles = pl.BlockSpec((tm, ROW_TILES, LANES), lambda i: (i, 0, 0))
    in_specs = [row] + ([tiles] if has_delta else []) + [
        pl.BlockSpec((1, d), lambda i: (0, 0)),
        pl.BlockSpec(w.shape, lambda i: (0, 0)),
    ]
    out_shape, out_specs = [], []
    if has_delta:
        out_shape.append(jax.ShapeDtypeStruct((n_tok, d), F32))
        out_specs.append(row)
    for wd in widths:
        out_shape.append(jax.ShapeDtypeStruct((n_tok, wd), F32))
        out_specs.append(pl.BlockSpec((tm, wd), lambda i: (i, 0)))
    args = [x] + ([delta] if has_delta else []) + [g.reshape(1, d), w]
    return pl.pallas_call(
        functools.partial(_norm_proj_kernel, widths=tuple(widths), has_delta=has_delta, emit_h=has_delta),
        grid=(n_tok // tm,),
        in_specs=in_specs,
        out_specs=out_specs,
        out_shape=out_shape,
        compiler_params=_cparams(("parallel",), 48),
        name="norm_proj",
    )(*args)


def _proj_res_kernel(*refs, n_in):
    h_ref = refs[0]
    a_refs = refs[1:1 + n_in]
    w_refs = refs[1 + n_in:1 + 2 * n_in]
    o_ref = refs[1 + 2 * n_in]
    acc = h_ref[...]
    for a, w in zip(a_refs, w_refs):
        acc = acc + jnp.dot(a[...].astype(BF16), w[...], preferred_element_type=F32)
    o_ref[...] = acc


def _proj_residual(h, acts, ws, tm=512):
    n_tok, d = h.shape
    tm = min(tm, n_tok)
    in_specs = [pl.BlockSpec((tm, d), lambda i: (i, 0))]
    in_specs += [pl.BlockSpec((tm, a.shape[1]), lambda i: (i, 0)) for a in acts]
    in_specs += [pl.BlockSpec(w.shape, lambda i: (0, 0)) for w in ws]
    return pl.pallas_call(
        functools.partial(_proj_res_kernel, n_in=len(acts)),
        grid=(n_tok // tm,),
        in_specs=in_specs,
        out_specs=pl.BlockSpec((tm, d), lambda i: (i, 0)),
        out_shape=jax.ShapeDtypeStruct((n_tok, d), F32),
        compiler_params=_cparams(("parallel",), 48),
        name="proj_residual",
    )(h, *acts, *ws)


def _add_norm_kernel(x_ref, d_ref, g_ref, o_ref):
    o_ref[...] = _rms(x_ref[...] + _rows_from_tiles(d_ref), g_ref[...])


def _add_norm(x, delta, g, tm=512):
    n_tok, d = x.shape
    tm = min(tm, n_tok)
    row = pl.BlockSpec((tm, d), lambda i: (i, 0))
    tiles = pl.BlockSpec((tm, ROW_TILES, LANES), lambda i: (i, 0, 0))
    return pl.pallas_call(
        _add_norm_kernel,
        grid=(n_tok // tm,),
        in_specs=[row, tiles, pl.BlockSpec((1, d), lambda i: (0, 0))],
        out_specs=row,
        out_shape=jax.ShapeDtypeStruct((n_tok, d), F32),
        compiler_params=_cparams(("parallel",)),
        name="add_norm",
    )(x, delta, g.reshape(1, d))


def _bdot(a, b):
    return jnp.dot(a.astype(BF16), b.astype(BF16), preferred_element_type=F32)


def _bdot_nt(a, b):
    return lax.dot_general(a.astype(BF16), b.astype(BF16), (((1,), (1,)), ((), ())), preferred_element_type=F32)


def _bdot_tn(a, b):
    return lax.dot_general(a.astype(BF16), b.astype(BF16), (((0,), (0,)), ((), ())), preferred_element_type=F32)


def _delta_kernel(qkv_ref, z_ref, ba_ref, prev_ref, s0_ref, cw_ref, al_ref, dt_ref, og_ref,
                  o_ref, s_ref, *scratch, period, carry):
    rows = qkv_ref.shape[0]
    n_sub = rows // period
    x = qkv_ref[...]
    if carry:
        tail_ref = scratch[0]
        j = pl.program_id(1)

        @pl.when(j == 0)
        def _():
            s_ref[0] = s0_ref[0]
            tail_ref[...] = jnp.zeros_like(tail_ref)
            tail_ref[SUBLANES - (DN_CONV - 1):, :] = prev_ref[0]

        xx = jnp.concatenate([tail_ref[...], x], axis=0)
        acc = jnp.zeros((rows, QKV_DIM), F32)
        for jj in range(DN_CONV):
            off = SUBLANES - (DN_CONV - 1) + jj
            acc = acc + xx[off:off + rows, :] * cw_ref[jj:jj + 1, :]
        tail_ref[...] = x[rows - SUBLANES:, :]
    else:
        pos = lax.broadcasted_iota(I32, (rows, QKV_DIM), 0) % period
        hist = prev_ref[...]
        acc = x * cw_ref[DN_CONV - 1:DN_CONV, :]
        for d in range(1, DN_CONV):
            shifted = jnp.where(pos >= d, pltpu.roll(x, d, 0), pltpu.roll(hist, d, 0))
            acc = acc + shifted * cw_ref[DN_CONV - 1 - d:DN_CONV - d, :]
    y = _silu(acc)
    ba = ba_ref[...]
    lane = lax.broadcasted_iota(I32, ba.shape, 1)
    bg = jnp.where(lane < DN_HEADS, _sigmoid(ba), -jnp.exp(al_ref[...]) * _softplus(ba + dt_ref[...]))

    row = lax.broadcasted_iota(I32, (rows, rows), 0)
    col = lax.broadcasted_iota(I32, (rows, rows), 1)
    same = (row // period) == (col // period)
    causal = same & (row >= col)
    strict = same & (row > col)
    eye = (row == col).astype(F32)
    gcs = jnp.dot(causal.astype(F32), bg, precision=_HI, preferred_element_type=F32)
    n_fac = int(math.log2(period)) - 1

    heads = range(DN_HEADS)
    qs, ks, vs, betas, gcl, decays, kbs = [], [], [], [], [], [], []
    for hd in heads:
        q = y[:, hd * DN_DK:(hd + 1) * DN_DK]
        k = y[:, DN_QK + hd * DN_DK:DN_QK + (hd + 1) * DN_DK]
        qs.append(q * lax.rsqrt(jnp.sum(q * q, axis=-1, keepdims=True) + EPS) * (DN_DK ** -0.5))
        ks.append(k * lax.rsqrt(jnp.sum(k * k, axis=-1, keepdims=True) + EPS))
        vs.append(y[:, 2 * DN_QK + hd * DN_DV:2 * DN_QK + (hd + 1) * DN_DV])
        betas.append(bg[:, hd:hd + 1])
        gc = gcs[:, DN_HEADS + hd:DN_HEADS + hd + 1]
        gc_cols = jnp.broadcast_to(gc, (rows, rows))
        decays.append(jnp.exp(jnp.where(causal, gc_cols - gc_cols.T, -jnp.inf)))
        gcl.append(gc)
        kbs.append(ks[hd] * betas[hd])
    a_s = [jnp.where(strict, _bdot_nt(kbs[hd], ks[hd]) * decays[hd], 0.0) for hd in heads]
    tinv = [eye - a for a in a_s]
    pw = [_bdot(a, a) for a in a_s]
    for f in range(n_fac):
        tinv = [t + _bdot(t, p) for t, p in zip(tinv, pw)]
        if f + 1 < n_fac:
            pw = [_bdot(p, p) for p in pw]
    us = [_bdot(tinv[hd], vs[hd] * betas[hd]) for hd in heads]
    ws = [_bdot(tinv[hd], kbs[hd] * jnp.exp(gcl[hd])) for hd in heads]
    qks = [_bdot_nt(qs[hd], ks[hd]) * decays[hd] for hd in heads]
    qgs = [qs[hd] * jnp.exp(gcl[hd]) for hd in heads]
    v_new = [[] for _ in heads]
    o_state = [[] for _ in heads]
    for c in range(n_sub):
        sl = slice(c * period, (c + 1) * period)
        for hd in heads:
            s_old = s_ref[0, hd] if carry else s0_ref[c, hd]
            vn = us[hd][sl] - _bdot(ws[hd][sl], s_old)
            o_state[hd].append(_bdot(qgs[hd][sl], s_old))
            glast = gcl[hd][(c + 1) * period - 1:(c + 1) * period, :]
            s_new = s_old * jnp.exp(glast) + _bdot_tn(ks[hd][sl] * jnp.exp(glast - gcl[hd][sl]), vn)
            if carry:
                s_ref[0, hd] = s_new
            else:
                s_ref[c, hd] = s_new
            v_new[hd].append(vn)
    for hd in heads:
        o = jnp.concatenate(o_state[hd], axis=0) + _bdot(qks[hd], jnp.concatenate(v_new[hd], axis=0))
        o = _rms(o, og_ref[...])
        o_ref[:, hd * DN_DV:(hd + 1) * DN_DV] = o * _silu(z_ref[:, hd * DN_DV:(hd + 1) * DN_DV])


_DELTA_ROWS = 256
_DELTA_SEQS = 16


def _delta_mixer(qkv, z, ba, c0, s0, b, t, conv_w, a_log, dt_bias, o_norm_g):
    al = jnp.zeros((1, LANES), F32).at[0, DN_HEADS:2 * DN_HEADS].set(a_log)
    dt = jnp.zeros((1, LANES), F32).at[0, DN_HEADS:2 * DN_HEADS].set(dt_bias)
    carry = t >= DN_CHUNK
    if carry:
        period, rows, spb = DN_CHUNK, _DELTA_ROWS, 1
        nj = t // rows
        grid = (b, nj)
        tok = lambda i, j: (i * nj + j, 0)
        st = lambda i, j: (i, 0, 0, 0)
        const = lambda i, j: (0, 0)
        prev, prev_spec = c0, pl.BlockSpec((1, DN_CONV - 1, QKV_DIM), lambda i, j: (i, 0, 0))
        scratch = [pltpu.VMEM((SUBLANES, QKV_DIM), F32)]
        sem = ("parallel", "arbitrary")
    else:
        period, spb = t, _DELTA_SEQS
        rows = spb * period
        grid = (b // spb,)
        tok = lambda i: (i, 0)
        st = lambda i: (i, 0, 0, 0)
        const = lambda i: (0, 0)
        hist = jnp.zeros((b, period, QKV_DIM), F32).at[:, period - (DN_CONV - 1):].set(c0)
        prev = jnp.roll(hist.reshape(b // spb, rows, QKV_DIM), -period, axis=1).reshape(b * period, QKV_DIM)
        prev_spec = pl.BlockSpec((rows, QKV_DIM), tok)
        scratch = []
        sem = ("parallel",)
    state = pl.BlockSpec((spb, DN_HEADS, DN_DK, DN_DV), st)
    return pl.pallas_call(
        functools.partial(_delta_kernel, period=period, carry=carry),
        grid=grid,
        in_specs=[
            pl.BlockSpec((rows, QKV_DIM), tok),
            pl.BlockSpec((rows, DN_V), tok),
            pl.BlockSpec((rows, LANES), tok),
            prev_spec,
            state,
            pl.BlockSpec((DN_CONV, QKV_DIM), const),
            pl.BlockSpec((1, LANES), const),
            pl.BlockSpec((1, LANES), const),
            pl.BlockSpec((1, DN_DV), const),
        ],
        out_specs=[pl.BlockSpec((rows, DN_V), tok), state],
        out_shape=[
            jax.ShapeDtypeStruct((b * t, DN_V), F32),
            jax.ShapeDtypeStruct((b, DN_HEADS, DN_DK, DN_DV), F32),
        ],
        scratch_shapes=scratch,
        compiler_params=_cparams(sem, 48),
        name="delta_mixer",
    )(qkv, z, ba, prev, s0, conv_w, al, dt, o_norm_g.reshape(1, DN_DV))


def _sgu_kernel(gu_ref, gv_ref, w_ref, b_ref, lg_ref, lb_ref, h_ref, oa_ref, wa_ref, wb_ref, o_ref, *rest, period):
    vv_ref = rest[0] if rest else None
    n = SG_CHUNK
    row = lax.broadcasted_iota(I32, (n, n), 0)
    col = lax.broadcasted_iota(I32, (n, n), 1)
    keep = row >= col
    if period < n:
        keep = keep & ((row // period) == (col // period))
    wms = [jnp.where(keep, w_ref[g], 0.0).astype(BF16) for g in range(SG_GROUPS)]
    o_bs = []
    for c in range(gu_ref.shape[0] // n):
        rows = slice(c * n, (c + 1) * n)
        gated = []
        for g in range(SG_GROUPS):
            sl = slice(g * SG_CH, (g + 1) * SG_CH)
            u = _gelu(gu_ref[rows, sl])
            x = _gelu(gv_ref[rows, sl])
            mu = jnp.mean(x, axis=-1, keepdims=True)
            xc = x - mu
            var = jnp.mean(xc * xc, axis=-1, keepdims=True)
            vv = xc * lax.rsqrt(var + EPS) * lg_ref[:, sl] + lb_ref[:, sl]
            if vv_ref is not None:
                vv_ref[rows, sl] = vv
            mix = jnp.dot(wms[g], vv.astype(BF16), preferred_element_type=F32) + b_ref[g]
            gated.append(u * mix)
        o_bs.append(jnp.concatenate(gated, axis=-1))
    o_b = jnp.concatenate(o_bs, axis=0)
    o_ref[...] = (h_ref[...]
                  + jnp.dot(oa_ref[...].astype(BF16), wa_ref[...], preferred_element_type=F32)
                  + jnp.dot(o_b.astype(BF16), wb_ref[...], preferred_element_type=F32))


_SGU_ROWS = 512


def _sgu_out(gu, gv, w, bias, ln_g, ln_b, period, emit_vv, h, o_a, wa, wb):
    n_tok, d = h.shape
    rows = min(_SGU_ROWS, n_tok)
    blk = blk_in = pl.BlockSpec((rows, SG_WIDTH), lambda i: (i, 0))
    row = row_in = pl.BlockSpec((rows, d), lambda i: (i, 0))
    out_shape = [jax.ShapeDtypeStruct((n_tok, d), F32)]
    out_specs = [row]
    if emit_vv:
        out_shape.append(jax.ShapeDtypeStruct((n_tok, SG_WIDTH), F32))
        out_specs.append(blk)
    return pl.pallas_call(
        functools.partial(_sgu_kernel, period=period),
        grid=(n_tok // rows,),
        in_specs=[
            blk_in, blk_in,
            pl.BlockSpec((SG_GROUPS, SG_CHUNK, SG_CHUNK), lambda i: (0, 0, 0)),
            pl.BlockSpec((SG_GROUPS, SG_CHUNK, SG_CH), lambda i: (0, 0, 0)),
            pl.BlockSpec((1, SG_WIDTH), lambda i: (0, 0)),
            pl.BlockSpec((1, SG_WIDTH), lambda i: (0, 0)),
            row_in, blk_in,
            pl.BlockSpec(wa.shape, lambda i: (0, 0)),
            pl.BlockSpec(wb.shape, lambda i: (0, 0)),
        ],
        out_specs=out_specs,
        out_shape=out_shape,
        compiler_params=_cparams(("parallel",), 48),
        name="sgu_out",
    )(gu, gv, w, bias, ln_g.reshape(1, SG_WIDTH), ln_b.reshape(1, SG_WIDTH), h, o_a, wa, wb)


def _mem_attn_kernel(q_ref, k_ref, v_ref, o_ref):
    for hd in range(MEM_HEADS):
        sl = slice(hd * MEM_HD, (hd + 1) * MEM_HD)
        q = q_ref[0, :, sl].astype(BF16)
        k = k_ref[0, :, sl].astype(BF16)
        v = v_ref[0, :, sl].astype(BF16)
        s = lax.dot_general(q, k, (((1,), (1,)), ((), ())), preferred_element_type=F32) * (MEM_HD ** -0.5)
        m = jnp.max(s, axis=-1, keepdims=True)
        e = jnp.exp(s - m)
        p = e / jnp.sum(e, axis=-1, keepdims=True)
        o_ref[0, :, sl] = jnp.dot(p.astype(BF16), v, preferred_element_type=F32)


def _mem_attn_cache_kernel(q_ref, k_ref, v_ref, o_ref):
    t = q_ref.shape[1]
    q_all = jnp.concatenate([q_ref[0, :, hd * MEM_HD:(hd + 1) * MEM_HD] for hd in range(MEM_HEADS)], axis=0)
    k2 = k_ref[...].reshape(MEM_LEN * MEM_HEADS, MEM_HD).astype(BF16)
    v2 = v_ref[...].reshape(MEM_LEN * MEM_HEADS, MEM_HD).astype(BF16)
    s = lax.dot_general(q_all.astype(BF16), k2, (((1,), (1,)), ((), ())),
                        preferred_element_type=F32) * (MEM_HD ** -0.5)
    q_head = lax.broadcasted_iota(I32, s.shape, 0) // t
    k_head = lax.broadcasted_iota(I32, s.shape, 1) % MEM_HEADS
    s = jnp.where(q_head == k_head, s, -jnp.inf)
    m = jnp.max(s, axis=-1, keepdims=True)
    e = jnp.exp(s - m)
    p = e / jnp.sum(e, axis=-1, keepdims=True)
    out = jnp.dot(p.astype(BF16), v2, preferred_element_type=F32)
    for hd in range(MEM_HEADS):
        o_ref[0, :, hd * MEM_HD:(hd + 1) * MEM_HD] = out[hd * t:(hd + 1) * t, :]


def _mem_attn(q, mk, mv, layer=None):
    b, t, d = q.shape
    tq = min(t, 512)
    if layer is None:
        kv = pl.BlockSpec((1, MEM_LEN, d), lambda i, j: (i, 0, 0))
        body = _mem_attn_kernel
    else:
        kv = pl.BlockSpec((None, None, MEM_LEN, MEM_HEADS, MEM_HD), lambda i, j: (layer, i, 0, 0, 0))
        body = _mem_attn_cache_kernel
    return pl.pallas_call(
        body,
        grid=(b, t // tq),
        in_specs=[pl.BlockSpec((1, tq, d), lambda i, j: (i, j, 0)), kv, kv],
        out_specs=pl.BlockSpec((1, tq, d), lambda i, j: (i, j, 0)),
        out_shape=jax.ShapeDtypeStruct((b, t, d), F32),
        compiler_params=_cparams(("parallel", "parallel")),
        name="mem_attn",
    )(q, mk, mv)


def _attn_layer_kernel(h_ref, g_ref, wq_ref, k_ref, v_ref, wo_ref, o_ref):
    x = h_ref[...]
    q = jnp.dot(_rms(x, g_ref[...]).astype(BF16), wq_ref[...], preferred_element_type=F32)
    heads = []
    for hd in range(MEM_HEADS):
        sl = slice(hd * MEM_HD, (hd + 1) * MEM_HD)
        k = k_ref[0, :, sl].astype(BF16)
        v = v_ref[0, :, sl].astype(BF16)
        s = lax.dot_general(q[:, sl].astype(BF16), k, (((1,), (1,)), ((), ())),
                            preferred_element_type=F32) * (MEM_HD ** -0.5)
        m = jnp.max(s, axis=-1, keepdims=True)
        e = jnp.exp(s - m)
        p = e / jnp.sum(e, axis=-1, keepdims=True)
        heads.append(jnp.dot(p.astype(BF16), v, preferred_element_type=F32))
    att = jnp.concatenate(heads, axis=-1)
    o_ref[...] = x + jnp.dot(att.astype(BF16), wo_ref[...], preferred_element_type=F32)


def _attn_layer(h, g, wq, mk, mv, wo, b, t):
    d = h.shape[1]
    tq = 512
    nj = t // tq
    row = pl.BlockSpec((tq, d), lambda i, j: (i * nj + j, 0))
    kv = pl.BlockSpec((1, MEM_LEN, d), lambda i, j: (i, 0, 0))
    full = pl.BlockSpec((d, d), lambda i, j: (0, 0))
    return pl.pallas_call(
        _attn_layer_kernel,
        grid=(b, nj),
        in_specs=[row, pl.BlockSpec((1, d), lambda i, j: (0, 0)), full, kv, kv, full],
        out_specs=row,
        out_shape=jax.ShapeDtypeStruct(h.shape, F32),
        compiler_params=_cparams(("parallel", "parallel"), 48),
        name="attn_layer",
    )(h, g.reshape(1, d), wq, mk, mv, wo)


def _top_rows(s, n_top, rank=None, payload=None):
    if rank is None:
        rank = lax.broadcasted_iota(I32, s.shape, 0)
    big = jnp.int32(2 ** 30)
    vals, ids, pays = [], [], []
    for _ in range(n_top):
        m = jnp.max(s, axis=0, keepdims=True)
        i = jnp.min(jnp.where(s == m, rank, big), axis=0, keepdims=True)
        hit = rank == i
        vals.append(m)
        ids.append(i)
        if payload is not None:
            pays.append(jnp.max(jnp.where(hit, payload, -1), axis=0, keepdims=True))
        s = jnp.where(hit, -jnp.inf, s)
    out = [jnp.concatenate(vals, axis=0), jnp.concatenate(ids, axis=0)]
    if payload is not None:
        out.append(jnp.concatenate(pays, axis=0))
    return out


_PAIR_GROUPS = ((0, 0), (0, 8), (1, 0), (2, 0), (3, 0), (4, 0), (5, 0), (6, 0), (7, 0))


def _pair_candidates(sv0, si0, sv1, si1):
    sub = lax.broadcasted_iota(I32, (SUBLANES,) + sv0.shape[1:], 0)
    cand, flat, eid = [], [], []
    for a, b0 in _PAIR_GROUPS:
        cand.append(sv0[a:a + 1, :] + sv1[b0:b0 + SUBLANES, :])
        flat.append(a * PEER_TOPK + b0 + sub)
        eid.append(si0[a:a + 1, :] * PEER_NKEYS + si1[b0:b0 + SUBLANES, :])
    cand.append(sv0[SUBLANES:, :] + sv1[0:1, :])
    flat.append((SUBLANES + sub) * PEER_TOPK)
    eid.append(si0[SUBLANES:, :] * PEER_NKEYS + si1[0:1, :])
    return jnp.concatenate(cand, axis=0), jnp.concatenate(flat, axis=0), jnp.concatenate(eid, axis=0)


def _peer_route_kernel(x_ref, g_ref, wq_ref, keys_ref,
                       xn_ref, roff_ref, sh_ref, gate_ref, e_scr, g_scr):
    n = _rms(x_ref[...], g_ref[...])
    for s in range(ROW_TILES):
        xn_ref[:, s, :] = n[:, s * LANES:(s + 1) * LANES]
    q = jnp.dot(n.astype(BF16), wq_ref[...], preferred_element_type=F32)
    for hd in range(PEER_HEADS):
        qh = q[:, hd * PEER_DKEY:(hd + 1) * PEER_DKEY].astype(BF16)
        tops = []
        for p in range(2):
            s_t = lax.dot_general(keys_ref[hd, p], qh, (((1,), (1,)), ((), ())),
                                  preferred_element_type=F32)
            tops.append(_top_rows(s_t, PEER_TOPK))
        (sv0, si0), (sv1, si1) = tops
        cand, flat, ecand = _pair_candidates(sv0, si0, sv1, si1)
        cv, _, ce = _top_rows(cand, PEER_TOPK, rank=flat, payload=ecand)
        ex = jnp.exp(cv - cv[0:1, :])
        gate = ex / jnp.sum(ex, axis=0, keepdims=True)
        e_scr[hd * PEER_TOPK:(hd + 1) * PEER_TOPK, :] = ce
        g_scr[hd * PEER_TOPK:(hd + 1) * PEER_TOPK, :] = gate
    e_t = pltpu.bitcast(pltpu.bitcast(e_scr[...], F32).T, I32)
    hi = e_t >= HALF_EXPERTS
    roff_ref[...] = jnp.where(hi, e_t - HALF_EXPERTS, e_t) * ROW_TILES
    sh_ref[...] = jnp.where(hi, 0, 16)
    gate_ref[...] = g_scr[...].T


def _peer_route(h, g, wq, keys_pad):
    n_tok, d = h.shape
    tb = LANES
    row = pl.BlockSpec((tb, d), lambda i: (i, 0))
    slot = pl.BlockSpec((tb, PEER_SLOTS), lambda i: (i, 0))
    return pl.pallas_call(
        _peer_route_kernel,
        grid=(n_tok // tb,),
        in_specs=[
            row,
            pl.BlockSpec((1, d), lambda i: (0, 0)),
            pl.BlockSpec(wq.shape, lambda i: (0, 0)),
            pl.BlockSpec(keys_pad.shape, lambda i: (0, 0, 0, 0)),
        ],
        out_specs=[pl.BlockSpec((tb, ROW_TILES, LANES), lambda i: (i, 0, 0)), slot, slot, slot],
        out_shape=[
            jax.ShapeDtypeStruct((n_tok, ROW_TILES, LANES), F32),
            jax.ShapeDtypeStruct((n_tok, PEER_SLOTS), I32),
            jax.ShapeDtypeStruct((n_tok, PEER_SLOTS), I32),
            jax.ShapeDtypeStruct((n_tok, PEER_SLOTS), F32),
        ],
        scratch_shapes=[pltpu.VMEM((PEER_SLOTS, tb), I32), pltpu.VMEM((PEER_SLOTS, tb), F32)],
        compiler_params=_cparams(("parallel",), 48),
        name="peer_route",
    )(h, g.reshape(1, d), wq, keys_pad)


def _expert_tile(tab_ref, roff, shv):
    w = tab_ref[pl.ds(pl.multiple_of(roff, SUBLANES), SUBLANES), :]
    return pltpu.bitcast(w << shv, F32)


def _fold_rows(ps):
    sub = lax.broadcasted_iota(I32, (SUBLANES, LANES), 0)
    dist = SUBLANES // 2
    while len(ps) > 1:
        lo = (sub % (2 * dist)) < dist
        half = len(ps) // 2
        nxt = []
        for a in range(half):
            x, y = ps[a], ps[a + half]
            nxt.append(jnp.where(lo, x, pltpu.roll(y, dist, 0))
                       + jnp.where(lo, pltpu.roll(x, SUBLANES - dist, 0), y))
        ps = nxt
        dist //= 2
    return ps[0]


def _slot_columns(rows8, dst_ref):
    for u in range(SUBLANES):
        dst_ref[u] = jnp.broadcast_to(rows8[u:u + 1, :], (PEER_SLOTS, LANES)).T


def _bcast_row(ref, u, k):
    return jnp.broadcast_to(ref[u, k:k + 1, :], (SUBLANES, LANES))


def _peer_token_loop(tb, prep, token_group):
    n_pairs = tb // (2 * SUBLANES)
    prep(0, 0)

    def pair(i, carry):
        g0 = 2 * i
        prep(g0 + 1, 1)
        token_group(g0, 0)
        prep(jnp.minimum(g0 + 2, 2 * n_pairs - 1), 0)
        token_group(g0 + 1, 1)
        return carry

    lax.fori_loop(0, n_pairs, pair, 0)


def _peer_up_kernel(roff_ref, sh_ref, gate_ref, x_ref, tab_ref, c_ref, shs_a, shs_b):
    tb = sh_ref.shape[0]
    shs = (shs_a, shs_b)

    def prep(grp, par):
        r0 = pl.multiple_of(grp * SUBLANES, SUBLANES)
        _slot_columns(pltpu.bitcast(sh_ref[pl.ds(r0, SUBLANES), :], F32), shs[par])

    def token_group(grp, par):
        rows = []
        for u in range(SUBLANES):
            t = grp * SUBLANES + u
            x_t = x_ref[t]
            tiles = []
            for g in range(PEER_SLOTS // SUBLANES):
                ps = []
                for jj in range(SUBLANES):
                    k = g * SUBLANES + jj
                    shv = pltpu.bitcast(_bcast_row(shs[par], u, k), U32)
                    ps.append(_expert_tile(tab_ref, roff_ref[t, k], shv) * x_t)
                tiles.append(_fold_rows(ps))
            r = jnp.concatenate(tiles, axis=0)
            rows.append(jnp.sum(r.T, axis=0, keepdims=True))
        out = pl.ds(pl.multiple_of(grp * SUBLANES, SUBLANES), SUBLANES)
        c_ref[out, :] = gate_ref[out, :] * _gelu(jnp.concatenate(rows, axis=0))

    _peer_token_loop(tb, prep, token_group)


def _peer_down_kernel(roff_hbm, sh_ref, sh_next, c_ref, c_next, tab_ref, o_ref,
                      idx_a, idx_b, sh_a, sh_b, c_a, c_b, sem):
    i = pl.program_id(0)
    n = pl.num_programs(0)
    tb = sh_ref.shape[0]
    th = tb // 2
    idx = (idx_a, idx_b)

    def idx_copy(step, half):
        row0 = pl.multiple_of(step * tb + half * th, th)
        return pltpu.make_async_copy(roff_hbm.at[pl.ds(row0, th)], idx[half], sem.at[half])

    def build(src_sh, src_c, row, sh_t, c_t, u, after=None):
        sh_row = src_sh[row:row + 1, :]
        c_row = src_c[row:row + 1, :]
        if after is not None:
            zero = lax.shift_right_logical(pltpu.bitcast(after[0:1, :], U32), jnp.uint32(32))
            sh_row = sh_row + pltpu.bitcast(zero, I32)
            c_row = c_row + pltpu.bitcast(zero, F32)
        sh_t[u] = jnp.broadcast_to(pltpu.bitcast(sh_row, F32), (PEER_SLOTS, LANES)).T
        c_t[u] = jnp.broadcast_to(c_row, (PEER_SLOTS, LANES)).T

    def consume(half, sh_t, c_t, u):
        accs = [jnp.zeros((SUBLANES, LANES), F32) for _ in range(2)]
        for k in range(PEER_SLOTS):
            shv = pltpu.bitcast(_bcast_row(sh_t, u, k), U32)
            accs[k % 2] = accs[k % 2] + _expert_tile(tab_ref, idx[half][u, k], shv) * _bcast_row(c_t, u, k)
        acc = accs[0] + accs[1]
        o_ref[half * th + u] = acc
        return acc

    def half_step(half, use, fill, src_sh, src_c, row0):
        @pl.when(i >= 0)
        def _():
            acc = None
            for u in range(th):
                build(src_sh, src_c, row0 + u, fill[0], fill[1], u, after=acc)
                acc = consume(half, use[0], use[1], u)

    tiles = ((sh_a, c_a), (sh_b, c_b))

    @pl.when(i == 0)
    def _():
        idx_copy(0, 0).start()
        idx_copy(0, 1).start()
        for u in range(th):
            build(sh_ref, c_ref, u, sh_a, c_a, u)

    idx_copy(i, 0).wait()
    half_step(0, tiles[0], tiles[1], sh_ref, c_ref, th)

    @pl.when(i + 1 < n)
    def _():
        idx_copy(i + 1, 0).start()

    idx_copy(i, 1).wait()
    half_step(1, tiles[1], tiles[0], sh_next, c_next, 0)

    @pl.when(i + 1 < n)
    def _():
        idx_copy(i + 1, 1).start()


_PEER_TB = 64
_PEER_DOWN_TB = 64
_PEER_TABLE_MIB = HALF_EXPERTS * D_MODEL * 4 // 2 ** 20
_PEER_VMEM_MIB = min(_PEER_TABLE_MIB + 20, VMEM_MIB_V7X - 8)
_COL_TILES = pltpu.VMEM((SUBLANES, PEER_SLOTS, LANES), F32)


def _table_spec(tab):
    return pl.BlockSpec(tab.shape, lambda i: (0, 0), pipeline_mode=pl.Buffered(1))


def _peer_up(roff, sh, gate, x_tiles, tab):
    n_tok = roff.shape[0]
    tb = _PEER_TB
    slot = pl.BlockSpec((tb, PEER_SLOTS), lambda i: (i, 0))
    return pl.pallas_call(
        _peer_up_kernel,
        grid=(n_tok // tb,),
        in_specs=[
            pl.BlockSpec((tb, PEER_SLOTS), lambda i: (i, 0), memory_space=pltpu.SMEM),
            slot, slot,
            pl.BlockSpec((tb, ROW_TILES, LANES), lambda i: (i, 0, 0)),
            _table_spec(tab),
        ],
        out_specs=slot,
        out_shape=jax.ShapeDtypeStruct((n_tok, PEER_SLOTS), F32),
        scratch_shapes=[_COL_TILES] * 2,
        compiler_params=_cparams(("arbitrary",), _PEER_VMEM_MIB),
        name="peer_up",
    )(roff, sh, gate, x_tiles, tab)


def _peer_down(roff, sh, coef, tab):
    n_tok = roff.shape[0]
    tb = _PEER_DOWN_TB
    th = tb // 2
    n_steps = n_tok // tb
    cur = pl.BlockSpec((tb, PEER_SLOTS), lambda i: (i, 0))
    nxt = pl.BlockSpec((tb, PEER_SLOTS), lambda i: (jnp.minimum(i + 1, n_steps - 1), 0))
    half_tiles = pltpu.VMEM((th, PEER_SLOTS, LANES), F32)
    return pl.pallas_call(
        _peer_down_kernel,
        grid=(n_steps,),
        in_specs=[pl.BlockSpec(memory_space=pl.ANY), cur, nxt, cur, nxt, _table_spec(tab)],
        out_specs=pl.BlockSpec((tb, ROW_TILES, LANES), lambda i: (i, 0, 0)),
        out_shape=jax.ShapeDtypeStruct((n_tok, ROW_TILES, LANES), F32),
        scratch_shapes=[pltpu.SMEM((th, PEER_SLOTS), I32)] * 2 + [half_tiles] * 4
        + [pltpu.SemaphoreType.DMA((2,))],
        compiler_params=_cparams(("arbitrary",), _PEER_VMEM_MIB),
        name="peer_down",
    )(roff, sh, sh, coef, coef, tab)


def _pack_kernel(lo_ref, hi_ref, o_ref):
    low = pltpu.bitcast(pltpu.bitcast(lo_ref[...].astype(BF16).astype(F32), U32) >> 16, I32)
    bits = pltpu.bitcast(hi_ref[...], I32)
    sign = bits & jnp.int32(-2 ** 31)
    mag = jnp.minimum(bits & jnp.int32(0x7FFFFFFF), jnp.int32(0x7F7F0000))
    top = jnp.maximum(mag + jnp.int32(0x8000) - low, 0) & jnp.int32(-65536)
    word = pltpu.bitcast(sign | top | low, U32)
    for s in range(ROW_TILES):
        o_ref[:, s, :] = word[:, s * LANES:(s + 1) * LANES]


def _pack_table(tabs, layer, rows=512):
    d = tabs.shape[2]
    n_blk = HALF_EXPERTS // rows
    packed = pl.pallas_call(
        _pack_kernel,
        grid=(n_blk,),
        in_specs=[pl.BlockSpec((None, rows, d), lambda i: (layer, i, 0)),
                  pl.BlockSpec((None, rows, d), lambda i: (layer, i + n_blk, 0))],
        out_specs=pl.BlockSpec((rows, ROW_TILES, LANES), lambda i: (i, 0, 0)),
        out_shape=jax.ShapeDtypeStruct((HALF_EXPERTS, ROW_TILES, LANES), U32),
        compiler_params=_cparams(("parallel",), 48),
        name="pack_table",
    )(tabs, tabs)
    return packed.reshape(HALF_EXPERTS * ROW_TILES, LANES)


def _peer(h, g, wq, keys_pad, u_tab, v_tab):
    xn, roff, sh, gate = _peer_route(h, g, wq, keys_pad)
    coef = _peer_up(roff, sh, gate, xn, u_tab)
    return _peer_down(roff, sh, coef, v_tab)


def _prep_layer(l, w_in, conv_w, a_log, dt_bias, o_norm_g, sg_ln_g, sg_ln_b, sg_w, sg_b, w_out,
                norm_mix_g, norm_mem_g, mem_norm_g, w_mq, w_mk, w_mv, w_mo, norm_ffn_g,
                peer_wq, peer_keys, peer_u, peer_v):
    o1 = QKV_DIM
    o2 = o1 + DN_V
    o4 = o2 + 2 * DN_HEADS
    o5 = o4 + SG_WIDTH
    wi = w_in[l]
    ba_cols = jnp.pad(wi[:, o2:o4], ((0, 0), (0, LANES - 2 * DN_HEADS)))
    w_in_r = jnp.concatenate([wi[:, :o2], wi[:, o4:], ba_cols], axis=1).astype(BF16)
    kz = jnp.zeros((PEER_HEADS, PEER_NKEYS, PEER_DKEY // 2), F32)
    keys_pad = jnp.stack([jnp.concatenate([peer_keys[l][:, 0], kz], axis=-1),
                          jnp.concatenate([kz, peer_keys[l][:, 1]], axis=-1)], axis=1).astype(BF16)
    return dict(
        w_in=w_in_r, conv_w=conv_w[l], a_log=a_log[l], dt_bias=dt_bias[l], o_norm_g=o_norm_g[l],
        sg_ln_g=sg_ln_g[l].reshape(-1), sg_ln_b=sg_ln_b[l].reshape(-1), sg_w=sg_w[l], sg_b=sg_b[l],
        w_out_a=w_out[l][:DN_V].astype(BF16), w_out_b=w_out[l][DN_V:].astype(BF16),
        norm_mix_g=norm_mix_g[l], norm_mem_g=norm_mem_g[l], mem_norm_g=mem_norm_g[l],
        w_mq=w_mq[l].astype(BF16), w_mkv=jnp.concatenate([w_mk[l], w_mv[l]], axis=1).astype(BF16),
        w_mo=w_mo[l].astype(BF16), norm_ffn_g=norm_ffn_g[l], peer_wq=peer_wq[l].astype(BF16),
        keys_pad=keys_pad, u_tab=_pack_table(peer_u, l), v_tab=_pack_table(peer_v, l),
    )


_IN_WIDTHS = (QKV_DIM, DN_V, SG_WIDTH, SG_WIDTH, LANES)


def _layer(h, delta, b, t, mk, mv, s0, c0, p, emit_vv, cache_layer=None):
    n_tok = b * t
    outs = _norm_proj(h, p["norm_mix_g"], p["w_in"], _IN_WIDTHS, delta=delta)
    if delta is not None:
        h, outs = outs[0], outs[1:]
    qkv, z, gu, gv, ba = outs
    o_a, s_new = _delta_mixer(qkv, z, ba, c0, s0, b, t, p["conv_w"], p["a_log"], p["dt_bias"], p["o_norm_g"])
    conv_new = qkv.reshape(b, t, QKV_DIM)[:, t - (DN_CONV - 1):, :]
    period = min(t, SG_CHUNK)
    reps = SG_CHUNK // period
    sg_w = jnp.tile(p["sg_w"][:, :period, :period], (1, reps, reps))
    sg_bias = jnp.broadcast_to(jnp.tile(p["sg_b"][:, :period], (1, reps))[:, :, None],
                               (SG_GROUPS, SG_CHUNK, SG_CH))
    sg_out = _sgu_out(gu, gv, sg_w, sg_bias, p["sg_ln_g"], p["sg_ln_b"], period, emit_vv,
                      h, o_a, p["w_out_a"], p["w_out_b"])
    h = sg_out[0]
    vv = sg_out[1] if emit_vv else None
    if cache_layer is None and t % 512 == 0:
        h = _attn_layer(h, p["norm_mem_g"], p["w_mq"], mk, mv, p["w_mo"], b, t)
    else:
        (q,) = _norm_proj(h, p["norm_mem_g"], p["w_mq"], (D_MODEL,))
        att = _mem_attn(q.reshape(b, t, D_MODEL), mk, mv, layer=cache_layer)
        h = _proj_residual(h, [att.reshape(n_tok, D_MODEL)], [p["w_mo"]])
    return h, s_new, conv_new, vv


def kernel(x_prompt, x_sample, state_delta, state_conv, cache_mem_k, cache_mem_v, mem_prompt, w_in, conv_w, a_log, dt_bias, o_norm_g, sg_ln_g, sg_ln_b, sg_w, sg_b, w_out, norm_mix_g, norm_mem_g, mem_norm_g, w_mq, w_mk, w_mv, w_mo, norm_ffn_g, peer_wq, peer_keys, peer_u, peer_v, final_norm_g):
    depth = w_in.shape[0]
    bp, tp, d = x_prompt.shape
    bs, ts, _ = x_sample.shape
    hp = x_prompt.reshape(bp * tp, d)
    hs = x_sample.reshape(bs * ts, d)
    dp = ds = None
    mem_flat = mem_prompt.reshape(bp * MEM_LEN, d)
    sd_p, sc_p, mk_p, mv_p, sd_s, sc_s, vr_s = [], [], [], [], [], [], []
    for l in range(depth):
        p = _prep_layer(l, w_in, conv_w, a_log, dt_bias, o_norm_g, sg_ln_g, sg_ln_b, sg_w, sg_b, w_out,
                        norm_mix_g, norm_mem_g, mem_norm_g, w_mq, w_mk, w_mv, w_mo, norm_ffn_g,
                        peer_wq, peer_keys, peer_u, peer_v)
        mk, mv = _norm_proj(mem_flat, p["mem_norm_g"], p["w_mkv"], (D_MODEL, D_MODEL))
        mk = mk.reshape(bp, MEM_LEN, d)
        mv = mv.reshape(bp, MEM_LEN, d)
        s0 = jnp.zeros((bp, DN_HEADS, DN_DK, DN_DV), F32)
        c0 = jnp.zeros((bp, DN_CONV - 1, QKV_DIM), F32)
        hp, s_p, c_p, _ = _layer(hp, dp, bp, tp, mk, mv, s0, c0, p, False)
        hs, s_s, c_s, vv = _layer(hs, ds, bs, ts, cache_mem_k, cache_mem_v, state_delta[l], state_conv[l], p, True,
                                  cache_layer=l)
        dp = _peer(hp, p["norm_ffn_g"], p["peer_wq"], p["keys_pad"], p["u_tab"], p["v_tab"])
        ds = _peer(hs, p["norm_ffn_g"], p["peer_wq"], p["keys_pad"], p["u_tab"], p["v_tab"])
        sd_p.append(s_p)
        sc_p.append(c_p)
        mk_p.append(mk.reshape(bp, MEM_LEN, MEM_HEADS, MEM_HD))
        mv_p.append(mv.reshape(bp, MEM_LEN, MEM_HEADS, MEM_HD))
        sd_s.append(s_s)
        sc_s.append(c_s)
        vr_s.append(vv.reshape(bs, ts, SG_WIDTH))
    y_prompt = _add_norm(hp, dp, final_norm_g).reshape(bp, tp, d)
    y_sample = _add_norm(hs, ds, final_norm_g).reshape(bs, ts, d)
    return (y_prompt, y_sample, jnp.stack(sd_p), jnp.stack(sc_p), jnp.stack(mk_p), jnp.stack(mv_p),
            jnp.stack(sd_s), jnp.stack(sc_s), jnp.stack(vr_s))
```
